```python
import math
import numpy as np
import jax
import jax.numpy as jnp
from jax import lax

D_MODEL = 2048
BATCH = 16
SEQ = 2048
DEPTH = 1
DEC_BATCH = 32
DEC_SEQ = 1
PAST_LEN = 16384
PAGE_SIZE = 128

HEAD_DIM = 128
NSA_HEADS = 8
NSA_KV_HEADS = 2
NSA_GROUP = NSA_HEADS // NSA_KV_HEADS
NSA_WIDTH = NSA_HEADS * HEAD_DIM
NSA_KV_WIDTH = NSA_KV_HEADS * HEAD_DIM
CMP_LEN = 32
CMP_STRIDE = 16
CMP_R = CMP_LEN // CMP_STRIDE
CMP_HIDDEN = 128
SEL_BLOCK = 64
SEL_TOPK = 16
N_LOCAL_SEL = 2
WINDOW = 512
Q_BLOCK = 128
FORCE_SCORE = 1e4
ALIBI_MAX = 8.0
GDN_HEADS = 8
GDN_DK = 128
GDN_DV = 128
GDN_QK_WIDTH = GDN_HEADS * GDN_DK
GDN_V_WIDTH = GDN_HEADS * GDN_DV
GDN_CONV_CH = 2 * GDN_QK_WIDTH + GDN_V_WIDTH
CONV_W = 4
GDN_CHUNK = 64
EPS = 1e-6
SCALE = HEAD_DIM ** -0.5
SPLITS = (NSA_WIDTH, 2 * NSA_KV_WIDTH, 2 * NSA_KV_WIDTH, 2 * NSA_KV_WIDTH, 3 * NSA_HEADS, NSA_WIDTH,
          GDN_CONV_CH, GDN_HEADS, GDN_HEADS, GDN_V_WIDTH, D_MODEL, D_MODEL)
IN_WIDTH = sum(SPLITS)

kernel_name = "nsa_gated_deltanet_parallel_hybrid_step"


def rms_norm(x, w):
    xf = x.astype(jnp.float32)
    y = xf * lax.rsqrt(jnp.mean(xf * xf, axis=-1, keepdims=True) + EPS)
    return (y * w.astype(jnp.float32)).astype(x.dtype)


def l2_norm(x):
    return x * lax.rsqrt(jnp.sum(x * x, axis=-1, keepdims=True) + EPS)


def alibi_slopes():
    h = np.arange(1, NSA_HEADS + 1, dtype=np.float32)
    s = np.power(np.float32(2.0), -ALIBI_MAX * h / NSA_HEADS).astype(np.float32)
    return jnp.asarray(s, dtype=jnp.float32).reshape(NSA_KV_HEADS, NSA_GROUP)


def split_proj(h):
    bounds = np.cumsum(np.array(SPLITS))[:-1].tolist()
    return jnp.split(h, bounds, axis=-1)


def masked_softmax(s, mask, axis):
    s = jnp.where(mask, s, -jnp.inf)
    m = jnp.max(s, axis=axis, keepdims=True)
    m = jnp.where(jnp.isfinite(m), m, 0.0)
    e = jnp.where(mask, jnp.exp(s - m), 0.0)
    den = jnp.sum(e, axis=axis, keepdims=True)
    return e / jnp.where(den > 0, den, 1.0)


def dense_gqa(q, k, v, mask, dist, slopes):
    s = jnp.einsum('bqgrd,bkgd->bgrqk', q, k, preferred_element_type=jnp.float32) * SCALE
    s = s - slopes[None, :, :, None, None] * dist[:, None, None]
    p = masked_softmax(s, mask[:, None, None], axis=-1)
    o = jnp.einsum('bgrqk,bkgd->bqgrd', p.astype(v.dtype), v)
    return o, p


def nsa_project(q_a, kv_cmp, kv_slc, kv_win, q_norm_w, k_norm_slc_w, k_norm_win_w):
    B, T, _ = q_a.shape
    q = rms_norm(q_a.reshape(B, T, NSA_KV_HEADS, NSA_GROUP, HEAD_DIM), q_norm_w)
    kv_shape = (B, T, 2, NSA_KV_HEADS, HEAD_DIM)

    def knorm(kv, w):
        kv = kv.reshape(kv_shape)
        return jnp.stack([rms_norm(kv[:, :, 0], w), kv[:, :, 1]], axis=2)

    return q, kv_cmp.reshape(kv_shape), knorm(kv_slc, k_norm_slc_w), knorm(kv_win, k_norm_win_w)


def cmp_partials(rows, w1):
    b, n, g, d = rows.shape
    ch = rows.reshape(b, n // CMP_STRIDE, CMP_STRIDE, g, d)
    return jnp.einsum('bcsgd,rsde->bcrge', ch, w1)


def cmp_finish(part, b1, w2):
    n_cmp = part.shape[1] - CMP_R + 1
    h = sum(part[:, r:r + n_cmp, r] for r in range(CMP_R))
    return jnp.einsum('bngh,hd->bngd', jax.nn.silu(h + b1), w2)


def compress_partials(rows_kv, cw):
    return cmp_partials(rows_kv[:, :, 0], cw[0]), cmp_partials(rows_kv[:, :, 1], cw[3])


def compress_finish(pk, pv, cw):
    kc = rms_norm(cmp_finish(pk, cw[1], cw[2]), cw[6])
    vc = cmp_finish(pv, cw[4], cw[5])
    return kc, vc


def cmp_attend(q, kc, vc, qpos, slopes):
    kend = jnp.arange(kc.shape[1]) * CMP_STRIDE + CMP_LEN - 1
    d = qpos[:, None] - kend[None]
    return dense_gqa(q, kc, vc, (d >= 0)[None], d[None].astype(jnp.float32), slopes)


def sel_map(n_cmp, n_sel):
    cs = np.arange(n_cmp) * CMP_STRIDE
    ss = np.arange(n_sel) * SEL_BLOCK
    ov = np.minimum(cs[:, None] + CMP_LEN, ss[None] + SEL_BLOCK) - np.maximum(cs[:, None], ss[None])
    return jnp.asarray(np.clip(ov, 0, None).astype(np.float32) / CMP_LEN, dtype=jnp.float32)


def select_blocks(p_cmp, qpos, n_cmp, n_sel):
    imp = jnp.einsum('bgrqn,nj->bqgj', p_cmp, sel_map(n_cmp, n_sel))
    j = jnp.arange(n_sel)[None]
    cur = (qpos // SEL_BLOCK)[:, None]
    visible = (j * SEL_BLOCK <= qpos[:, None])[None, :, None, :]
    forced = ((j == 0) | ((cur - j >= 0) & (cur - j < N_LOCAL_SEL)))[None, :, None, :]
    score = jnp.where(visible, jnp.where(forced, FORCE_SCORE, imp), -FORCE_SCORE)
    _, idx = lax.top_k(score, min(SEL_TOPK, n_sel))
    valid = idx * SEL_BLOCK <= qpos[None, :, None, None]
    return idx, valid


def slc_attend(q, kg, vg, tok, valid, qpos, slopes):
    s = jnp.einsum('bqgrd,bqgnkd->bqgrnk', q, kg, preferred_element_type=jnp.float32) * SCALE
    dist = (qpos[None, :, None, None, None] - tok).astype(jnp.float32)
    s = s - slopes[None, None, :, :, None, None] * dist[:, :, :, None]
    mask = (valid[..., None] & (dist >= 0))[:, :, :, None]
    p = masked_softmax(s, mask, axis=(-2, -1))
    return jnp.einsum('bqgrnk,bqgnkd->bqgrd', p.astype(vg.dtype), vg)


def slc_prompt(q, kv_slc, idx, valid, slopes):
    B, T, G, R, D = q.shape
    nqb = T // Q_BLOCK

    def blocks(a):
        return a.reshape((B * nqb, Q_BLOCK) + a.shape[2:])

    bid = jnp.repeat(jnp.arange(B), nqb)
    qpos = jnp.tile(jnp.arange(T).reshape(nqb, Q_BLOCK), (B, 1))
    gidx = jnp.arange(G)[None, :, None, None]

    def step(args):
        qb, ib, vb, b, pos = args
        tok = ib[..., None] * SEL_BLOCK + jnp.arange(SEL_BLOCK)
        rows = kv_slc[b]
        kg = rows[tok, 0, gidx]
        vg = rows[tok, 1, gidx]
        return slc_attend(qb[None], kg[None], vg[None], tok[None], vb[None], pos, slopes)[0]

    o = lax.map(step, (blocks(q), blocks(idx), blocks(valid), bid, qpos))
    return o.reshape(B, T, G, R, D)


def slc_sample(q, kv_new, cache_slc, page_table, idx, valid, qpos, past_len, slopes):
    DB, DS, G = q.shape[0], q.shape[1], q.shape[2]
    tok = idx[..., None] * SEL_BLOCK + jnp.arange(SEL_BLOCK)
    b = jnp.arange(DB)[:, None, None, None, None]
    g = jnp.arange(G)[None, None, :, None, None]
    tp = jnp.minimum(tok, past_len - 1)
    phys = page_table[b, tp // PAGE_SIZE]
    off = tp % PAGE_SIZE
    tn = jnp.clip(tok - past_len, 0, DS - 1)
    in_past = (tok < past_len)[..., None]
    kg = jnp.where(in_past, cache_slc[phys, off, 0, g], kv_new[b, tn, 0, g])
    vg = jnp.where(in_past, cache_slc[phys, off, 1, g], kv_new[b, tn, 1, g])
    return slc_attend(q, kg, vg, tok, valid, qpos, slopes)


def win_prompt(q, kv_win, slopes):
    B, T, G, R, D = q.shape
    nqb = T // Q_BLOCK
    span = Q_BLOCK + WINDOW
    kv_pad = jnp.pad(kv_win, ((0, 0), (WINDOW, 0), (0, 0), (0, 0), (0, 0)))
    qb = jnp.swapaxes(q.reshape(B, nqb, Q_BLOCK, G, R, D), 0, 1)

    def step(args):
        i, qblk = args
        start = i * Q_BLOCK
        kvb = lax.dynamic_slice_in_dim(kv_pad, start, span, axis=1)
        kpos = start - WINDOW + jnp.arange(span)
        qpos = start + jnp.arange(Q_BLOCK)
        d = qpos[:, None] - kpos[None]
        mask = (d >= 0) & (d <= WINDOW) & (kpos >= 0)[None]
        o, _ = dense_gqa(qblk, kvb[:, :, 0], kvb[:, :, 1], mask[None], d[None].astype(jnp.float32), slopes)
        return o

    o = lax.map(step, (jnp.arange(nqb), qb))
    return jnp.swapaxes(o, 0, 1).reshape(B, T, G, R, D)


def nsa_prompt(q, kv_cmp, kv_slc, kv_win, cw, slopes):
    T = q.shape[1]
    qpos = jnp.arange(T)
    pk, pv = compress_partials(kv_cmp, cw)
    kc, vc = compress_finish(pk, pv, cw)
    o_cmp, p_cmp = cmp_attend(q, kc, vc, qpos, slopes)
    idx, valid = select_blocks(p_cmp, qpos, kc.shape[1], -(-T // SEL_BLOCK))
    o_slc = slc_prompt(q, kv_slc, idx, valid, slopes)
    o_win = win_prompt(q, kv_win, slopes)
    return o_cmp, o_slc, o_win


def nsa_sample(q, kv_cmp, kv_slc, kv_win, cache_cmp, cache_slc, st_win, page_table, cw, slopes):
    DB, DS = q.shape[0], q.shape[1]
    past_len = page_table.shape[1] * PAGE_SIZE
    qpos = past_len + jnp.arange(DS)
    past = cache_cmp[page_table].reshape(DB, past_len, 2, NSA_KV_HEADS, HEAD_DIM)
    pk_past, pv_past = compress_partials(past, cw)
    n_new = (DS // CMP_STRIDE) * CMP_STRIDE
    pk_new, pv_new = compress_partials(kv_cmp[:, :n_new], cw)
    kc, vc = compress_finish(jnp.concatenate([pk_past, pk_new], axis=1),
                             jnp.concatenate([pv_past, pv_new], axis=1), cw)
    o_cmp, p_cmp = cmp_attend(q, kc, vc, qpos, slopes)
    idx, valid = select_blocks(p_cmp, qpos, kc.shape[1], -(-(past_len + DS) // SEL_BLOCK))
    o_slc = slc_sample(q, kv_slc, cache_slc, page_table, idx, valid, qpos, past_len, slopes)
    wb = st_win.shape[1]
    kv_all = jnp.concatenate([st_win.astype(kv_win.dtype), kv_win], axis=1)
    kpos = past_len - wb + jnp.arange(wb + DS)
    d = qpos[:, None] - kpos[None]
    mask = (d >= 0) & (d <= WINDOW)
    o_win, _ = dense_gqa(q, kv_all[:, :, 0], kv_all[:, :, 1], mask[None], d[None].astype(jnp.float32), slopes)
    return o_cmp, o_slc, o_win, kv_all[:, -wb:]


def nsa_merge(o_cmp, o_slc, o_win, g_nsa, z_a, w_pa):
    B, T, _ = z_a.shape
    g = jax.nn.sigmoid(g_nsa.astype(jnp.float32)).reshape(B, T, 3, NSA_KV_HEADS, NSA_GROUP, 1)
    o = g[:, :, 0] * o_cmp + g[:, :, 1] * o_slc + g[:, :, 2] * o_win
    o = o.reshape(B, T, NSA_WIDTH).astype(z_a.dtype) * jax.nn.silu(z_a)
    return o @ w_pa


def causal_conv(buf, w):
    T = buf.shape[1] - (CONV_W - 1)
    return sum(buf[:, i:i + T] * w[i] for i in range(CONV_W))


def gdn_chunked(q, k, v, g, beta, S0):
    B, T, H, DK = q.shape
    C = GDN_CHUNK
    Tp = -(-T // C) * C
    nc = Tp // C

    def prep(a):
        a = jnp.pad(a, ((0, 0), (0, Tp - T)) + ((0, 0),) * (a.ndim - 2))
        a = a.reshape((B, nc, C) + a.shape[2:])
        return jnp.moveaxis(jnp.moveaxis(a, 1, 0), 3, 2)

    xs = tuple(prep(a) for a in (q, k, v, g, beta))
    tril = jnp.tril(jnp.ones((C, C), dtype=bool))
    stril = jnp.tril(jnp.ones((C, C), dtype=bool), -1)
    eye = jnp.eye(C, dtype=jnp.float32)

    def step(S, xc):
        qc, kc, vc, gc, bc = xc
        dec = jnp.cumsum(gc, axis=-1)
        L = jnp.exp(jnp.where(tril, dec[..., :, None] - dec[..., None, :], -jnp.inf))
        kb = kc * bc[..., None]
        vb = vc * bc[..., None]
        A = jnp.where(stril, jnp.einsum('bhik,bhjk->bhij', kb, kc) * L, 0.0)
        Tm = lax.linalg.triangular_solve(eye + A, jnp.broadcast_to(eye, A.shape),
                                         left_side=True, lower=True, unit_diagonal=True)
        u = Tm @ vb
        w = Tm @ (kb * jnp.exp(dec)[..., None])
        vn = u - w @ S
        att = jnp.where(tril, jnp.einsum('bhik,bhjk->bhij', qc, kc) * L, 0.0)
        o = (qc * jnp.exp(dec)[..., None]) @ S + att @ vn
        dl = dec[..., -1:]
        S = S * jnp.exp(dl)[..., None] + jnp.einsum('bhck,bhcv->bhkv', kc * jnp.exp(dl - dec)[..., None], vn)
        return S, o

    S, o = lax.scan(step, S0, xs)
    o = o.transpose(1, 0, 3, 2, 4).reshape(B, Tp, H, -1)[:, :T]
    return o, S


def gdn_branch(qkv, a, b, z, conv_buf, S0, conv_w, a_log, dt_bias, gdn_norm_w, w_pb):
    B, T, _ = qkv.shape
    buf = jnp.concatenate([conv_buf.astype(qkv.dtype), qkv], axis=1)
    c = jax.nn.silu(causal_conv(buf, conv_w)).astype(jnp.float32)
    qg, kg, vg = jnp.split(c, [GDN_QK_WIDTH, 2 * GDN_QK_WIDTH], axis=-1)
    q = l2_norm(qg.reshape(B, T, GDN_HEADS, GDN_DK)) * (GDN_DK ** -0.5)
    k = l2_norm(kg.reshape(B, T, GDN_HEADS, GDN_DK))
    v = vg.reshape(B, T, GDN_HEADS, GDN_DV)
    beta = jax.nn.sigmoid(b.astype(jnp.float32))
    g = -jnp.exp(a_log.astype(jnp.float32)) * jax.nn.softplus(a.astype(jnp.float32) + dt_bias.astype(jnp.float32))
    o, S = gdn_chunked(q, k, v, g, beta, S0.astype(jnp.float32))
    o = rms_norm(o, gdn_norm_w).reshape(B, T, GDN_V_WIDTH).astype(z.dtype) * jax.nn.silu(z)
    return o @ w_pb, S.astype(S0.dtype), buf[:, -(CONV_W - 1):]


def merge_residual(x, y_a, y_b, gate_a, gate_b, w_o):
    m = jax.nn.sigmoid(gate_a) * y_a + jax.nn.sigmoid(gate_b) * y_b
    return x + m @ w_o


def decoder_layer(x_p, x_s, cache_cmp, cache_slc, st_win, st_S, st_conv, page_table, w):
    (norm_w, w_in, q_norm_w, k_norm_cmp_w, k_norm_slc_w, k_norm_win_w,
     cmp_w1_k, cmp_b1_k, cmp_w2_k, cmp_w1_v, cmp_b1_v, cmp_w2_v,
     conv_w, a_log, dt_bias, gdn_norm_w, w_pa, w_pb, w_o) = w
    cw = (cmp_w1_k, cmp_b1_k, cmp_w2_k, cmp_w1_v, cmp_b1_v, cmp_w2_v, k_norm_cmp_w)
    gw = (conv_w, a_log, dt_bias, gdn_norm_w, w_pb)
    slopes = alibi_slopes()

    B, T, _ = x_p.shape
    (q_a, kvc, kvs, kvw, g_nsa, z_a, qkv_b, a_b, b_b, z_b, gate_a, gate_b) = split_proj(rms_norm(x_p, norm_w) @ w_in)
    q, kvc, kvs, kvw = nsa_project(q_a, kvc, kvs, kvw, q_norm_w, k_norm_slc_w, k_norm_win_w)
    o_cmp, o_slc, o_win = nsa_prompt(q, kvc, kvs, kvw, cw, slopes)
    y_a = nsa_merge(o_cmp, o_slc, o_win, g_nsa, z_a, w_pa)
    zero_conv = jnp.zeros((B, CONV_W - 1, GDN_CONV_CH), x_p.dtype)
    zero_S = jnp.zeros((B, GDN_HEADS, GDN_DK, GDN_DV), x_p.dtype)
    y_b, S_p, conv_p = gdn_branch(qkv_b, a_b, b_b, z_b, zero_conv, zero_S, *gw)
    y_p = merge_residual(x_p, y_a, y_b, gate_a, gate_b, w_o)
    win_p = kvw[:, -min(WINDOW, T):]

    (q_a, kvc_s, kvs_s, kvw_s, g_nsa, z_a, qkv_b, a_b, b_b, z_b, gate_a, gate_b) = split_proj(rms_norm(x_s, norm_w) @ w_in)
    q, kvc_s, kvs_s, kvw_s = nsa_project(q_a, kvc_s, kvs_s, kvw_s, q_norm_w, k_norm_slc_w, k_norm_win_w)
    o_cmp, o_slc, o_win, win_s = nsa_sample(q, kvc_s, kvs_s, kvw_s, cache_cmp, cache_slc, st_win, page_table, cw, slopes)
    y_a = nsa_merge(o_cmp, o_slc, o_win, g_nsa, z_a, w_pa)
    y_b, S_s, conv_s = gdn_branch(qkv_b, a_b, b_b, z_b, st_conv, st_S, *gw)
    y_s = merge_residual(x_s, y_a, y_b, gate_a, gate_b, w_o)
    return y_p, y_s, (kvc, kvs, win_p, S_p, conv_p, kvc_s, kvs_s, win_s, S_s, conv_s)


def setup_inputs(seed: int = 0) -> dict:
    key = jax.random.key(seed)
    ks = jax.random.split(key, 40)
    f32 = jnp.float32
    n_pages = PAST_LEN // PAGE_SIZE
    n_phys = (5 * DEC_BATCH * n_pages) // 4
    win_buf = min(WINDOW, PAST_LEN)

    def nrm(k, shape, scale):
        return scale * jax.random.normal(k, shape, f32)

    def gain(k, shape):
        return 1.0 + 0.02 * jax.random.normal(k, shape, f32)

    kv_pool = (DEPTH, n_phys, PAGE_SIZE, 2, NSA_KV_HEADS, HEAD_DIM)
    perm = jax.random.permutation(ks[7], n_phys)
    page_table = perm[: DEC_BATCH * n_pages].reshape(DEC_BATCH, n_pages).astype(jnp.int32)
    dt = jnp.exp(jax.random.uniform(ks[25], (DEPTH, GDN_HEADS), f32, math.log(1e-3), math.log(1e-1)))
    return {
        "x_prompt": nrm(ks[0], (BATCH, SEQ, D_MODEL), 1.0),
        "x_sample": nrm(ks[1], (DEC_BATCH, DEC_SEQ, D_MODEL), 1.0),
        "cache_cmp_kv": nrm(ks[2], kv_pool, 1.0),
        "cache_slc_kv": nrm(ks[3], kv_pool, 1.0),
        "state_win_kv": nrm(ks[4], (DEPTH, DEC_BATCH, win_buf, 2, NSA_KV_HEADS, HEAD_DIM), 1.0),
        "state_gdn_S": nrm(ks[5], (DEPTH, DEC_BATCH, GDN_HEADS, GDN_DK, GDN_DV), 0.1),
        "state_gdn_conv": nrm(ks[6], (DEPTH, DEC_BATCH, CONV_W - 1, GDN_CONV_CH), 1.0),
        "page_table": page_table,
        "norm_w": gain(ks[8], (DEPTH, D_MODEL)),
        "w_in": nrm(ks[9], (DEPTH, D_MODEL, IN_WIDTH), D_MODEL ** -0.5),
        "q_norm_w": gain(ks[10], (DEPTH, HEAD_DIM)),
        "k_norm_cmp_w": gain(ks[11], (DEPTH, HEAD_DIM)),
        "k_norm_slc_w": gain(ks[12], (DEPTH, HEAD_DIM)),
        "k_norm_win_w": gain(ks[13], (DEPTH, HEAD_DIM)),
        "cmp_w1_k": nrm(ks[14], (DEPTH, CMP_R, CMP_STRIDE, HEAD_DIM, CMP_HIDDEN), (CMP_LEN * HEAD_DIM) ** -0.5),
        "cmp_b1_k": nrm(ks[15], (DEPTH, CMP_HIDDEN), 0.02),
        "cmp_w2_k": nrm(ks[16], (DEPTH, CMP_HIDDEN, HEAD_DIM), CMP_HIDDEN ** -0.5),
        "cmp_w1_v": nrm(ks[17], (DEPTH, CMP_R, CMP_STRIDE, HEAD_DIM, CMP_HIDDEN), (CMP_LEN * HEAD_DIM) ** -0.5),
        "cmp_b1_v": nrm(ks[18], (DEPTH, CMP_HIDDEN), 0.02),
        "cmp_w2_v": nrm(ks[19], (DEPTH, CMP_HIDDEN, HEAD_DIM), CMP_HIDDEN ** -0.5),
        "conv_w": nrm(ks[20], (DEPTH, CONV_W, GDN_CONV_CH), CONV_W ** -0.5),
        "a_log": jnp.log(jax.random.uniform(ks[21], (DEPTH, GDN_HEADS), f32, 1.0, 16.0)),
        "dt_bias": dt + jnp.log(-jnp.expm1(-dt)),
        "gdn_norm_w": gain(ks[22], (DEPTH, GDN_DV)),
        "w_pa": nrm(ks[23], (DEPTH, NSA_WIDTH, D_MODEL), NSA_WIDTH ** -0.5),
        "w_pb": nrm(ks[24], (DEPTH, GDN_V_WIDTH, D_MODEL), GDN_V_WIDTH ** -0.5),
        "w_o": nrm(ks[26], (DEPTH, D_MODEL, D_MODEL), D_MODEL ** -0.5),
    }


def reference(x_prompt, x_sample, cache_cmp_kv, cache_slc_kv, state_win_kv, state_gdn_S, state_gdn_conv, page_table,
              norm_w, w_in, q_norm_w, k_norm_cmp_w, k_norm_slc_w, k_norm_win_w,
              cmp_w1_k, cmp_b1_k, cmp_w2_k, cmp_w1_v, cmp_b1_v, cmp_w2_v,
              conv_w, a_log, dt_bias, gdn_norm_w, w_pa, w_pb, w_o):
    weights = (norm_w, w_in, q_norm_w, k_norm_cmp_w, k_norm_slc_w, k_norm_win_w,
               cmp_w1_k, cmp_b1_k, cmp_w2_k, cmp_w1_v, cmp_b1_v, cmp_w2_v,
               conv_w, a_log, dt_bias, gdn_norm_w, w_pa, w_pb, w_o)
    y_p, y_s = x_prompt, x_sample
    per_layer = []
    for layer in range(DEPTH):
        y_p, y_s, st = decoder_layer(y_p, y_s, cache_cmp_kv[layer], cache_slc_kv[layer], state_win_kv[layer],
                                     state_gdn_S[layer], state_gdn_conv[layer], page_table,
                                     tuple(wt[layer] for wt in weights))
        per_layer.append(st)
    (cmp_p, slc_p, win_p, S_p, conv_p, cmp_s, slc_s, win_s, S_s, conv_s) = [jnp.stack(a) for a in zip(*per_layer)]
    return (y_p, y_s, cmp_p, slc_p, win_p, S_p, conv_p, cmp_s, slc_s, win_s, S_s, conv_s)
```

```python
import functools

import numpy as np
import jax
import jax.numpy as jnp
from jax import lax
from jax.experimental import pallas as pl
from jax.experimental.pallas import tpu as pltpu

F32 = jnp.float32
BF16 = jnp.bfloat16
HI = lax.Precision.HIGHEST

PAGE_SIZE = 128
HEAD_DIM = 128
NSA_HEADS = 8
NSA_KV_HEADS = 2
NSA_GROUP = NSA_HEADS // NSA_KV_HEADS
NSA_WIDTH = NSA_HEADS * HEAD_DIM
NSA_KV_WIDTH = NSA_KV_HEADS * HEAD_DIM
CMP_LEN = 32
CMP_STRIDE = 16
CMP_R = CMP_LEN // CMP_STRIDE
CMP_HIDDEN = 128
SEL_BLOCK = 64
SEL_TOPK = 16
N_LOCAL_SEL = 2
WINDOW = 512
Q_BLOCK = 128
FORCE_SCORE = 1e4
ALIBI_MAX = 8.0
GDN_HEADS = 8
GDN_DK = 128
GDN_DV = 128
GDN_QK_WIDTH = GDN_HEADS * GDN_DK
GDN_V_WIDTH = GDN_HEADS * GDN_DV
GDN_CONV_CH = 2 * GDN_QK_WIDTH + GDN_V_WIDTH
CONV_W = 4
GDN_CHUNK = 64
EPS = 1e-6
SCALE = HEAD_DIM ** -0.5
NEG = -1e30

C_Q = 0
C_ZA = 1024
C_ZB = 2048
C_QKVB = 3072
C_GA = 6144
C_GB = 8192
C_KVC = 10240
C_KVS = 10752
C_KVW = 11264
C_SM = 11776
H_WIDTH = 12288
TN = 512
SM_A = 3 * NSA_HEADS
SM_B = SM_A + GDN_HEADS

LANE = 128
VMEM_LIMIT = 56 * 1024 * 1024


def _cparams(sem, vmem=VMEM_LIMIT):
    return pltpu.CompilerParams(dimension_semantics=sem, vmem_limit_bytes=vmem)


def _silu(x):
    return x * jax.nn.sigmoid(x)


def _rms(a, w):
    return a * lax.rsqrt(jnp.mean(a * a, axis=-1, keepdims=True) + EPS) * w


def _l2(a):
    return a * lax.rsqrt(jnp.sum(a * a, axis=-1, keepdims=True) + EPS)


def _dot(a, b):
    return jnp.dot(a, b, preferred_element_type=F32)


def _dot_nt(a, b):
    return lax.dot_general(a, b, (((1,), (1,)), ((), ())), preferred_element_type=F32)


def _dot_hi(a, b):
    return jnp.dot(a, b, preferred_element_type=F32, precision=HI)


def _softplus(x):
    return jnp.maximum(x, 0.0) + jnp.log1p(jnp.exp(-jnp.abs(x)))


def _masked_softmax(s, mask):
    sm = jnp.where(mask, s, NEG)
    m = jnp.max(sm, axis=-1, keepdims=True)
    e = jnp.where(mask, jnp.exp(sm - m), 0.0)
    den = jnp.sum(e, axis=-1, keepdims=True)
    return e / jnp.where(den > 0, den, 1.0)


def _proj_kernel(x_ref, nw_ref, w_ref, hw_ref, o_ref, xn_ref):
    j = pl.program_id(1)

    @pl.when(j == 0)
    def _():
        x = x_ref[...]
        y = x * lax.rsqrt(jnp.mean(x * x, axis=-1, keepdims=True) + EPS)
        xn_ref[...] = (y * nw_ref[...]).astype(BF16)

    acc = _dot(xn_ref[...], w_ref[...])

    def normed(n_groups, w):
        parts = [_rms(acc[:, i * LANE:(i + 1) * LANE], w) for i in range(n_groups)]
        if n_groups * LANE < TN:
            parts.append(acc[:, n_groups * LANE:])
        return jnp.concatenate(parts, axis=1)

    is_q = j < (C_ZA // TN)
    is_ks = j == (C_KVS // TN)
    is_kw = j == (C_KVW // TN)

    @pl.when(is_q)
    def _():
        o_ref[...] = normed(TN // LANE, hw_ref[0:1, :])

    @pl.when(is_ks)
    def _():
        o_ref[...] = normed(NSA_KV_HEADS, hw_ref[1:2, :])

    @pl.when(is_kw)
    def _():
        o_ref[...] = normed(NSA_KV_HEADS, hw_ref[2:3, :])

    @pl.when(jnp.logical_not(is_q | is_ks | is_kw))
    def _():
        o_ref[...] = acc


def _proj(x2d, norm_w, w_perm, head_w, tm):
    n, d = x2d.shape
    return pl.pallas_call(
        _proj_kernel,
        grid=(n // tm, H_WIDTH // TN),
        in_specs=[pl.BlockSpec((tm, d), lambda i, j: (i, 0)),
                  pl.BlockSpec((1, d), lambda i, j: (0, 0)),
                  pl.BlockSpec((d, TN), lambda i, j: (0, j)),
                  pl.BlockSpec((3, LANE), lambda i, j: (0, 0))],
        out_specs=pl.BlockSpec((tm, TN), lambda i, j: (i, j)),
        out_shape=jax.ShapeDtypeStruct((n, H_WIDTH), F32),
        scratch_shapes=[pltpu.VMEM((tm, d), BF16)],
        compiler_params=_cparams(("parallel", "arbitrary")),
        name="in_proj",
    )(x2d, norm_w.reshape(1, d), w_perm, head_w)


def _cmp_partials_body(strided_rows, w1k_ref, w1v_ref, pk_ref, pv_ref, n_chunks):
    for g in range(NSA_KV_HEADS):
        for kv, (w1_ref, p_ref) in enumerate(((w1k_ref, pk_ref), (w1v_ref, pv_ref))):
            acc = jnp.zeros((n_chunks, CMP_R * CMP_HIDDEN), F32)
            for s in range(CMP_STRIDE):
                xs = strided_rows(kv * NSA_KV_HEADS + g, s)
                acc = acc + _dot(xs.astype(BF16), w1_ref[s])
            p_ref[0, g] = acc


def _cmp_part_prompt_kernel(r0_ref, r1_ref, r2_ref, r3_ref, w1k_ref, w1v_ref, pk_ref, pv_ref, *, n_chunks):
    refs = (r0_ref, r1_ref, r2_ref, r3_ref)

    def strided_rows(c, s):
        return refs[c][pl.ds(s, n_chunks, stride=CMP_STRIDE), :]

    _cmp_partials_body(strided_rows, w1k_ref, w1v_ref, pk_ref, pv_ref, n_chunks)


def _cmp_part_prompt(h, w1k, w1v, bsz, t):
    n_chunks = t // CMP_STRIDE
    pshape = jax.ShapeDtypeStruct((bsz, NSA_KV_HEADS, n_chunks, CMP_R * CMP_HIDDEN), F32)
    pspec = pl.BlockSpec((1, NSA_KV_HEADS, n_chunks, CMP_R * CMP_HIDDEN), lambda b: (b, 0, 0, 0))
    wspec = pl.BlockSpec((CMP_STRIDE, HEAD_DIM, CMP_R * CMP_HIDDEN), lambda b: (0, 0, 0))
    return pl.pallas_call(
        functools.partial(_cmp_part_prompt_kernel, n_chunks=n_chunks),
        grid=(bsz,),
        in_specs=[pl.BlockSpec((t, HEAD_DIM), functools.partial(lambda c, b: (b, C_KVC // HEAD_DIM + c), c))
                  for c in range(2 * NSA_KV_HEADS)] + [wspec, wspec],
        out_specs=[pspec, pspec],
        out_shape=[pshape, pshape],
        compiler_params=_cparams(("parallel",)),
        name="cmp_part_prompt",
    )(h, h, h, h, w1k, w1v)


def _cmp_part_sample_kernel(pt_ref, cache_ref, w1k_ref, w1v_ref, pk_ref, pv_ref, buf_ref, sem_ref,
                            *, pages_per_step, n_pages):
    b = pl.program_id(0)
    st = pl.program_id(1)

    n_chunks = pages_per_step * PAGE_SIZE // CMP_STRIDE
    n_col = 2 * NSA_KV_HEADS

    def page_copy(p, c):
        page = pt_ref[b * n_pages + st * pages_per_step + p]
        return pltpu.make_async_copy(cache_ref.at[page, :, pl.ds(c * HEAD_DIM, HEAD_DIM)],
                                     buf_ref.at[c, pl.ds(p * PAGE_SIZE, PAGE_SIZE)], sem_ref)

    for p in range(pages_per_step):
        for c in range(n_col):
            page_copy(p, c).start()
    for p in range(pages_per_step):
        for c in range(n_col):
            page_copy(p, c).wait()

    def strided_rows(c, s):
        return buf_ref[c, pl.ds(s, n_chunks, stride=CMP_STRIDE), :]

    _cmp_partials_body(strided_rows, w1k_ref, w1v_ref, pk_ref, pv_ref, n_chunks)


def _cmp_part_sample(cache2d, page_table, w1k, w1v):
    dbsz, n_pages = page_table.shape
    pages_per_step = min(16, n_pages)
    n_steps = n_pages // pages_per_step
    cps = pages_per_step * PAGE_SIZE // CMP_STRIDE
    n_chunks = n_pages * PAGE_SIZE // CMP_STRIDE
    pshape = jax.ShapeDtypeStruct((dbsz, NSA_KV_HEADS, n_chunks, CMP_R * CMP_HIDDEN), F32)
    pspec = pl.BlockSpec((1, NSA_KV_HEADS, cps, CMP_R * CMP_HIDDEN), lambda b, s, pt: (b, 0, s, 0))
    wspec = pl.BlockSpec((CMP_STRIDE, HEAD_DIM, CMP_R * CMP_HIDDEN), lambda b, s, pt: (0, 0, 0))
    grid_spec = pltpu.PrefetchScalarGridSpec(
        num_scalar_prefetch=1,
        grid=(dbsz, n_steps),
        in_specs=[pl.BlockSpec(memory_space=pl.ANY), wspec, wspec],
        out_specs=[pspec, pspec],
        scratch_shapes=[pltpu.VMEM((2 * NSA_KV_HEADS, pages_per_step * PAGE_SIZE, HEAD_DIM), F32),
                        pltpu.SemaphoreType.DMA(())],
    )
    return pl.pallas_call(
        functools.partial(_cmp_part_sample_kernel, pages_per_step=pages_per_step, n_pages=n_pages),
        grid_spec=grid_spec,
        out_shape=[pshape, pshape],
        compiler_params=_cparams(("arbitrary", "arbitrary")),
        name="cmp_part_sample",
    )(page_table.reshape(-1), cache2d, w1k, w1v)


def _cmp_finish_kernel(pk_ref, pv_ref, b1k_ref, w2k_ref, b1v_ref, w2v_ref, nk_ref, kc_ref, vc_ref,
                       *, n_chunks):
    for g in range(NSA_KV_HEADS):
        for kv, (p_ref, b1_ref, w2_ref, o_ref) in enumerate(((pk_ref, b1k_ref, w2k_ref, kc_ref),
                                                            (pv_ref, b1v_ref, w2v_ref, vc_ref))):
            part = p_ref[0, g]
            nxt = pltpu.roll(part[:, CMP_HIDDEN:], n_chunks - 1, 0)
            hid = _silu(part[:, :CMP_HIDDEN] + nxt + b1_ref[...])
            out = _dot(hid.astype(BF16), w2_ref[...])
            if kv == 0:
                out = _rms(out, nk_ref[...])
            o_ref[0, g] = out


def _cmp_finish(pk, pv, b1k, w2k, b1v, w2v, nk):
    bsz, _, n_chunks, _ = pk.shape
    pspec = pl.BlockSpec((1, NSA_KV_HEADS, n_chunks, CMP_R * CMP_HIDDEN), lambda b: (b, 0, 0, 0))
    vspec = pl.BlockSpec((1, CMP_HIDDEN), lambda b: (0, 0))
    wspec = pl.BlockSpec((CMP_HIDDEN, HEAD_DIM), lambda b: (0, 0))
    oshape = jax.ShapeDtypeStruct((bsz, NSA_KV_HEADS, n_chunks, HEAD_DIM), F32)
    ospec = pl.BlockSpec((1, NSA_KV_HEADS, n_chunks, HEAD_DIM), lambda b: (b, 0, 0, 0))
    return pl.pallas_call(
        functools.partial(_cmp_finish_kernel, n_chunks=n_chunks),
        grid=(bsz,),
        in_specs=[pspec, pspec, vspec, wspec, vspec, wspec, vspec],
        out_specs=[ospec, ospec],
        out_shape=[oshape, oshape],
        compiler_params=_cparams(("parallel",)),
        name="cmp_finish",
    )(pk, pv, b1k.reshape(1, -1), w2k, b1v.reshape(1, -1), w2v, nk.reshape(1, -1))


def _sel_map(n_cmp, n_sel, n_sel_pad):
    cs = np.arange(n_cmp) * CMP_STRIDE
    ss = np.arange(n_sel) * SEL_BLOCK
    ov = np.minimum(cs[:, None] + CMP_LEN, ss[None] + SEL_BLOCK) - np.maximum(cs[:, None], ss[None])
    m = np.zeros((n_cmp, n_sel_pad), np.float32)
    m[:, :n_sel] = np.clip(ov, 0, None).astype(np.float32) / CMP_LEN
    return jnp.asarray(m, dtype=BF16)


def _topk_rank_rows(score, n_valid):
    j = lax.broadcasted_iota(jnp.int32, score.shape, 1)
    cnt = jnp.zeros(score.shape, F32)
    for jp in range(n_valid):
        col = score[:, jp:jp + 1]
        earlier = (j > jp).astype(F32)
        cnt = cnt + jnp.where(col > score, 1.0, jnp.where(col == score, earlier, 0.0))
    return cnt


SLC_CHUNK = 256


def _attn_prompt_kernel(q_ref, ks_ref, vs_ref, kw_ref, vw_ref, kc_ref, vc_ref, sm_ref, za_ref,
                        selmap_ref, exp_ref, o_ref, *, t_len):
    g = pl.program_id(1)
    qb = pl.program_id(2)
    qstart = qb * Q_BLOCK
    n_sel = t_len // SEL_BLOCK
    rows = NSA_GROUP * Q_BLOCK

    q = q_ref[...]
    qs = jnp.concatenate([q[:, r * HEAD_DIM:(r + 1) * HEAD_DIM] for r in range(NSA_GROUP)],
                         axis=0).astype(BF16)
    row = lax.broadcasted_iota(jnp.int32, (rows, 1), 0)
    r_of = row // Q_BLOCK
    qpos = qstart + row % Q_BLOCK
    slope = jnp.zeros((rows, 1), F32)
    for r in range(NSA_GROUP):
        s_r = jnp.where(g == 0, 2.0 ** -(r + 1.0), 2.0 ** -(r + 1.0 + NSA_GROUP)).astype(F32)
        slope = jnp.where(r_of == r, s_r, slope)

    kc = kc_ref[0, 0].astype(BF16)
    vc = vc_ref[0, 0].astype(BF16)
    n_cmp = kc.shape[0]
    kend = lax.broadcasted_iota(jnp.int32, (1, n_cmp), 1) * CMP_STRIDE + (CMP_LEN - 1)
    d = qpos - kend
    s = _dot_nt(qs, kc) * SCALE - slope * d.astype(F32)
    p_cmp = _masked_softmax(s, d >= 0).astype(BF16)
    o_cmp = _dot(p_cmp, vc)
    imp4 = _dot(p_cmp, selmap_ref[...])
    imp = imp4[0:Q_BLOCK]
    for r in range(1, NSA_GROUP):
        imp = imp + imp4[r * Q_BLOCK:(r + 1) * Q_BLOCK]

    tq = qstart + lax.broadcasted_iota(jnp.int32, (Q_BLOCK, 1), 0)
    j = lax.broadcasted_iota(jnp.int32, (Q_BLOCK, LANE), 1)
    visible = j * SEL_BLOCK <= tq
    back = tq // SEL_BLOCK - j
    forced = (j == 0) | ((back >= 0) & (back < N_LOCAL_SEL))
    score = jnp.where(visible, jnp.where(forced, FORCE_SCORE, imp), -FORCE_SCORE)
    score = jnp.where(j < n_sel, score, -3e38)
    rank = _topk_rank_rows(score, n_sel)
    sel = jnp.where((rank < min(SEL_TOPK, n_sel)) & visible & (j < n_sel), 1.0, 0.0).astype(BF16)
    sel4 = jnp.concatenate([sel] * NSA_GROUP, axis=0)

    def slc_step(c, carry):
        m, l, acc = carry
        k0 = pl.multiple_of(c * SLC_CHUNK, SLC_CHUNK)
        k = ks_ref[pl.ds(k0, SLC_CHUNK), :].astype(BF16)
        v = vs_ref[pl.ds(k0, SLC_CHUNK), :].astype(BF16)
        kpos = k0 + lax.broadcasted_iota(jnp.int32, (1, SLC_CHUNK), 1)
        dd = qpos - kpos
        in_sel = _dot(sel4, exp_ref[c])
        mask = (in_sel > 0.5) & (dd >= 0)
        sc = _dot_nt(qs, k) * SCALE - slope * dd.astype(F32)
        sc = jnp.where(mask, sc, NEG)
        m_new = jnp.maximum(m, jnp.max(sc, axis=-1, keepdims=True))
        alpha = jnp.exp(m - m_new)
        p = jnp.where(mask, jnp.exp(sc - m_new), 0.0)
        l = alpha * l + jnp.sum(p, axis=-1, keepdims=True)
        acc = alpha * acc + _dot(p.astype(BF16), v)
        return m_new, l, acc

    n_chunks = (qstart + Q_BLOCK + SLC_CHUNK - 1) // SLC_CHUNK
    m0 = jnp.full((rows, 1), NEG, F32)
    l0 = jnp.zeros((rows, 1), F32)
    a0 = jnp.zeros((rows, HEAD_DIM), F32)
    _, l_s, acc_s = lax.fori_loop(0, n_chunks, slc_step, (m0, l0, a0))
    o_slc = acc_s / jnp.where(l_s > 0, l_s, 1.0)

    span = Q_BLOCK + WINDOW
    wstart = pl.multiple_of(jnp.maximum(qstart - WINDOW, 0), Q_BLOCK)
    kw = kw_ref[pl.ds(wstart, span), :].astype(BF16)
    vw = vw_ref[pl.ds(wstart, span), :].astype(BF16)
    kposw = wstart + lax.broadcasted_iota(jnp.int32, (1, span), 1)
    dw = qpos - kposw
    sw = _dot_nt(qs, kw) * SCALE - slope * dw.astype(F32)
    p_win = _masked_softmax(sw, (dw >= 0) & (dw <= WINDOW)).astype(BF16)
    o_win = _dot(p_win, vw)

    sm = sm_ref[...]
    outs = []
    for r in range(NSA_GROUP):
        tot = jnp.zeros((Q_BLOCK, HEAD_DIM), F32)
        for br, o_br in enumerate((o_cmp, o_slc, o_win)):
            c0 = br * NSA_HEADS + r
            gate = jnp.where(g == 0, sm[:, c0:c0 + 1], sm[:, c0 + NSA_GROUP:c0 + NSA_GROUP + 1])
            tot = tot + jax.nn.sigmoid(gate) * o_br[r * Q_BLOCK:(r + 1) * Q_BLOCK]
        outs.append(tot * _silu(za_ref[:, r * HEAD_DIM:(r + 1) * HEAD_DIM]))
    o_ref[...] = jnp.concatenate(outs, axis=1)


def _attn_prompt(h, kc, vc, bsz, t):
    nqb = t // Q_BLOCK
    n_cmp = t // CMP_STRIDE
    n_sel = t // SEL_BLOCK
    assert n_sel <= LANE and t >= Q_BLOCK + WINDOW and t % SLC_CHUNK == 0
    selmap = _sel_map(n_cmp, n_sel, LANE)
    kk = np.arange(t)
    e = (np.arange(LANE)[:, None] == (kk // SEL_BLOCK)[None]).astype(np.float32)
    expand = jnp.asarray(e.reshape(LANE, t // SLC_CHUNK, SLC_CHUNK).transpose(1, 0, 2), dtype=BF16)
    gw = NSA_GROUP * HEAD_DIM
    row_blk = lambda b, g, i: b * nqb + i
    kv_spec = lambda c0: pl.BlockSpec((t, HEAD_DIM), lambda b, g, i: (b, c0 // HEAD_DIM + g))
    return pl.pallas_call(
        functools.partial(_attn_prompt_kernel, t_len=t),
        grid=(bsz, NSA_KV_HEADS, nqb),
        in_specs=[pl.BlockSpec((Q_BLOCK, gw), lambda b, g, i: (row_blk(b, g, i), C_Q // gw + g)),
                  kv_spec(C_KVS), kv_spec(C_KVS + NSA_KV_WIDTH),
                  kv_spec(C_KVW), kv_spec(C_KVW + NSA_KV_WIDTH),
                  pl.BlockSpec((1, 1, n_cmp, HEAD_DIM), lambda b, g, i: (b, g, 0, 0)),
                  pl.BlockSpec((1, 1, n_cmp, HEAD_DIM), lambda b, g, i: (b, g, 0, 0)),
                  pl.BlockSpec((Q_BLOCK, TN), lambda b, g, i: (row_blk(b, g, i), C_SM // TN)),
                  pl.BlockSpec((Q_BLOCK, gw), lambda b, g, i: (row_blk(b, g, i), C_ZA // gw + g)),
                  pl.BlockSpec((n_cmp, LANE), lambda b, g, i: (0, 0)),
                  pl.BlockSpec((t // SLC_CHUNK, LANE, SLC_CHUNK), lambda b, g, i: (0, 0, 0))],
        out_specs=pl.BlockSpec((Q_BLOCK, gw), lambda b, g, i: (row_blk(b, g, i), g)),
        out_shape=jax.ShapeDtypeStruct((bsz * t, NSA_WIDTH), F32),
        compiler_params=_cparams(("parallel", "parallel", "arbitrary")),
        name="attn_prompt",
    )(h, h, h, h, h, kc, vc, h, h, selmap, expand)


GDN_PRE_TILE = 256


def _gdn_pre_kernel(x_ref, cw_ref, o_ref, *, t_len):
    j = pl.program_id(1)
    x = x_ref[...]
    t = lax.broadcasted_iota(jnp.int32, (t_len, 1), 0)
    acc = None
    for i in range(CONV_W):
        dly = CONV_W - 1 - i
        xs = x if dly == 0 else jnp.where(t >= dly, pltpu.roll(x, dly, 0), 0.0)
        term = xs * cw_ref[i:i + 1, :]
        acc = term if acc is None else acc + term
    c = _silu(acc)
    n_grp = GDN_PRE_TILE // GDN_DK
    q_tiles = GDN_QK_WIDTH // GDN_PRE_TILE

    @pl.when(j < q_tiles)
    def _():
        o_ref[...] = jnp.concatenate(
            [_l2(c[:, i * GDN_DK:(i + 1) * GDN_DK]) * (GDN_DK ** -0.5) for i in range(n_grp)], axis=1)

    @pl.when((j >= q_tiles) & (j < 2 * q_tiles))
    def _():
        o_ref[...] = jnp.concatenate([_l2(c[:, i * GDN_DK:(i + 1) * GDN_DK]) for i in range(n_grp)], axis=1)

    @pl.when(j >= 2 * q_tiles)
    def _():
        o_ref[...] = c


def _gdn_pre(h, conv_w, bsz, t):
    n_tiles = GDN_CONV_CH // GDN_PRE_TILE
    return pl.pallas_call(
        functools.partial(_gdn_pre_kernel, t_len=t),
        grid=(bsz, n_tiles),
        in_specs=[pl.BlockSpec((t, GDN_PRE_TILE), lambda b, j: (b, C_QKVB // GDN_PRE_TILE + j)),
                  pl.BlockSpec((CONV_W, GDN_PRE_TILE), lambda b, j: (0, j))],
        out_specs=pl.BlockSpec((t, GDN_PRE_TILE), lambda b, j: (b, j)),
        out_shape=jax.ShapeDtypeStruct((bsz * t, GDN_CONV_CH), F32),
        compiler_params=_cparams(("parallel", "parallel")),
        name="gdn_pre",
    )(h, conv_w)


def _gdn_chunk_kernel(q_ref, k_ref, v_ref, sm_ref, zb_ref, alog_ref, dtb_ref, gnw_ref,
                      o_ref, s_out_ref, s_ref):
    c = pl.program_id(1)
    n_c = pl.num_programs(1)
    cs = GDN_CHUNK

    @pl.when(c == 0)
    def _():
        s_ref[...] = jnp.zeros_like(s_ref)

    sm = sm_ref[...]
    a = sm[:, SM_A:SM_A + GDN_HEADS]
    bb = sm[:, SM_B:SM_B + GDN_HEADS]
    gcol = -jnp.exp(alog_ref[...]) * _softplus(a + dtb_ref[...])
    beta = jax.nn.sigmoid(bb)
    ri = lax.broadcasted_iota(jnp.int32, (cs, cs), 0)
    ci = lax.broadcasted_iota(jnp.int32, (cs, cs), 1)
    tril = ri >= ci
    stril = ri > ci
    lo = tril.astype(F32)
    up = stril.astype(F32)
    eye = (ri == ci).astype(F32)
    dec = _dot_hi(lo, gcol)

    outs = []
    for hh in range(GDN_HEADS):
        sl = slice(hh * GDN_DK, (hh + 1) * GDN_DK)
        q = q_ref[:, sl]
        k = k_ref[:, sl]
        v = v_ref[:, sl]
        g_h = gcol[:, hh:hh + 1]
        dec_h = dec[:, hh:hh + 1]
        b_h = beta[:, hh:hh + 1]
        dmat = _dot_hi(lo, g_h * up)
        lmat = jnp.where(tril, jnp.exp(dmat), 0.0)
        kb = k * b_h
        vb = v * b_h
        k_b = k.astype(BF16)
        amat = jnp.where(stril, _dot_nt(kb.astype(BF16), k_b) * lmat, 0.0)
        npow = -amat
        tm = eye + npow
        for _ in range(5):
            npow = _dot_hi(npow, npow)
            tm = tm + _dot_hi(tm, npow)
        tm_b = tm.astype(BF16)
        edec = jnp.exp(dec_h)
        u = _dot(tm_b, vb.astype(BF16))
        w = _dot(tm_b, (kb * edec).astype(BF16))
        s_h = s_ref[hh]
        s_b = s_h.astype(BF16)
        vn = u - _dot(w.astype(BF16), s_b)
        att = jnp.where(tril, _dot_nt(q.astype(BF16), k_b) * lmat, 0.0)
        vn_b = vn.astype(BF16)
        o = _dot((q * edec).astype(BF16), s_b) + _dot(att.astype(BF16), vn_b)
        dl = dec_h[cs - 1:cs, :]
        kd = (k * jnp.exp(dl - dec_h)).astype(BF16)
        s_new = s_h * jnp.exp(dl) + lax.dot_general(kd, vn_b, (((0,), (0,)), ((), ())),
                                                    preferred_element_type=F32)
        s_ref[hh] = s_new
        outs.append(_rms(o, gnw_ref[...]) * _silu(zb_ref[:, sl]))
    o_ref[...] = jnp.concatenate(outs, axis=1)

    @pl.when(c == n_c - 1)
    def _():
        s_out_ref[0] = s_ref[...]


def _gdn_chunk(qkvn, h, a_log, dt_bias, gnw, bsz, t):
    nc = t // GDN_CHUNK
    w = GDN_QK_WIDTH
    row = lambda b, c: b * nc + c
    vec = pl.BlockSpec((1, GDN_HEADS), lambda b, c: (0, 0))
    return pl.pallas_call(
        _gdn_chunk_kernel,
        grid=(bsz, nc),
        in_specs=[pl.BlockSpec((GDN_CHUNK, w), lambda b, c: (row(b, c), 0)),
                  pl.BlockSpec((GDN_CHUNK, w), lambda b, c: (row(b, c), 1)),
                  pl.BlockSpec((GDN_CHUNK, w), lambda b, c: (row(b, c), 2)),
                  pl.BlockSpec((GDN_CHUNK, TN), lambda b, c: (row(b, c), C_SM // TN)),
                  pl.BlockSpec((GDN_CHUNK, w), lambda b, c: (row(b, c), C_ZB // w)),
                  vec, vec,
                  pl.BlockSpec((1, GDN_DV), lambda b, c: (0, 0))],
        out_specs=[pl.BlockSpec((GDN_CHUNK, w), lambda b, c: (row(b, c), 0)),
                   pl.BlockSpec((1, GDN_HEADS, GDN_DK, GDN_DV), lambda b, c: (b, 0, 0, 0))],
        out_shape=[jax.ShapeDtypeStruct((bsz * t, GDN_V_WIDTH), F32),
                   jax.ShapeDtypeStruct((bsz, GDN_HEADS, GDN_DK, GDN_DV), F32)],
        scratch_shapes=[pltpu.VMEM((GDN_HEADS, GDN_DK, GDN_DV), F32)],
        compiler_params=_cparams(("parallel", "arbitrary")),
        name="gdn_chunk",
    )(qkvn, qkvn, qkvn, h, h, a_log.reshape(1, -1), dt_bias.reshape(1, -1), gnw.reshape(1, -1))


def _out_kernel(oa_ref, ob_ref, ga_ref, gb_ref, x_ref, wpa_ref, wpb_ref, wo_ref, y_ref):
    ya = _dot(oa_ref[...].astype(BF16), wpa_ref[...])
    yb = _dot(ob_ref[...].astype(BF16), wpb_ref[...])
    m = jax.nn.sigmoid(ga_ref[...]) * ya + jax.nn.sigmoid(gb_ref[...]) * yb
    y_ref[...] = x_ref[...] + _dot(m.astype(BF16), wo_ref[...])


def _out_proj(oa, ob, h, x2d, w_pa, w_pb, w_o, tm):
    n, d = x2d.shape
    once = pl.Buffered(1)
    return pl.pallas_call(
        _out_kernel,
        grid=(n // tm,),
        in_specs=[pl.BlockSpec((tm, NSA_WIDTH), lambda i: (i, 0)),
                  pl.BlockSpec((tm, GDN_V_WIDTH), lambda i: (i, 0)),
                  pl.BlockSpec((tm, d), lambda i: (i, C_GA // d)),
                  pl.BlockSpec((tm, d), lambda i: (i, C_GB // d)),
                  pl.BlockSpec((tm, d), lambda i: (i, 0)),
                  pl.BlockSpec((NSA_WIDTH, d), lambda i: (0, 0), pipeline_mode=once),
                  pl.BlockSpec((GDN_V_WIDTH, d), lambda i: (0, 0), pipeline_mode=once),
                  pl.BlockSpec((d, d), lambda i: (0, 0), pipeline_mode=once)],
        out_specs=pl.BlockSpec((tm, d), lambda i: (i, 0)),
        out_shape=jax.ShapeDtypeStruct((n, d), F32),
        compiler_params=_cparams(("parallel",)),
        name="out_proj",
    )(oa, ob, h, h, x2d, w_pa, w_pb, w_o)


def _group_slopes(g):
    r_of = lax.broadcasted_iota(jnp.int32, (NSA_GROUP, 1), 0)
    slope = jnp.zeros((NSA_GROUP, 1), F32)
    for r in range(NSA_GROUP):
        s_r = jnp.where(g == 0, 2.0 ** -(r + 1.0), 2.0 ** -(r + 1.0 + NSA_GROUP)).astype(F32)
        slope = jnp.where(r_of == r, s_r, slope)
    return slope


def _attn_sample1_kernel(q_ref, kc_ref, vc_ref, selmap_ref, stw_ref, kvw_ref,
                         ocmp_ref, owin_ref, idx_ref, *, past_len, n_sel, n_sel_pad):
    qpos = past_len
    n_cmp = kc_ref.shape[2]
    wb = stw_ref.shape[1]
    rowi = lax.broadcasted_iota(jnp.int32, (LANE, n_sel_pad), 0)
    jrow = lax.broadcasted_iota(jnp.int32, (1, n_sel_pad), 1)
    scores = []
    for g in range(NSA_KV_HEADS):
        slope = _group_slopes(g)
        qg = q_ref[0, g].astype(BF16)
        kend = lax.broadcasted_iota(jnp.int32, (1, n_cmp), 1) * CMP_STRIDE + (CMP_LEN - 1)
        d = qpos - kend
        s = _dot_nt(qg, kc_ref[0, g].astype(BF16)) * SCALE - slope * d.astype(F32)
        p = _masked_softmax(s, d >= 0).astype(BF16)
        ocmp_ref[0, g] = _dot(p, vc_ref[0, g].astype(BF16))
        imp = jnp.sum(_dot(p, selmap_ref[...]), axis=0, keepdims=True)
        visible = jrow * SEL_BLOCK <= qpos
        back = qpos // SEL_BLOCK - jrow
        forced = (jrow == 0) | ((back >= 0) & (back < N_LOCAL_SEL))
        sc = jnp.where(visible, jnp.where(forced, FORCE_SCORE, imp), -FORCE_SCORE)
        scores.append(jnp.where(jrow < n_sel, sc, -3e38))
        col = g * HEAD_DIM
        kw = stw_ref[0, :, col:col + HEAD_DIM].astype(BF16)
        vw = stw_ref[0, :, NSA_KV_WIDTH + col:NSA_KV_WIDTH + col + HEAD_DIM].astype(BF16)
        knew = kvw_ref[0, g:g + 1, :].astype(BF16).astype(F32)
        vnew = kvw_ref[0, NSA_KV_HEADS + g:NSA_KV_HEADS + g + 1, :].astype(BF16).astype(F32)
        dwin = wb - lax.broadcasted_iota(jnp.int32, (1, wb), 1)
        sw = _dot_nt(qg, kw) * SCALE - slope * dwin.astype(F32)
        ok = (dwin >= 0) & (dwin <= WINDOW)
        sw = jnp.where(ok, sw, NEG)
        s_new = jnp.sum(qg.astype(F32) * knew, axis=-1, keepdims=True) * SCALE
        m = jnp.maximum(jnp.max(sw, axis=-1, keepdims=True), s_new)
        e = jnp.where(ok, jnp.exp(sw - m), 0.0)
        e_new = jnp.exp(s_new - m)
        den = jnp.sum(e, axis=-1, keepdims=True) + e_new
        owin_ref[0, g] = (_dot((e / den).astype(BF16), vw)
                          + (e_new / den).astype(BF16).astype(F32) * vnew)

    stacked = jnp.zeros((LANE, n_sel_pad), F32)
    for g in range(NSA_KV_HEADS):
        stacked = jnp.where(rowi == g, scores[g], stacked)
    cols = stacked.T
    ii = lax.broadcasted_iota(jnp.int32, (n_sel_pad, n_sel_pad), 0)
    jj = lax.broadcasted_iota(jnp.int32, (n_sel_pad, n_sel_pad), 1)
    earlier = (ii < jj).astype(F32)
    kcol = lax.broadcasted_iota(jnp.int32, (SEL_TOPK, 1), 0).astype(F32)
    for g in range(NSA_KV_HEADS):
        ci = cols[:, g:g + 1]
        rj = scores[g]
        beats = jnp.where(ci > rj, 1.0, jnp.where(ci == rj, earlier, 0.0))
        rank = jnp.sum(beats, axis=0, keepdims=True)
        onehot = rank == kcol
        idx = jnp.sum(jnp.where(onehot, jrow.astype(F32), 0.0), axis=-1, keepdims=True)
        idx_ref[0, g] = idx.astype(jnp.int32)


def _attn_sample1(q_s, kc_s, vc_s, st_win2d, kvw_new, past_len):
    dbsz = q_s.shape[0]
    n_cmp = kc_s.shape[2]
    n_sel = -(-(past_len + 1) // SEL_BLOCK)
    n_sel_pad = -(-n_sel // LANE) * LANE
    assert n_sel >= SEL_TOPK
    selmap = _sel_map(n_cmp, n_sel, n_sel_pad)
    wb = st_win2d.shape[1]
    hshape = jax.ShapeDtypeStruct((dbsz, NSA_KV_HEADS, NSA_GROUP, HEAD_DIM), F32)
    hspec = pl.BlockSpec((1, NSA_KV_HEADS, NSA_GROUP, HEAD_DIM), lambda b: (b, 0, 0, 0))
    cspec = pl.BlockSpec((1, NSA_KV_HEADS, n_cmp, HEAD_DIM), lambda b: (b, 0, 0, 0))
    return pl.pallas_call(
        functools.partial(_attn_sample1_kernel, past_len=past_len, n_sel=n_sel, n_sel_pad=n_sel_pad),
        grid=(dbsz,),
        in_specs=[hspec, cspec, cspec,
                  pl.BlockSpec((n_cmp, n_sel_pad), lambda b: (0, 0)),
                  pl.BlockSpec((1, wb, 2 * NSA_KV_WIDTH), lambda b: (b, 0, 0)),
                  pl.BlockSpec((1, 2 * NSA_KV_HEADS, HEAD_DIM), lambda b: (b, 0, 0))],
        out_specs=[hspec, hspec,
                   pl.BlockSpec((1, NSA_KV_HEADS, SEL_TOPK, 1), lambda b: (b, 0, 0, 0))],
        out_shape=[hshape, hshape,
                   jax.ShapeDtypeStruct((dbsz, NSA_KV_HEADS, SEL_TOPK, 1), jnp.int32)],
        compiler_params=_cparams(("parallel",)),
        name="attn_sample_cmp_win",
    )(q_s, kc_s, vc_s, selmap, st_win2d, kvw_new)


def _attn_sample2_kernel(idx_ref, pt_ref, q_ref, blk_ref, kvn_ref, ocmp_ref, owin_ref, gate_ref, za_ref,
                         o_ref, m_ref, l_ref, acc_ref, *, past_len):
    b = pl.program_id(0)
    g = pl.program_id(1)
    k = pl.program_id(2)
    n_k = pl.num_programs(2)
    qpos = past_len
    n_past_blocks = past_len // SEL_BLOCK
    slope = _group_slopes(g)
    q = q_ref[0, 0].astype(BF16)

    @pl.when(k == 0)
    def _():
        knew = kvn_ref[0, 0, 0].astype(BF16).astype(F32)
        vnew = kvn_ref[0, 1, 0].astype(BF16).astype(F32)
        m_ref[...] = jnp.sum(q.astype(F32) * knew, axis=-1, keepdims=True) * SCALE
        l_ref[...] = jnp.ones_like(l_ref)
        acc_ref[...] = jnp.broadcast_to(vnew, acc_ref.shape)

    blk = idx_ref[(b * NSA_KV_HEADS + g) * SEL_TOPK + k]
    in_past = blk < n_past_blocks
    tile = blk_ref[0]
    kk = jnp.where(g == 0, tile[:, 0:HEAD_DIM], tile[:, HEAD_DIM:2 * HEAD_DIM]).astype(BF16)
    vv = jnp.where(g == 0, tile[:, NSA_KV_WIDTH:NSA_KV_WIDTH + HEAD_DIM],
                   tile[:, NSA_KV_WIDTH + HEAD_DIM:]).astype(BF16)
    tok = blk * SEL_BLOCK + lax.broadcasted_iota(jnp.int32, (1, SEL_BLOCK), 1)
    dd = qpos - tok
    mask = (dd >= 0) & in_past
    sc = _dot_nt(q, kk) * SCALE - slope * dd.astype(F32)
    sc = jnp.where(mask, sc, NEG)
    m_old = m_ref[...]
    m_new = jnp.maximum(m_old, jnp.max(sc, axis=-1, keepdims=True))
    alpha = jnp.exp(m_old - m_new)
    p = jnp.where(mask, jnp.exp(sc - m_new), 0.0)
    m_ref[...] = m_new
    l_ref[...] = alpha * l_ref[...] + jnp.sum(p, axis=-1, keepdims=True)
    acc_ref[...] = alpha * acc_ref[...] + _dot(p.astype(BF16), vv)

    @pl.when(k == n_k - 1)
    def _():
        o_slc = acc_ref[...] / l_ref[...]
        tot = (jax.nn.sigmoid(gate_ref[0, 0, 0]) * ocmp_ref[0, 0]
               + jax.nn.sigmoid(gate_ref[0, 1, 0]) * o_slc
               + jax.nn.sigmoid(gate_ref[0, 2, 0]) * owin_ref[0, 0])
        o_ref[0, 0] = tot * _silu(za_ref[0, 0])


def _attn_sample2(idx, page_table, q_s, cache_blk, kvs_new, o_cmp, o_win, gates, za_s, past_len):
    dbsz, n_pages = page_table.shape
    n_past_blocks = past_len // SEL_BLOCK
    per_page = PAGE_SIZE // SEL_BLOCK

    def blk_map(b, g, k, idx_ref, pt_ref):
        blk = jnp.minimum(idx_ref[(b * NSA_KV_HEADS + g) * SEL_TOPK + k], n_past_blocks - 1)
        return (pt_ref[b * n_pages + blk // per_page] * per_page + blk % per_page, 0, 0)

    head = pl.BlockSpec((1, 1, NSA_GROUP, HEAD_DIM), lambda b, g, k, i, p: (b, g, 0, 0))
    grid_spec = pltpu.PrefetchScalarGridSpec(
        num_scalar_prefetch=2,
        grid=(dbsz, NSA_KV_HEADS, SEL_TOPK),
        in_specs=[head,
                  pl.BlockSpec((1, SEL_BLOCK, 2 * NSA_KV_WIDTH), blk_map),
                  pl.BlockSpec((1, 2, 1, 1, HEAD_DIM), lambda b, g, k, i, p: (b, 0, g, 0, 0)),
                  head, head,
                  pl.BlockSpec((1, 3, 1, NSA_GROUP, 1), lambda b, g, k, i, p: (b, 0, g, 0, 0)),
                  head],
        out_specs=head,
        scratch_shapes=[pltpu.VMEM((NSA_GROUP, 1), F32), pltpu.VMEM((NSA_GROUP, 1), F32),
                        pltpu.VMEM((NSA_GROUP, HEAD_DIM), F32)],
    )
    return pl.pallas_call(
        functools.partial(_attn_sample2_kernel, past_len=past_len),
        grid_spec=grid_spec,
        out_shape=jax.ShapeDtypeStruct((dbsz, NSA_KV_HEADS, NSA_GROUP, HEAD_DIM), F32),
        compiler_params=_cparams(("arbitrary", "arbitrary", "arbitrary")),
        name="attn_sample_slc",
    )(idx.reshape(-1), page_table.reshape(-1), q_s, cache_blk, kvs_new, o_cmp, o_win, gates, za_s)


def _gdn_sample_kernel(x_ref, cst_ref, cw_ref, a_ref, b_ref, alog_ref, dtb_ref, s_ref, gnw_ref, zb_ref,
                       o_ref, s_out_ref):
    acc = None
    for i in range(CONV_W):
        row = cst_ref[0, i] if i < CONV_W - 1 else x_ref[0]
        term = row * cw_ref[i]
        acc = term if acc is None else acc + term
    c = _silu(acc)
    q = _l2(c[0:GDN_HEADS]) * (GDN_DK ** -0.5)
    k = _l2(c[GDN_HEADS:2 * GDN_HEADS])
    v = c[2 * GDN_HEADS:3 * GDN_HEADS]
    gdec = -jnp.exp(alog_ref[...]) * _softplus(a_ref[0] + dtb_ref[...])
    beta = jax.nn.sigmoid(b_ref[0])
    eg = jnp.exp(gdec)
    qk = jnp.sum(q * k, axis=-1, keepdims=True)
    pad = jnp.concatenate([q, k, jnp.zeros((LANE - 2 * GDN_HEADS, GDN_DK), F32)], axis=0)
    cols = pad.T
    outs = []
    for hh in range(GDN_HEADS):
        s_h = s_ref[0, hh]
        qcol = cols[:, hh:hh + 1]
        kcol = cols[:, GDN_HEADS + hh:GDN_HEADS + hh + 1]
        b_h = beta[hh:hh + 1]
        e_h = eg[hh:hh + 1]
        k_s = jnp.sum(kcol * s_h, axis=0, keepdims=True)
        q_s = jnp.sum(qcol * s_h, axis=0, keepdims=True)
        vn = b_h * v[hh:hh + 1] - (b_h * e_h) * k_s
        outs.append(e_h * q_s + qk[hh:hh + 1] * vn)
        s_out_ref[0, hh] = s_h * e_h + kcol * vn
    o = jnp.concatenate(outs, axis=0)
    o_ref[0] = _rms(o, gnw_ref[...]) * _silu(zb_ref[0])


def _gdn_sample(x_s, cst, cw, a_s, b_s, a_log, dt_bias, s0, gnw, zb_s):
    dbsz = x_s.shape[0]
    nh3 = 3 * GDN_HEADS
    col = pl.BlockSpec((1, GDN_HEADS, 1), lambda b: (b, 0, 0))
    par = pl.BlockSpec((GDN_HEADS, 1), lambda b: (0, 0))
    head = pl.BlockSpec((1, GDN_HEADS, GDN_DV), lambda b: (b, 0, 0))
    sspec = pl.BlockSpec((1, GDN_HEADS, GDN_DK, GDN_DV), lambda b: (b, 0, 0, 0))
    return pl.pallas_call(
        _gdn_sample_kernel,
        grid=(dbsz,),
        in_specs=[pl.BlockSpec((1, nh3, GDN_DK), lambda b: (b, 0, 0)),
                  pl.BlockSpec((1, CONV_W - 1, nh3, GDN_DK), lambda b: (b, 0, 0, 0)),
                  pl.BlockSpec((CONV_W, nh3, GDN_DK), lambda b: (0, 0, 0)),
                  col, col, par, par, sspec,
                  pl.BlockSpec((1, GDN_DV), lambda b: (0, 0)),
                  head],
        out_specs=[head, sspec],
        out_shape=[jax.ShapeDtypeStruct((dbsz, GDN_HEADS, GDN_DV), F32),
                   jax.ShapeDtypeStruct(s0.shape, F32)],
        compiler_params=_cparams(("parallel",)),
        name="gdn_sample",
    )(x_s, cst, cw, a_s, b_s, a_log.reshape(-1, 1), dt_bias.reshape(-1, 1), s0, gnw.reshape(1, -1), zb_s)


def _permute_w_in(w_in):
    splits = (NSA_WIDTH, 2 * NSA_KV_WIDTH, 2 * NSA_KV_WIDTH, 2 * NSA_KV_WIDTH, 3 * NSA_HEADS, NSA_WIDTH,
              GDN_CONV_CH, GDN_HEADS, GDN_HEADS, GDN_V_WIDTH, w_in.shape[0], w_in.shape[0])
    bounds = np.cumsum(np.array(splits))[:-1].tolist()
    (q_a, kvc, kvs, kvw, g_nsa, z_a, qkv_b, a_b, b_b, z_b, gate_a, gate_b) = jnp.split(w_in, bounds, axis=1)
    small_pad = jnp.zeros((w_in.shape[0], H_WIDTH - C_SM - SM_B - GDN_HEADS), w_in.dtype)
    return jnp.concatenate([q_a, z_a, z_b, qkv_b, gate_a, gate_b, kvc, kvs, kvw, g_nsa, a_b, b_b, small_pad],
                           axis=1).astype(BF16)


def _cmp_w1(w1):
    return jnp.transpose(w1, (1, 2, 0, 3)).reshape(CMP_STRIDE, HEAD_DIM, CMP_R * CMP_HIDDEN).astype(BF16)


def _layer(x_p, x_s, cache_cmp, cache_slc, st_win, st_s, st_conv, page_table, w):
    (norm_w, w_in, q_norm_w, k_norm_cmp_w, k_norm_slc_w, k_norm_win_w,
     cmp_w1_k, cmp_b1_k, cmp_w2_k, cmp_w1_v, cmp_b1_v, cmp_w2_v,
     conv_w, a_log, dt_bias, gdn_norm_w, w_pa, w_pb, w_o) = w
    bsz, t, d = x_p.shape
    dbsz, dseq, _ = x_s.shape
    assert dseq == 1 and d == C_GB - C_GA
    n_pages = page_table.shape[1]
    past_len = n_pages * PAGE_SIZE

    w_perm = _permute_w_in(w_in)
    head_w = jnp.stack([q_norm_w, k_norm_slc_w, k_norm_win_w])
    w1k, w1v = _cmp_w1(cmp_w1_k), _cmp_w1(cmp_w1_v)
    w2k, w2v = cmp_w2_k.astype(BF16), cmp_w2_v.astype(BF16)
    wpa, wpb, wo = w_pa.astype(BF16), w_pb.astype(BF16), w_o.astype(BF16)
    kvw_cols = 2 * NSA_KV_WIDTH

    n = bsz * t
    x2 = x_p.reshape(n, d)
    h = _proj(x2, norm_w, w_perm, head_w, tm=min(1024, n))
    pk, pv = _cmp_part_prompt(h, w1k, w1v, bsz, t)
    kc, vc = _cmp_finish(pk, pv, cmp_b1_k, w2k, cmp_b1_v, w2v, k_norm_cmp_w)
    oa = _attn_prompt(h, kc, vc, bsz, t)
    qkvn = _gdn_pre(h, conv_w, bsz, t)
    ob, s_p = _gdn_chunk(qkvn, h, a_log, dt_bias, gdn_norm_w, bsz, t)
    y_p = _out_proj(oa, ob, h, x2, wpa, wpb, wo, tm=min(256, n)).reshape(bsz, t, d)

    kv_shape = (bsz, t, 2, NSA_KV_HEADS, HEAD_DIM)
    cmp_p = h[:, C_KVC:C_KVC + kvw_cols].reshape(kv_shape)
    slc_p = h[:, C_KVS:C_KVS + kvw_cols].reshape(kv_shape)
    wlen = min(WINDOW, t)
    win_p = h.reshape(bsz, t, H_WIDTH)[:, t - wlen:, C_KVW:C_KVW + kvw_cols].reshape(
        bsz, wlen, 2, NSA_KV_HEADS, HEAD_DIM)
    conv_p = h.reshape(bsz, t, H_WIDTH)[:, t - (CONV_W - 1):, C_QKVB:C_QKVB + GDN_CONV_CH]

    xs2 = x_s.reshape(dbsz, d)
    hs = _proj(xs2, norm_w, w_perm, head_w, tm=dbsz)
    grp = (dbsz, NSA_KV_HEADS, NSA_GROUP, HEAD_DIM)
    q_s = hs[:, C_Q:C_Q + NSA_WIDTH].reshape(grp)
    za_s = hs[:, C_ZA:C_ZA + NSA_WIDTH].reshape(grp)
    kvc_s = hs[:, C_KVC:C_KVC + kvw_cols]
    kvs_s = hs[:, C_KVS:C_KVS + kvw_cols]
    kvw_s = hs[:, C_KVW:C_KVW + kvw_cols]
    n_phys = cache_cmp.shape[0]
    pk_s, pv_s = _cmp_part_sample(cache_cmp.reshape(n_phys, PAGE_SIZE, kvw_cols), page_table, w1k, w1v)
    kc_s, vc_s = _cmp_finish(pk_s, pv_s, cmp_b1_k, w2k, cmp_b1_v, w2v, k_norm_cmp_w)
    wb = st_win.shape[1]
    o_cmp_s, o_win_s, idx = _attn_sample1(q_s, kc_s, vc_s, st_win.reshape(dbsz, wb, kvw_cols),
                                          kvw_s.reshape(dbsz, 2 * NSA_KV_HEADS, HEAD_DIM), past_len)
    gates_s = hs[:, C_SM:C_SM + 3 * NSA_HEADS].reshape(dbsz, 3, NSA_KV_HEADS, NSA_GROUP, 1)
    per_page = PAGE_SIZE // SEL_BLOCK
    oa_s = _attn_sample2(idx, page_table, q_s, cache_slc.reshape(n_phys * per_page, SEL_BLOCK, kvw_cols),
                         kvs_s.reshape(dbsz, 2, NSA_KV_HEADS, 1, HEAD_DIM), o_cmp_s, o_win_s, gates_s, za_s,
                         past_len)
    nh3 = 3 * GDN_HEADS
    ob_s, s_s = _gdn_sample(hs[:, C_QKVB:C_QKVB + GDN_CONV_CH].reshape(dbsz, nh3, GDN_DK),
                            st_conv.reshape(dbsz, CONV_W - 1, nh3, GDN_DK),
                            conv_w.reshape(CONV_W, nh3, GDN_DK),
                            hs[:, C_SM + SM_A:C_SM + SM_A + GDN_HEADS].reshape(dbsz, GDN_HEADS, 1),
                            hs[:, C_SM + SM_B:C_SM + SM_B + GDN_HEADS].reshape(dbsz, GDN_HEADS, 1),
                            a_log, dt_bias, st_s, gdn_norm_w,
                            hs[:, C_ZB:C_ZB + GDN_V_WIDTH].reshape(dbsz, GDN_HEADS, GDN_DV))
    y_s = _out_proj(oa_s.reshape(dbsz, NSA_WIDTH), ob_s.reshape(dbsz, GDN_V_WIDTH), hs, xs2,
                    wpa, wpb, wo, tm=dbsz).reshape(dbsz, 1, d)

    kv1 = (dbsz, 1, 2, NSA_KV_HEADS, HEAD_DIM)
    win_s = jnp.concatenate([st_win, kvw_s.reshape(kv1)], axis=1)[:, -wb:]
    conv_s = jnp.concatenate([st_conv, hs[:, None, C_QKVB:C_QKVB + GDN_CONV_CH]], axis=1)[:, -(CONV_W - 1):]
    return y_p, y_s, (cmp_p, slc_p, win_p, s_p, conv_p,
                      kvc_s.reshape(kv1), kvs_s.reshape(kv1), win_s, s_s, conv_s)


def kernel(x_prompt, x_sample, cache_cmp_kv, cache_slc_kv, state_win_kv, state_gdn_S, state_gdn_conv, page_table, norm_w, w_in, q_norm_w, k_norm_cmp_w, k_norm_slc_w, k_norm_win_w, cmp_w1_k, cmp_b1_k, cmp_w2_k, cmp_w1_v, cmp_b1_v, cmp_w2_v, conv_w, a_log, dt_bias, gdn_norm_w, w_pa, w_pb, w_o):
    weights = (norm_w, w_in, q_norm_w, k_norm_cmp_w, k_norm_slc_w, k_norm_win_w,
               cmp_w1_k, cmp_b1_k, cmp_w2_k, cmp_w1_v, cmp_b1_v, cmp_w2_v,
               conv_w, a_log, dt_bias, gdn_norm_w, w_pa, w_pb, w_o)
    depth = norm_w.shape[0]
    y_p, y_s = x_prompt, x_sample
    per_layer = []
    for layer in range(depth):
        y_p, y_s, st = _layer(y_p, y_s, cache_cmp_kv[layer], cache_slc_kv[layer], state_win_kv[layer],
                              state_gdn_S[layer], state_gdn_conv[layer], page_table,
                              tuple(wt[layer] for wt in weights))
        per_layer.append(st)
    stacked = [jnp.stack(a) for a in zip(*per_layer)]
    return (y_p, y_s, *stacked)
```

```python
import functools

import numpy as np
import jax
import jax.numpy as jnp
from jax import lax
from jax.experimental import pallas as pl
from jax.experimental.pallas import tpu as pltpu

F32 = jnp.float32
BF16 = jnp.bfloat16
HI = lax.Precision.HIGHEST

PAGE_SIZE = 128
HEAD_DIM = 128
NSA_HEADS = 8
NSA_KV_HEADS = 2
NSA_GROUP = NSA_HEADS // NSA_KV_HEADS
NSA_WIDTH = NSA_HEADS * HEAD_DIM
NSA_KV_WIDTH = NSA_KV_HEADS * HEAD_DIM
CMP_LEN = 32
CMP_STRIDE = 16
CMP_R = CMP_LEN // CMP_STRIDE
CMP_HIDDEN = 128
SEL_BLOCK = 64
SEL_TOPK = 16
N_LOCAL_SEL = 2
WINDOW = 512
Q_BLOCK = 128
FORCE_SCORE = 1e4
ALIBI_MAX = 8.0
GDN_HEADS = 8
GDN_DK = 128
GDN_DV = 128
GDN_QK_WIDTH = GDN_HEADS * GDN_DK
GDN_V_WIDTH = GDN_HEADS * GDN_DV
GDN_CONV_CH = 2 * GDN_QK_WIDTH + GDN_V_WIDTH
CONV_W = 4
GDN_CHUNK = 64
EPS = 1e-6
SCALE = HEAD_DIM ** -0.5
NEG = -1e30

C_Q = 0
C_ZA = 1024
C_ZB = 2048
C_QKVB = 3072
C_GA = 6144
C_GB = 8192
C_KVC = 10240
C_KVS = 10752
C_KVW = 11264
C_SM = 11776
H_WIDTH = 12288
TN = 512
SM_A = 3 * NSA_HEADS
SM_B = SM_A + GDN_HEADS

LANE = 128
VMEM_LIMIT = 56 * 1024 * 1024


def _cparams(sem, vmem=VMEM_LIMIT):
    return pltpu.CompilerParams(dimension_semantics=sem, vmem_limit_bytes=vmem)


def _silu(x):
    return x * jax.nn.sigmoid(x)


def _rms(a, w):
    return a * lax.rsqrt(jnp.mean(a * a, axis=-1, keepdims=True) + EPS) * w


def _l2(a):
    return a * lax.rsqrt(jnp.sum(a * a, axis=-1, keepdims=True) + EPS)


def _dot(a, b):
    return jnp.dot(a, b, preferred_element_type=F32)


def _dot_nt(a, b):
    return lax.dot_general(a, b, (((1,), (1,)), ((), ())), preferred_element_type=F32)


def _dot_hi(a, b):
    return jnp.dot(a, b, preferred_element_type=F32, precision=HI)


def _softplus(x):
    return jnp.maximum(x, 0.0) + jnp.log1p(jnp.exp(-jnp.abs(x)))


def _masked_softmax(s, mask):
    sm = jnp.where(mask, s, NEG)
    m = jnp.max(sm, axis=-1, keepdims=True)
    e = jnp.where(mask, jnp.exp(sm - m), 0.0)
    den = jnp.sum(e, axis=-1, keepdims=True)
    return e / jnp.where(den > 0, den, 1.0)


def _proj_kernel(x_ref, nw_ref, w_ref, hw_ref, o_ref, xn_ref):
    j = pl.program_id(1)

    @pl.when(j == 0)
    def _():
        x = x_ref[...]
        y = x * lax.rsqrt(jnp.mean(x * x, axis=-1, keepdims=True) + EPS)
        xn_ref[...] = (y * nw_ref[...]).astype(BF16)

    acc = _dot(xn_ref[...], w_ref[...])

    def normed(n_groups, w):
        parts = [_rms(acc[:, i * LANE:(i + 1) * LANE], w) for i in range(n_groups)]
        if n_groups * LANE < TN:
            parts.append(acc[:, n_groups * LANE:])
        return jnp.concatenate(parts, axis=1)

    is_q = j < (C_ZA // TN)
    is_ks = j == (C_KVS // TN)
    is_kw = j == (C_KVW // TN)

    @pl.when(is_q)
    def _():
        o_ref[...] = normed(TN // LANE, hw_ref[0:1, :])

    @pl.when(is_ks)
    def _():
        o_ref[...] = normed(NSA_KV_HEADS, hw_ref[1:2, :])

    @pl.when(is_kw)
    def _():
        o_ref[...] = normed(NSA_KV_HEADS, hw_ref[2:3, :])

    @pl.when(jnp.logical_not(is_q | is_ks | is_kw))
    def _():
        o_ref[...] = acc


def _proj(x2d, norm_w, w_perm, head_w, tm):
    n, d = x2d.shape
    return pl.pallas_call(
        _proj_kernel,
        grid=(n // tm, H_WIDTH // TN),
        in_specs=[pl.BlockSpec((tm, d), lambda i, j: (i, 0)),
                  pl.BlockSpec((1, d), lambda i, j: (0, 0)),
                  pl.BlockSpec((d, TN), lambda i, j: (0, j)),
                  pl.BlockSpec((3, LANE), lambda i, j: (0, 0))],
        out_specs=pl.BlockSpec((tm, TN), lambda i, j: (i, j)),
        out_shape=jax.ShapeDtypeStruct((n, H_WIDTH), F32),
        scratch_shapes=[pltpu.VMEM((tm, d), BF16)],
        compiler_params=_cparams(("parallel", "arbitrary")),
        name="in_proj",
    )(x2d, norm_w.reshape(1, d), w_perm, head_w)


def _cmp_partials_body(strided_rows, w1k_ref, w1v_ref, pk_ref, pv_ref, n_chunks):
    for g in range(NSA_KV_HEADS):
        for kv, (w1_ref, p_ref) in enumerate(((w1k_ref, pk_ref), (w1v_ref, pv_ref))):
            acc = jnp.zeros((n_chunks, CMP_R * CMP_HIDDEN), F32)
            for s in range(CMP_STRIDE):
                xs = strided_rows(kv * NSA_KV_HEADS + g, s)
                acc = acc + _dot(xs.astype(BF16), w1_ref[s])
            p_ref[0, g] = acc


def _cmp_part_prompt_kernel(r0_ref, r1_ref, r2_ref, r3_ref, w1k_ref, w1v_ref, pk_ref, pv_ref, *, n_chunks):
    refs = (r0_ref, r1_ref, r2_ref, r3_ref)

    def strided_rows(c, s):
        return refs[c][pl.ds(s, n_chunks, stride=CMP_STRIDE), :]

    _cmp_partials_body(strided_rows, w1k_ref, w1v_ref, pk_ref, pv_ref, n_chunks)


def _cmp_part_prompt(h, w1k, w1v, bsz, t):
    n_chunks = t // CMP_STRIDE
    pshape = jax.ShapeDtypeStruct((bsz, NSA_KV_HEADS, n_chunks, CMP_R * CMP_HIDDEN), F32)
    pspec = pl.BlockSpec((1, NSA_KV_HEADS, n_chunks, CMP_R * CMP_HIDDEN), lambda b: (b, 0, 0, 0))
    wspec = pl.BlockSpec((CMP_STRIDE, HEAD_DIM, CMP_R * CMP_HIDDEN), lambda b: (0, 0, 0))
    return pl.pallas_call(
        functools.partial(_cmp_part_prompt_kernel, n_chunks=n_chunks),
        grid=(bsz,),
        in_specs=[pl.BlockSpec((t, HEAD_DIM), functools.partial(lambda c, b: (b, C_KVC // HEAD_DIM + c), c))
                  for c in range(2 * NSA_KV_HEADS)] + [wspec, wspec],
        out_specs=[pspec, pspec],
        out_shape=[pshape, pshape],
        compiler_params=_cparams(("parallel",)),
        name="cmp_part_prompt",
    )(h, h, h, h, w1k, w1v)


def _cmp_part_sample_kernel(pt_ref, cache_ref, w1k_ref, w1v_ref, pk_ref, pv_ref, buf_ref, sem_ref,
                            *, pages_per_step, n_pages):
    b = pl.program_id(0)
    st = pl.program_id(1)

    n_chunks = pages_per_step * PAGE_SIZE // CMP_STRIDE
    n_col = 2 * NSA_KV_HEADS

    def page_copy(p, c):
        page = pt_ref[b * n_pages + st * pages_per_step + p]
        return pltpu.make_async_copy(cache_ref.at[page, :, pl.ds(c * HEAD_DIM, HEAD_DIM)],
                                     buf_ref.at[c, pl.ds(p * PAGE_SIZE, PAGE_SIZE)], sem_ref)

    for p in range(pages_per_step):
        for c in range(n_col):
            page_copy(p, c).start()
    for p in range(pages_per_step):
        for c in range(n_col):
            page_copy(p, c).wait()

    def strided_rows(c, s):
        return buf_ref[c, pl.ds(s, n_chunks, stride=CMP_STRIDE), :]

    _cmp_partials_body(strided_rows, w1k_ref, w1v_ref, pk_ref, pv_ref, n_chunks)


def _cmp_part_sample(cache2d, page_table, w1k, w1v):
    dbsz, n_pages = page_table.shape
    pages_per_step = min(16, n_pages)
    n_steps = n_pages // pages_per_step
    cps = pages_per_step * PAGE_SIZE // CMP_STRIDE
    n_chunks = n_pages * PAGE_SIZE // CMP_STRIDE
    pshape = jax.ShapeDtypeStruct((dbsz, NSA_KV_HEADS, n_chunks, CMP_R * CMP_HIDDEN), F32)
    pspec = pl.BlockSpec((1, NSA_KV_HEADS, cps, CMP_R * CMP_HIDDEN), lambda b, s, pt: (b, 0, s, 0))
    wspec = pl.BlockSpec((CMP_STRIDE, HEAD_DIM, CMP_R * CMP_HIDDEN), lambda b, s, pt: (0, 0, 0))
    grid_spec = pltpu.PrefetchScalarGridSpec(
        num_scalar_prefetch=1,
        grid=(dbsz, n_steps),
        in_specs=[pl.BlockSpec(memory_space=pl.ANY), wspec, wspec],
        out_specs=[pspec, pspec],
        scratch_shapes=[pltpu.VMEM((2 * NSA_KV_HEADS, pages_per_step * PAGE_SIZE, HEAD_DIM), F32),
                        pltpu.SemaphoreType.DMA(())],
    )
    return pl.pallas_call(
        functools.partial(_cmp_part_sample_kernel, pages_per_step=pages_per_step, n_pages=n_pages),
        grid_spec=grid_spec,
        out_shape=[pshape, pshape],
        compiler_params=_cparams(("arbitrary", "arbitrary")),
        name="cmp_part_sample",
    )(page_table.reshape(-1), cache2d, w1k, w1v)


def _cmp_finish_kernel(pk_ref, pv_ref, b1k_ref, w2k_ref, b1v_ref, w2v_ref, nk_ref, kc_ref, vc_ref,
                       *, n_chunks):
    for g in range(NSA_KV_HEADS):
        for kv, (p_ref, b1_ref, w2_ref, o_ref) in enumerate(((pk_ref, b1k_ref, w2k_ref, kc_ref),
                                                            (pv_ref, b1v_ref, w2v_ref, vc_ref))):
            part = p_ref[0, g]
            nxt = pltpu.roll(part[:, CMP_HIDDEN:], n_chunks - 1, 0)
            hid = _silu(part[:, :CMP_HIDDEN] + nxt + b1_ref[...])
            out = _dot(hid.astype(BF16), w2_ref[...])
            if kv == 0:
                out = _rms(out, nk_ref[...])
            o_ref[0, g] = out


def _cmp_finish(pk, pv, b1k, w2k, b1v, w2v, nk):
    bsz, _, n_chunks, _ = pk.shape
    pspec = pl.BlockSpec((1, NSA_KV_HEADS, n_chunks, CMP_R * CMP_HIDDEN), lambda b: (b, 0, 0, 0))
    vspec = pl.BlockSpec((1, CMP_HIDDEN), lambda b: (0, 0))
    wspec = pl.BlockSpec((CMP_HIDDEN, HEAD_DIM), lambda b: (0, 0))
    oshape = jax.ShapeDtypeStruct((bsz, NSA_KV_HEADS, n_chunks, HEAD_DIM), F32)
    ospec = pl.BlockSpec((1, NSA_KV_HEADS, n_chunks, HEAD_DIM), lambda b: (b, 0, 0, 0))
    return pl.pallas_call(
        functools.partial(_cmp_finish_kernel, n_chunks=n_chunks),
        grid=(bsz,),
        in_specs=[pspec, pspec, vspec, wspec, vspec, wspec, vspec],
        out_specs=[ospec, ospec],
        out_shape=[oshape, oshape],
        compiler_params=_cparams(("parallel",)),
        name="cmp_finish",
    )(pk, pv, b1k.reshape(1, -1), w2k, b1v.reshape(1, -1), w2v, nk.reshape(1, -1))


def _sel_map(n_cmp, n_sel, n_sel_pad):
    cs = np.arange(n_cmp) * CMP_STRIDE
    ss = np.arange(n_sel) * SEL_BLOCK
    ov = np.minimum(cs[:, None] + CMP_LEN, ss[None] + SEL_BLOCK) - np.maximum(cs[:, None], ss[None])
    m = np.zeros((n_cmp, n_sel_pad), np.float32)
    m[:, :n_sel] = np.clip(ov, 0, None).astype(np.float32) / CMP_LEN
    return jnp.asarray(m, dtype=BF16)


def _topk_rank_rows(score, n_valid):
    j = lax.broadcasted_iota(jnp.int32, score.shape, 1)
    cnt = jnp.zeros(score.shape, F32)
    for jp in range(n_valid):
        col = score[:, jp:jp + 1]
        earlier = (j > jp).astype(F32)
        cnt = cnt + jnp.where(col > score, 1.0, jnp.where(col == score, earlier, 0.0))
    return cnt


QT = 256
MASK_BIG = 16384.0


def _attn_prompt_kernel(q_ref, ks_ref, vs_ref, kw_ref, vw_ref, kc_ref, vc_ref, sm_ref, za_ref,
                        selmap_ref, kaug_ref, caug_ref, padaug_ref, band_ref, o_ref,
                        ksa_ref, vsb_ref, kwa_ref, vwb_ref, kca_ref, vcb_ref, *, t_len):
    g = pl.program_id(1)
    qb = pl.program_id(2)
    qstart = pl.multiple_of(qb * QT, QT)
    n_sel = t_len // SEL_BLOCK
    rows = NSA_GROUP * QT

    @pl.when(qb == 0)
    def _():
        ksa_ref[:, 0:HEAD_DIM] = ks_ref[...].astype(BF16)
        ksa_ref[:, HEAD_DIM:] = kaug_ref[...]
        vsb_ref[...] = vs_ref[...].astype(BF16)
        kwa_ref[0:WINDOW, 0:HEAD_DIM] = jnp.zeros((WINDOW, HEAD_DIM), BF16)
        kwa_ref[0:WINDOW, HEAD_DIM:] = padaug_ref[...]
        kwa_ref[WINDOW:, 0:HEAD_DIM] = kw_ref[...].astype(BF16)
        kwa_ref[WINDOW:, HEAD_DIM:] = kaug_ref[...]
        vwb_ref[0:WINDOW, :] = jnp.zeros((WINDOW, HEAD_DIM), BF16)
        vwb_ref[WINDOW:, :] = vw_ref[...].astype(BF16)
        kca_ref[:, 0:HEAD_DIM] = kc_ref[0, 0].astype(BF16)
        kca_ref[:, HEAD_DIM:] = caug_ref[...]
        vcb_ref[...] = vc_ref[0, 0].astype(BF16)

    q = q_ref[...]
    qsb = (jnp.concatenate([q[:, r * HEAD_DIM:(r + 1) * HEAD_DIM] for r in range(NSA_GROUP)], axis=0)
           * SCALE).astype(BF16)
    row = lax.broadcasted_iota(jnp.int32, (rows, 1), 0)
    r_of = row // QT
    tloc = row % QT
    qpos = qstart + tloc
    slope = jnp.zeros((rows, 1), F32)
    for r in range(NSA_GROUP):
        s_r = jnp.where(g == 0, 2.0 ** -(r + 1.0), 2.0 ** -(r + 1.0 + NSA_GROUP)).astype(F32)
        slope = jnp.where(r_of == r, s_r, slope)
    lane = lax.broadcasted_iota(jnp.int32, (rows, LANE), 1)
    pos_cols = jnp.where(lane == n_sel, slope * 256.0, jnp.where(lane == n_sel + 1, slope, 0.0))
    q_w = jnp.concatenate([qsb, jnp.where(lane == n_sel + 2, -MASK_BIG, pos_cols).astype(BF16)], axis=1)

    n_cmp = kca_ref.shape[0]
    kend = lax.broadcasted_iota(jnp.int32, (1, n_cmp), 1) * CMP_STRIDE + (CMP_LEN - 1)
    p_cmp = _masked_softmax(_dot_nt(q_w, kca_ref[...]), (qpos - kend) >= 0).astype(BF16)
    o_cmp = _dot(p_cmp, vcb_ref[...])
    imp4 = _dot(p_cmp, selmap_ref[...])
    imp = imp4[0:QT]
    for r in range(1, NSA_GROUP):
        imp = imp + imp4[r * QT:(r + 1) * QT]

    tq = qstart + lax.broadcasted_iota(jnp.int32, (QT, 1), 0)
    j = lax.broadcasted_iota(jnp.int32, (QT, LANE), 1)
    visible = j * SEL_BLOCK <= tq
    back = tq // SEL_BLOCK - j
    forced = (j == 0) | ((back >= 0) & (back < N_LOCAL_SEL))
    score = jnp.where(visible, jnp.where(forced, FORCE_SCORE, imp), -FORCE_SCORE)
    score = jnp.where(j < n_sel, score, -3e38)
    rank = _topk_rank_rows(score, n_sel)
    sel = jnp.where((rank < min(SEL_TOPK, n_sel)) & visible & (j < n_sel), 1.0, 0.0)
    sel4 = jnp.concatenate([sel] * NSA_GROUP, axis=0)
    q_s = jnp.concatenate([qsb, jnp.where(lane < n_sel, (sel4 - 1.0) * MASK_BIG, pos_cols).astype(BF16)],
                          axis=1)

    def online(sc, k0, carry):
        m, l, acc = carry
        m_new = jnp.maximum(m, jnp.max(sc, axis=-1, keepdims=True))
        alpha = jnp.exp(m - m_new)
        p = jnp.exp(sc - m_new)
        l = alpha * l + jnp.sum(p, axis=-1, keepdims=True)
        acc = alpha * acc + _dot(p.astype(BF16), vsb_ref[pl.ds(k0, QT), :])
        return m_new, l, acc

    def past_chunk(c, carry):
        k0 = pl.multiple_of(c * QT, QT)
        return online(_dot_nt(q_s, ksa_ref[pl.ds(k0, QT), :]), k0, carry)

    init = (jnp.full((rows, 1), NEG, F32), jnp.zeros((rows, 1), F32), jnp.zeros((rows, HEAD_DIM), F32))
    carry = lax.fori_loop(0, qb, past_chunk, init)
    ki = lax.broadcasted_iota(jnp.int32, (1, QT), 1)
    diag = jnp.where(tloc >= ki, _dot_nt(q_s, ksa_ref[pl.ds(qstart, QT), :]), -MASK_BIG)
    _, l_s, acc_s = online(diag, qstart, carry)
    o_slc = acc_s / l_s

    span = QT + WINDOW
    band = band_ref[...]
    sw = _dot_nt(q_w, kwa_ref[pl.ds(qstart, span), :]) + jnp.concatenate([band] * NSA_GROUP, axis=0)
    e_win = jnp.exp(sw - jnp.max(sw, axis=-1, keepdims=True))
    o_win = (_dot(e_win.astype(BF16), vwb_ref[pl.ds(qstart, span), :])
             / jnp.sum(e_win, axis=-1, keepdims=True))

    sm = sm_ref[...]
    outs = []
    for r in range(NSA_GROUP):
        tot = jnp.zeros((QT, HEAD_DIM), F32)
        for br, o_br in enumerate((o_cmp, o_slc, o_win)):
            c0 = br * NSA_HEADS + r
            gate = jnp.where(g == 0, sm[:, c0:c0 + 1], sm[:, c0 + NSA_GROUP:c0 + NSA_GROUP + 1])
            tot = tot + jax.nn.sigmoid(gate) * o_br[r * QT:(r + 1) * QT]
        outs.append(tot * _silu(za_ref[:, r * HEAD_DIM:(r + 1) * HEAD_DIM]))
    o_ref[...] = jnp.concatenate(outs, axis=1)


def _attn_prompt(h, kc, vc, bsz, t):
    nqb = t // QT
    n_cmp = t // CMP_STRIDE
    n_sel = t // SEL_BLOCK
    span = QT + WINDOW
    assert n_sel + 3 <= LANE and t % QT == 0 and t + CMP_LEN < 256 * 256
    selmap = _sel_map(n_cmp, n_sel, LANE)

    def aug_cols(pos, onehot_block):
        a = np.zeros((pos.shape[0], LANE), np.float32)
        if onehot_block:
            a[np.arange(pos.shape[0]), pos // SEL_BLOCK] = 1.0
        a[:, n_sel] = pos // 256
        a[:, n_sel + 1] = pos % 256
        return jnp.asarray(a, dtype=BF16)

    kaug = aug_cols(np.arange(t), True)
    caug = aug_cols(np.arange(n_cmp) * CMP_STRIDE + CMP_LEN - 1, False)
    pad = np.zeros((WINDOW, LANE), np.float32)
    pad[:, n_sel + 2] = 1.0
    padaug = jnp.asarray(pad, dtype=BF16)
    ti = np.arange(QT)[:, None]
    ii = np.arange(span)[None]
    band = jnp.asarray(np.where((ii >= ti) & (ii <= ti + WINDOW), 0.0, -MASK_BIG).astype(np.float32))
    gw = NSA_GROUP * HEAD_DIM
    row_blk = lambda b, g, i: b * nqb + i
    kv_spec = lambda c0: pl.BlockSpec((t, HEAD_DIM), lambda b, g, i: (b, c0 // HEAD_DIM + g))
    const = lambda shape: pl.BlockSpec(shape, lambda b, g, i: (0,) * len(shape))
    return pl.pallas_call(
        functools.partial(_attn_prompt_kernel, t_len=t),
        grid=(bsz, NSA_KV_HEADS, nqb),
        in_specs=[pl.BlockSpec((QT, gw), lambda b, g, i: (row_blk(b, g, i), C_Q // gw + g)),
                  kv_spec(C_KVS), kv_spec(C_KVS + NSA_KV_WIDTH),
                  kv_spec(C_KVW), kv_spec(C_KVW + NSA_KV_WIDTH),
                  pl.BlockSpec((1, 1, n_cmp, HEAD_DIM), lambda b, g, i: (b, g, 0, 0)),
                  pl.BlockSpec((1, 1, n_cmp, HEAD_DIM), lambda b, g, i: (b, g, 0, 0)),
                  pl.BlockSpec((QT, TN), lambda b, g, i: (row_blk(b, g, i), C_SM // TN)),
                  pl.BlockSpec((QT, gw), lambda b, g, i: (row_blk(b, g, i), C_ZA // gw + g)),
                  const((n_cmp, LANE)), const((t, LANE)), const((n_cmp, LANE)), const((WINDOW, LANE)),
                  const((QT, span))],
        out_specs=pl.BlockSpec((QT, gw), lambda b, g, i: (row_blk(b, g, i), g)),
        out_shape=jax.ShapeDtypeStruct((bsz * t, NSA_WIDTH), F32),
        scratch_shapes=[pltpu.VMEM((t, 2 * HEAD_DIM), BF16), pltpu.VMEM((t, HEAD_DIM), BF16),
                        pltpu.VMEM((WINDOW + t, 2 * HEAD_DIM), BF16), pltpu.VMEM((WINDOW + t, HEAD_DIM), BF16),
                        pltpu.VMEM((n_cmp, 2 * HEAD_DIM), BF16), pltpu.VMEM((n_cmp, HEAD_DIM), BF16)],
        compiler_params=_cparams(("parallel", "parallel", "arbitrary")),
        name="attn_prompt",
    )(h, h, h, h, h, kc, vc, h, h, selmap, kaug, caug, padaug, band)


GDN_PRE_TILE = 256


def _gdn_pre_kernel(x_ref, cw_ref, o_ref, *, t_len):
    j = pl.program_id(1)
    x = x_ref[...]
    t = lax.broadcasted_iota(jnp.int32, (t_len, 1), 0)
    acc = None
    for i in range(CONV_W):
        dly = CONV_W - 1 - i
        xs = x if dly == 0 else jnp.where(t >= dly, pltpu.roll(x, dly, 0), 0.0)
        term = xs * cw_ref[i:i + 1, :]
        acc = term if acc is None else acc + term
    c = _silu(acc)
    n_grp = GDN_PRE_TILE // GDN_DK
    q_tiles = GDN_QK_WIDTH // GDN_PRE_TILE

    @pl.when(j < q_tiles)
    def _():
        o_ref[...] = jnp.concatenate(
            [_l2(c[:, i * GDN_DK:(i + 1) * GDN_DK]) * (GDN_DK ** -0.5) for i in range(n_grp)], axis=1)

    @pl.when((j >= q_tiles) & (j < 2 * q_tiles))
    def _():
        o_ref[...] = jnp.concatenate([_l2(c[:, i * GDN_DK:(i + 1) * GDN_DK]) for i in range(n_grp)], axis=1)

    @pl.when(j >= 2 * q_tiles)
    def _():
        o_ref[...] = c


def _gdn_pre(h, conv_w, bsz, t):
    n_tiles = GDN_CONV_CH // GDN_PRE_TILE
    return pl.pallas_call(
        functools.partial(_gdn_pre_kernel, t_len=t),
        grid=(bsz, n_tiles),
        in_specs=[pl.BlockSpec((t, GDN_PRE_TILE), lambda b, j: (b, C_QKVB // GDN_PRE_TILE + j)),
                  pl.BlockSpec((CONV_W, GDN_PRE_TILE), lambda b, j: (0, j))],
        out_specs=pl.BlockSpec((t, GDN_PRE_TILE), lambda b, j: (b, j)),
        out_shape=jax.ShapeDtypeStruct((bsz * t, GDN_CONV_CH), F32),
        compiler_params=_cparams(("parallel", "parallel")),
        name="gdn_pre",
    )(h, conv_w)


GDN_PAIR = 2
GDN_PAIRS = GDN_HEADS // GDN_PAIR
GDN_PROWS = GDN_PAIR * GDN_CHUNK


def _split2(x):
    hi = x.astype(BF16)
    return hi, (x - hi.astype(F32)).astype(BF16)


def _split3(x):
    hi = x.astype(BF16)
    r = x - hi.astype(F32)
    mid = r.astype(BF16)
    return hi, mid, (r - mid.astype(F32)).astype(BF16)


def _dot_x3(a_parts, b_parts):
    a_hi, a_lo = a_parts
    b_hi, b_lo = b_parts
    return _dot(jnp.concatenate([a_hi, a_lo, a_hi], axis=1), jnp.concatenate([b_hi, b_hi, b_lo], axis=0))


def _gdn_intra_kernel(q_ref, k_ref, v_ref, acol_ref, bcol_ref, arow_ref, alogc_ref, dtbc_ref, alogr_ref,
                      dtbr_ref, u_ref, w_ref, att_ref, qd_ref, kd_ref, edl_ref):
    pr = GDN_PROWS
    ri = lax.broadcasted_iota(jnp.int32, (pr, pr), 0)
    ci = lax.broadcasted_iota(jnp.int32, (pr, pr), 1)
    same = (ri // GDN_CHUNK) == (ci // GDN_CHUNK)
    tril = same & (ri >= ci)
    stril = same & (ri > ci)
    eye = (ri == ci).astype(F32)
    lo_b = tril.astype(BF16)
    ut_b = (same & (ri <= ci)).astype(BF16)
    rcol = lax.broadcasted_iota(jnp.int32, (pr, 1), 0)

    gcol = -jnp.exp(alogc_ref[...]) * _softplus(acol_ref[0] + dtbc_ref[...])
    beta = jax.nn.sigmoid(bcol_ref[0])
    grow = -jnp.exp(alogr_ref[...]) * _softplus(arow_ref[0] + dtbr_ref[...])
    dec_col = sum(_dot(lo_b, part) for part in _split3(gcol))
    dec_row = sum(_dot(part, ut_b) for part in _split3(grow))

    def stack(ref, p):
        return jnp.concatenate([ref[:, (GDN_PAIR * p + i) * GDN_DK:(GDN_PAIR * p + i + 1) * GDN_DK]
                                for i in range(GDN_PAIR)], axis=0)

    ks, kbs, kbfs, lmats, vbs, qs, dcs = [], [], [], [], [], [], []
    for p in range(GDN_PAIRS):
        k2 = stack(k_ref, p)
        dc = dec_col[:, p:p + 1]
        b2 = beta[:, p:p + 1]
        dmat = jnp.where(tril, dc - dec_row[p:p + 1, :], 0.0)
        ks.append(k2)
        kbs.append(k2 * b2)
        kbfs.append(k2.astype(BF16))
        lmats.append(jnp.where(tril, jnp.exp(dmat), 0.0))
        vbs.append(stack(v_ref, p) * b2)
        qs.append(stack(q_ref, p))
        dcs.append(dc)

    npows = [-jnp.where(stril, _dot_nt(kbs[p].astype(BF16), kbfs[p]) * lmats[p], 0.0) for p in range(GDN_PAIRS)]
    tms = [eye + n for n in npows]
    for _ in range(5):
        parts = [_split2(n) for n in npows]
        npows = [_dot_x3(parts[p], parts[p]) for p in range(GDN_PAIRS)]
        parts = [_split2(n) for n in npows]
        tms = [tms[p] + _dot_x3(_split2(tms[p]), parts[p]) for p in range(GDN_PAIRS)]

    for p in range(GDN_PAIRS):
        tm_b = tms[p].astype(BF16)
        dc = dcs[p]
        edec = jnp.exp(dc)
        dl = jnp.where(rcol < GDN_CHUNK, dc[GDN_CHUNK - 1:GDN_CHUNK, :], dc[pr - 1:pr, :])
        u_ref[0, p] = _dot(tm_b, vbs[p].astype(BF16))
        w_ref[0, p] = _dot(tm_b, (kbs[p] * edec).astype(BF16)).astype(BF16)
        att_ref[0, p] = jnp.where(tril, _dot_nt(qs[p].astype(BF16), kbfs[p]) * lmats[p], 0.0).astype(BF16)
        qd_ref[0, p] = (qs[p] * edec).astype(BF16)
        kd_ref[0, p] = (ks[p] * jnp.exp(dl - dc)).astype(BF16)
        edl_ref[0, p] = jnp.broadcast_to(jnp.exp(dl), (pr, LANE))


def _gdn_intra(qkvn, a_col, b_col, a_row, a_log, dt_bias, bsz, t):
    nc = t // GDN_CHUNK
    n_chunks = bsz * nc
    w = GDN_QK_WIDTH
    pr = GDN_PROWS
    alog_c = jnp.tile(a_log.reshape(GDN_PAIRS, GDN_PAIR).T[:, None, :], (1, GDN_CHUNK, 1)).reshape(pr, GDN_PAIRS)
    dtb_c = jnp.tile(dt_bias.reshape(GDN_PAIRS, GDN_PAIR).T[:, None, :], (1, GDN_CHUNK, 1)).reshape(pr, GDN_PAIRS)
    alog_r = jnp.repeat(a_log.reshape(GDN_PAIRS, GDN_PAIR), GDN_CHUNK, axis=1)
    dtb_r = jnp.repeat(dt_bias.reshape(GDN_PAIRS, GDN_PAIR), GDN_CHUNK, axis=1)
    cspec = pl.BlockSpec((1, pr, GDN_PAIRS), lambda i: (i, 0, 0))
    cpar = pl.BlockSpec((pr, GDN_PAIRS), lambda i: (0, 0))
    rpar = pl.BlockSpec((GDN_PAIRS, pr), lambda i: (0, 0))
    mspec = pl.BlockSpec((1, GDN_PAIRS, pr, GDN_DK), lambda i: (i, 0, 0, 0))
    mshape = lambda dt: jax.ShapeDtypeStruct((n_chunks, GDN_PAIRS, pr, GDN_DK), dt)
    return pl.pallas_call(
        _gdn_intra_kernel,
        grid=(n_chunks,),
        in_specs=[pl.BlockSpec((GDN_CHUNK, w), lambda i: (i, 0)),
                  pl.BlockSpec((GDN_CHUNK, w), lambda i: (i, 1)),
                  pl.BlockSpec((GDN_CHUNK, w), lambda i: (i, 2)),
                  cspec, cspec,
                  pl.BlockSpec((1, GDN_PAIRS, pr), lambda i: (i, 0, 0)),
                  cpar, cpar, rpar, rpar],
        out_specs=[mspec] * 6,
        out_shape=[mshape(F32), mshape(BF16), mshape(BF16), mshape(BF16), mshape(BF16), mshape(F32)],
        compiler_params=_cparams(("parallel",)),
        name="gdn_intra",
    )(qkvn, qkvn, qkvn, a_col, b_col, a_row, alog_c, dtb_c, alog_r, dtb_r)


def _gdn_inter_kernel(u_ref, w_ref, att_ref, qd_ref, kd_ref, edl_ref, zb_ref, gnw_ref,
                      o_ref, s_out_ref, s_ref):
    c = pl.program_id(1)
    n_c = pl.num_programs(1)
    cs = GDN_CHUNK

    @pl.when(c == 0)
    def _():
        s_ref[...] = jnp.zeros_like(s_ref)

    heads = [(p, i, GDN_PAIR * p + i, slice(i * cs, (i + 1) * cs))
             for p in range(GDN_PAIRS) for i in range(GDN_PAIR)]
    s_old = [s_ref[hh] for (_, _, hh, _) in heads]
    s_bf = [s.astype(BF16) for s in s_old]
    ws = [_dot(w_ref[0, p, rows, :], s_bf[hh]) for (p, _, hh, rows) in heads]
    qss = [_dot(qd_ref[0, p, rows, :], s_bf[hh]) for (p, _, hh, rows) in heads]
    vn_b, o2 = [], []
    for p in range(GDN_PAIRS):
        vn = u_ref[0, p] - jnp.concatenate(ws[GDN_PAIR * p:GDN_PAIR * (p + 1)], axis=0)
        vn_b.append(vn.astype(BF16))
        o2.append(jnp.concatenate(qss[GDN_PAIR * p:GDN_PAIR * (p + 1)], axis=0) + _dot(att_ref[0, p], vn_b[p]))
    outs = []
    for (p, i, hh, rows) in heads:
        edl = edl_ref[0, p, i * cs:i * cs + 1, :]
        upd = lax.dot_general(kd_ref[0, p, rows, :], vn_b[p][rows], (((0,), (0,)), ((), ())),
                              preferred_element_type=F32)
        s_ref[hh] = s_old[hh] * edl + upd
        outs.append(_rms(o2[p][rows], gnw_ref[...]) * _silu(zb_ref[:, hh * GDN_DV:(hh + 1) * GDN_DV]))
    o_ref[...] = jnp.concatenate(outs, axis=1)

    @pl.when(c == n_c - 1)
    def _():
        s_out_ref[0] = s_ref[...]


def _gdn_inter(u, w, att, qd, kd, edl, h, gnw, bsz, t):
    nc = t // GDN_CHUNK
    wd = GDN_V_WIDTH
    pr = GDN_PROWS
    row = lambda b, c: b * nc + c
    mspec = pl.BlockSpec((1, GDN_PAIRS, pr, GDN_DK), lambda b, c: (row(b, c), 0, 0, 0))
    return pl.pallas_call(
        _gdn_inter_kernel,
        grid=(bsz, nc),
        in_specs=[mspec] * 6 + [pl.BlockSpec((GDN_CHUNK, wd), lambda b, c: (row(b, c), C_ZB // wd)),
                                pl.BlockSpec((1, GDN_DV), lambda b, c: (0, 0))],
        out_specs=[pl.BlockSpec((GDN_CHUNK, wd), lambda b, c: (row(b, c), 0)),
                   pl.BlockSpec((1, GDN_HEADS, GDN_DK, GDN_DV), lambda b, c: (b, 0, 0, 0))],
        out_shape=[jax.ShapeDtypeStruct((bsz * t, wd), F32),
                   jax.ShapeDtypeStruct((bsz, GDN_HEADS, GDN_DK, GDN_DV), F32)],
        scratch_shapes=[pltpu.VMEM((GDN_HEADS, GDN_DK, GDN_DV), F32)],
        compiler_params=_cparams(("parallel", "arbitrary")),
        name="gdn_inter",
    )(u, w, att, qd, kd, edl, h, gnw.reshape(1, -1))


def _out_kernel(oa_ref, ob_ref, ga_ref, gb_ref, x_ref, wpa_ref, wpb_ref, wo_ref, y_ref):
    ya = _dot(oa_ref[...].astype(BF16), wpa_ref[...])
    yb = _dot(ob_ref[...].astype(BF16), wpb_ref[...])
    m = jax.nn.sigmoid(ga_ref[...]) * ya + jax.nn.sigmoid(gb_ref[...]) * yb
    y_ref[...] = x_ref[...] + _dot(m.astype(BF16), wo_ref[...])


def _out_proj(oa, ob, h, x2d, w_pa, w_pb, w_o, tm):
    n, d = x2d.shape
    once = pl.Buffered(1)
    return pl.pallas_call(
        _out_kernel,
        grid=(n // tm,),
        in_specs=[pl.BlockSpec((tm, NSA_WIDTH), lambda i: (i, 0)),
                  pl.BlockSpec((tm, GDN_V_WIDTH), lambda i: (i, 0)),
                  pl.BlockSpec((tm, d), lambda i: (i, C_GA // d)),
                  pl.BlockSpec((tm, d), lambda i: (i, C_GB // d)),
                  pl.BlockSpec((tm, d), lambda i: (i, 0)),
                  pl.BlockSpec((NSA_WIDTH, d), lambda i: (0, 0), pipeline_mode=once),
                  pl.BlockSpec((GDN_V_WIDTH, d), lambda i: (0, 0), pipeline_mode=once),
                  pl.BlockSpec((d, d), lambda i: (0, 0), pipeline_mode=once)],
        out_specs=pl.BlockSpec((tm, d), lambda i: (i, 0)),
        out_shape=jax.ShapeDtypeStruct((n, d), F32),
        compiler_params=_cparams(("parallel",)),
        name="out_proj",
    )(oa, ob, h, h, x2d, w_pa, w_pb, w_o)


def _group_slopes(g):
    r_of = lax.broadcasted_iota(jnp.int32, (NSA_GROUP, 1), 0)
    slope = jnp.zeros((NSA_GROUP, 1), F32)
    for r in range(NSA_GROUP):
        s_r = jnp.where(g == 0, 2.0 ** -(r + 1.0), 2.0 ** -(r + 1.0 + NSA_GROUP)).astype(F32)
        slope = jnp.where(r_of == r, s_r, slope)
    return slope


def _attn_sample1_kernel(q_ref, kc_ref, vc_ref, selmap_ref, stw_ref, kvw_ref,
                         ocmp_ref, owin_ref, idx_ref, *, past_len, n_sel, n_sel_pad):
    qpos = past_len
    n_cmp = kc_ref.shape[2]
    wb = stw_ref.shape[1]
    rowi = lax.broadcasted_iota(jnp.int32, (LANE, n_sel_pad), 0)
    jrow = lax.broadcasted_iota(jnp.int32, (1, n_sel_pad), 1)
    scores = []
    for g in range(NSA_KV_HEADS):
        slope = _group_slopes(g)
        qg = q_ref[0, g].astype(BF16)
        kend = lax.broadcasted_iota(jnp.int32, (1, n_cmp), 1) * CMP_STRIDE + (CMP_LEN - 1)
        d = qpos - kend
        s = _dot_nt(qg, kc_ref[0, g].astype(BF16)) * SCALE - slope * d.astype(F32)
        p = _masked_softmax(s, d >= 0).astype(BF16)
        ocmp_ref[0, g] = _dot(p, vc_ref[0, g].astype(BF16))
        imp = jnp.sum(_dot(p, selmap_ref[...]), axis=0, keepdims=True)
        visible = jrow * SEL_BLOCK <= qpos
        back = qpos // SEL_BLOCK - jrow
        forced = (jrow == 0) | ((back >= 0) & (back < N_LOCAL_SEL))
        sc = jnp.where(visible, jnp.where(forced, FORCE_SCORE, imp), -FORCE_SCORE)
        scores.append(jnp.where(jrow < n_sel, sc, -3e38))
        col = g * HEAD_DIM
        kw = stw_ref[0, :, col:col + HEAD_DIM].astype(BF16)
        vw = stw_ref[0, :, NSA_KV_WIDTH + col:NSA_KV_WIDTH + col + HEAD_DIM].astype(BF16)
        knew = kvw_ref[0, g:g + 1, :].astype(BF16).astype(F32)
        vnew = kvw_ref[0, NSA_KV_HEADS + g:NSA_KV_HEADS + g + 1, :].astype(BF16).astype(F32)
        dwin = wb - lax.broadcasted_iota(jnp.int32, (1, wb), 1)
        sw = _dot_nt(qg, kw) * SCALE - slope * dwin.astype(F32)
        ok = (dwin >= 0) & (dwin <= WINDOW)
        sw = jnp.where(ok, sw, NEG)
        s_new = jnp.sum(qg.astype(F32) * knew, axis=-1, keepdims=True) * SCALE
        m = jnp.maximum(jnp.max(sw, axis=-1, keepdims=True), s_new)
        e = jnp.where(ok, jnp.exp(sw - m), 0.0)
        e_new = jnp.exp(s_new - m)
        den = jnp.sum(e, axis=-1, keepdims=True) + e_new
        owin_ref[0, g] = (_dot((e / den).astype(BF16), vw)
                          + (e_new / den).astype(BF16).astype(F32) * vnew)

    stacked = jnp.zeros((LANE, n_sel_pad), F32)
    for g in range(NSA_KV_HEADS):
        stacked = jnp.where(rowi == g, scores[g], stacked)
    cols = stacked.T
    ii = lax.broadcasted_iota(jnp.int32, (n_sel_pad, n_sel_pad), 0)
    jj = lax.broadcasted_iota(jnp.int32, (n_sel_pad, n_sel_pad), 1)
    earlier = (ii < jj).astype(F32)
    kcol = lax.broadcasted_iota(jnp.int32, (SEL_TOPK, 1), 0).astype(F32)
    for g in range(NSA_KV_HEADS):
        ci = cols[:, g:g + 1]
        rj = scores[g]
        beats = jnp.where(ci > rj, 1.0, jnp.where(ci == rj, earlier, 0.0))
        rank = jnp.sum(beats, axis=0, keepdims=True)
        onehot = rank == kcol
        idx = jnp.sum(jnp.where(onehot, jrow.astype(F32), 0.0), axis=-1, keepdims=True)
        idx_ref[0, g] = idx.astype(jnp.int32)


def _attn_sample1(q_s, kc_s, vc_s, st_win2d, kvw_new, past_len):
    dbsz = q_s.shape[0]
    n_cmp = kc_s.shape[2]
    n_sel = -(-(past_len + 1) // SEL_BLOCK)
    n_sel_pad = -(-n_sel // LANE) * LANE
    assert n_sel >= SEL_TOPK
    selmap = _sel_map(n_cmp, n_sel, n_sel_pad)
    wb = st_win2d.shape[1]
    hshape = jax.ShapeDtypeStruct((dbsz, NSA_KV_HEADS, NSA_GROUP, HEAD_DIM), F32)
    hspec = pl.BlockSpec((1, NSA_KV_HEADS, NSA_GROUP, HEAD_DIM), lambda b: (b, 0, 0, 0))
    cspec = pl.BlockSpec((1, NSA_KV_HEADS, n_cmp, HEAD_DIM), lambda b: (b, 0, 0, 0))
    return pl.pallas_call(
        functools.partial(_attn_sample1_kernel, past_len=past_len, n_sel=n_sel, n_sel_pad=n_sel_pad),
        grid=(dbsz,),
        in_specs=[hspec, cspec, cspec,
                  pl.BlockSpec((n_cmp, n_sel_pad), lambda b: (0, 0)),
                  pl.BlockSpec((1, wb, 2 * NSA_KV_WIDTH), lambda b: (b, 0, 0)),
                  pl.BlockSpec((1, 2 * NSA_KV_HEADS, HEAD_DIM), lambda b: (b, 0, 0))],
        out_specs=[hspec, hspec,
                   pl.BlockSpec((1, NSA_KV_HEADS, SEL_TOPK, 1), lambda b: (b, 0, 0, 0))],
        out_shape=[hshape, hshape,
                   jax.ShapeDtypeStruct((dbsz, NSA_KV_HEADS, SEL_TOPK, 1), jnp.int32)],
        compiler_params=_cparams(("parallel",)),
        name="attn_sample_cmp_win",
    )(q_s, kc_s, vc_s, selmap, st_win2d, kvw_new)


def _attn_sample2_kernel(idx_ref, pt_ref, q_ref, blk_ref, kvn_ref, ocmp_ref, owin_ref, gate_ref, za_ref,
                         o_ref, m_ref, l_ref, acc_ref, *, past_len):
    b = pl.program_id(0)
    g = pl.program_id(1)
    k = pl.program_id(2)
    n_k = pl.num_programs(2)
    qpos = past_len
    n_past_blocks = past_len // SEL_BLOCK
    slope = _group_slopes(g)
    q = q_ref[0, 0].astype(BF16)

    @pl.when(k == 0)
    def _():
        knew = kvn_ref[0, 0, 0].astype(BF16).astype(F32)
        vnew = kvn_ref[0, 1, 0].astype(BF16).astype(F32)
        m_ref[...] = jnp.sum(q.astype(F32) * knew, axis=-1, keepdims=True) * SCALE
        l_ref[...] = jnp.ones_like(l_ref)
        acc_ref[...] = jnp.broadcast_to(vnew, acc_ref.shape)

    blk = idx_ref[(b * NSA_KV_HEADS + g) * SEL_TOPK + k]
    in_past = blk < n_past_blocks
    tile = blk_ref[0]
    kk = jnp.where(g == 0, tile[:, 0:HEAD_DIM], tile[:, HEAD_DIM:2 * HEAD_DIM]).astype(BF16)
    vv = jnp.where(g == 0, tile[:, NSA_KV_WIDTH:NSA_KV_WIDTH + HEAD_DIM],
                   tile[:, NSA_KV_WIDTH + HEAD_DIM:]).astype(BF16)
    tok = blk * SEL_BLOCK + lax.broadcasted_iota(jnp.int32, (1, SEL_BLOCK), 1)
    dd = qpos - tok
    mask = (dd >= 0) & in_past
    sc = _dot_nt(q, kk) * SCALE - slope * dd.astype(F32)
    sc = jnp.where(mask, sc, NEG)
    m_old = m_ref[...]
    m_new = jnp.maximum(m_old, jnp.max(sc, axis=-1, keepdims=True))
    alpha = jnp.exp(m_old - m_new)
    p = jnp.where(mask, jnp.exp(sc - m_new), 0.0)
    m_ref[...] = m_new
    l_ref[...] = alpha * l_ref[...] + jnp.sum(p, axis=-1, keepdims=True)
    acc_ref[...] = alpha * acc_ref[...] + _dot(p.astype(BF16), vv)

    @pl.when(k == n_k - 1)
    def _():
        o_slc = acc_ref[...] / l_ref[...]
        tot = (jax.nn.sigmoid(gate_ref[0, 0, 0]) * ocmp_ref[0, 0]
               + jax.nn.sigmoid(gate_ref[0, 1, 0]) * o_slc
               + jax.nn.sigmoid(gate_ref[0, 2, 0]) * owin_ref[0, 0])
        o_ref[0, 0] = tot * _silu(za_ref[0, 0])


def _attn_sample2(idx, page_table, q_s, cache_blk, kvs_new, o_cmp, o_win, gates, za_s, past_len):
    dbsz, n_pages = page_table.shape
    n_past_blocks = past_len // SEL_BLOCK
    per_page = PAGE_SIZE // SEL_BLOCK

    def blk_map(b, g, k, idx_ref, pt_ref):
        blk = jnp.minimum(idx_ref[(b * NSA_KV_HEADS + g) * SEL_TOPK + k], n_past_blocks - 1)
        return (pt_ref[b * n_pages + blk // per_page] * per_page + blk % per_page, 0, 0)

    head = pl.BlockSpec((1, 1, NSA_GROUP, HEAD_DIM), lambda b, g, k, i, p: (b, g, 0, 0))
    grid_spec = pltpu.PrefetchScalarGridSpec(
        num_scalar_prefetch=2,
        grid=(dbsz, NSA_KV_HEADS, SEL_TOPK),
        in_specs=[head,
                  pl.BlockSpec((1, SEL_BLOCK, 2 * NSA_KV_WIDTH), blk_map),
                  pl.BlockSpec((1, 2, 1, 1, HEAD_DIM), lambda b, g, k, i, p: (b, 0, g, 0, 0)),
                  head, head,
                  pl.BlockSpec((1, 3, 1, NSA_GROUP, 1), lambda b, g, k, i, p: (b, 0, g, 0, 0)),
                  head],
        out_specs=head,
        scratch_shapes=[pltpu.VMEM((NSA_GROUP, 1), F32), pltpu.VMEM((NSA_GROUP, 1), F32),
                        pltpu.VMEM((NSA_GROUP, HEAD_DIM), F32)],
    )
    return pl.pallas_call(
        functools.partial(_attn_sample2_kernel, past_len=past_len),
        grid_spec=grid_spec,
        out_shape=jax.ShapeDtypeStruct((dbsz, NSA_KV_HEADS, NSA_GROUP, HEAD_DIM), F32),
        compiler_params=_cparams(("arbitrary", "arbitrary", "arbitrary")),
        name="attn_sample_slc",
    )(idx.reshape(-1), page_table.reshape(-1), q_s, cache_blk, kvs_new, o_cmp, o_win, gates, za_s)


def _gdn_sample_kernel(x_ref, cst_ref, cw_ref, a_ref, b_ref, alog_ref, dtb_ref, s_ref, gnw_ref, zb_ref,
                       o_ref, s_out_ref):
    acc = None
    for i in range(CONV_W):
        row = cst_ref[0, i] if i < CONV_W - 1 else x_ref[0]
        term = row * cw_ref[i]
        acc = term if acc is None else acc + term
    c = _silu(acc)
    q = _l2(c[0:GDN_HEADS]) * (GDN_DK ** -0.5)
    k = _l2(c[GDN_HEADS:2 * GDN_HEADS])
    v = c[2 * GDN_HEADS:3 * GDN_HEADS]
    gdec = -jnp.exp(alog_ref[...]) * _softplus(a_ref[0] + dtb_ref[...])
    beta = jax.nn.sigmoid(b_ref[0])
    eg = jnp.exp(gdec)
    qk = jnp.sum(q * k, axis=-1, keepdims=True)
    pad = jnp.concatenate([q, k, jnp.zeros((LANE - 2 * GDN_HEADS, GDN_DK), F32)], axis=0)
    cols = pad.T
    outs = []
    for hh in range(GDN_HEADS):
        s_h = s_ref[0, hh]
        qcol = cols[:, hh:hh + 1]
        kcol = cols[:, GDN_HEADS + hh:GDN_HEADS + hh + 1]
        b_h = beta[hh:hh + 1]
        e_h = eg[hh:hh + 1]
        k_s = jnp.sum(kcol * s_h, axis=0, keepdims=True)
        q_s = jnp.sum(qcol * s_h, axis=0, keepdims=True)
        vn = b_h * v[hh:hh + 1] - (b_h * e_h) * k_s
        outs.append(e_h * q_s + qk[hh:hh + 1] * vn)
        s_out_ref[0, hh] = s_h * e_h + kcol * vn
    o = jnp.concatenate(outs, axis=0)
    o_ref[0] = _rms(o, gnw_ref[...]) * _silu(zb_ref[0])


def _gdn_sample(x_s, cst, cw, a_s, b_s, a_log, dt_bias, s0, gnw, zb_s):
    dbsz = x_s.shape[0]
    nh3 = 3 * GDN_HEADS
    col = pl.BlockSpec((1, GDN_HEADS, 1), lambda b: (b, 0, 0))
    par = pl.BlockSpec((GDN_HEADS, 1), lambda b: (0, 0))
    head = pl.BlockSpec((1, GDN_HEADS, GDN_DV), lambda b: (b, 0, 0))
    sspec = pl.BlockSpec((1, GDN_HEADS, GDN_DK, GDN_DV), lambda b: (b, 0, 0, 0))
    return pl.pallas_call(
        _gdn_sample_kernel,
        grid=(dbsz,),
        in_specs=[pl.BlockSpec((1, nh3, GDN_DK), lambda b: (b, 0, 0)),
                  pl.BlockSpec((1, CONV_W - 1, nh3, GDN_DK), lambda b: (b, 0, 0, 0)),
                  pl.BlockSpec((CONV_W, nh3, GDN_DK), lambda b: (0, 0, 0)),
                  col, col, par, par, sspec,
                  pl.BlockSpec((1, GDN_DV), lambda b: (0, 0)),
                  head],
        out_specs=[head, sspec],
        out_shape=[jax.ShapeDtypeStruct((dbsz, GDN_HEADS, GDN_DV), F32),
                   jax.ShapeDtypeStruct(s0.shape, F32)],
        compiler_params=_cparams(("parallel",)),
        name="gdn_sample",
    )(x_s, cst, cw, a_s, b_s, a_log.reshape(-1, 1), dt_bias.reshape(-1, 1), s0, gnw.reshape(1, -1), zb_s)


def _permute_w_in(w_in):
    splits = (NSA_WIDTH, 2 * NSA_KV_WIDTH, 2 * NSA_KV_WIDTH, 2 * NSA_KV_WIDTH, 3 * NSA_HEADS, NSA_WIDTH,
              GDN_CONV_CH, GDN_HEADS, GDN_HEADS, GDN_V_WIDTH, w_in.shape[0], w_in.shape[0])
    bounds = np.cumsum(np.array(splits))[:-1].tolist()
    (q_a, kvc, kvs, kvw, g_nsa, z_a, qkv_b, a_b, b_b, z_b, gate_a, gate_b) = jnp.split(w_in, bounds, axis=1)
    small_pad = jnp.zeros((w_in.shape[0], H_WIDTH - C_SM - SM_B - GDN_HEADS), w_in.dtype)
    return jnp.concatenate([q_a, z_a, z_b, qkv_b, gate_a, gate_b, kvc, kvs, kvw, g_nsa, a_b, b_b, small_pad],
                           axis=1).astype(BF16)


def _cmp_w1(w1):
    return jnp.transpose(w1, (1, 2, 0, 3)).reshape(CMP_STRIDE, HEAD_DIM, CMP_R * CMP_HIDDEN).astype(BF16)


def _layer(x_p, x_s, cache_cmp, cache_slc, layer, st_win, st_s, st_conv, page_table, w):
    (norm_w, w_in, q_norm_w, k_norm_cmp_w, k_norm_slc_w, k_norm_win_w,
     cmp_w1_k, cmp_b1_k, cmp_w2_k, cmp_w1_v, cmp_b1_v, cmp_w2_v,
     conv_w, a_log, dt_bias, gdn_norm_w, w_pa, w_pb, w_o) = w
    bsz, t, d = x_p.shape
    dbsz, dseq, _ = x_s.shape
    assert dseq == 1 and d == C_GB - C_GA
    n_pages = page_table.shape[1]
    past_len = n_pages * PAGE_SIZE

    w_perm = _permute_w_in(w_in)
    head_w = jnp.stack([q_norm_w, k_norm_slc_w, k_norm_win_w])
    w1k, w1v = _cmp_w1(cmp_w1_k), _cmp_w1(cmp_w1_v)
    w2k, w2v = cmp_w2_k.astype(BF16), cmp_w2_v.astype(BF16)
    wpa, wpb, wo = w_pa.astype(BF16), w_pb.astype(BF16), w_o.astype(BF16)
    kvw_cols = 2 * NSA_KV_WIDTH

    n = bsz * t
    x2 = x_p.reshape(n, d)
    h = _proj(x2, norm_w, w_perm, head_w, tm=min(1024, n))
    pk, pv = _cmp_part_prompt(h, w1k, w1v, bsz, t)
    kc, vc = _cmp_finish(pk, pv, cmp_b1_k, w2k, cmp_b1_v, w2v, k_norm_cmp_w)
    oa = _attn_prompt(h, kc, vc, bsz, t)
    qkvn = _gdn_pre(h, conv_w, bsz, t)
    nc = t // GDN_CHUNK
    ab5 = h[:, C_SM + SM_A:C_SM + SM_B + GDN_HEADS].reshape(bsz * nc, GDN_CHUNK, 2, GDN_PAIRS, GDN_PAIR)
    col_form = lambda x: x.transpose(0, 3, 1, 2).reshape(bsz * nc, GDN_PROWS, GDN_PAIRS)
    a_row = ab5[:, :, 0].transpose(0, 2, 3, 1).reshape(bsz * nc, GDN_PAIRS, GDN_PROWS)
    u, wmat, att, qd, kd, edl = _gdn_intra(qkvn, col_form(ab5[:, :, 0]), col_form(ab5[:, :, 1]), a_row,
                                           a_log, dt_bias, bsz, t)
    ob, s_p = _gdn_inter(u, wmat, att, qd, kd, edl, h, gdn_norm_w, bsz, t)
    y_p = _out_proj(oa, ob, h, x2, wpa, wpb, wo, tm=min(256, n)).reshape(bsz, t, d)

    kv_shape = (bsz, t, 2, NSA_KV_HEADS, HEAD_DIM)
    cmp_p = h[:, C_KVC:C_KVC + kvw_cols].reshape(kv_shape)
    slc_p = h[:, C_KVS:C_KVS + kvw_cols].reshape(kv_shape)
    wlen = min(WINDOW, t)
    win_p = h.reshape(bsz, t, H_WIDTH)[:, t - wlen:, C_KVW:C_KVW + kvw_cols].reshape(
        bsz, wlen, 2, NSA_KV_HEADS, HEAD_DIM)
    conv_p = h.reshape(bsz, t, H_WIDTH)[:, t - (CONV_W - 1):, C_QKVB:C_QKVB + GDN_CONV_CH]

    xs2 = x_s.reshape(dbsz, d)
    hs = _proj(xs2, norm_w, w_perm, head_w, tm=dbsz)
    grp = (dbsz, NSA_KV_HEADS, NSA_GROUP, HEAD_DIM)
    q_s = hs[:, C_Q:C_Q + NSA_WIDTH].reshape(grp)
    za_s = hs[:, C_ZA:C_ZA + NSA_WIDTH].reshape(grp)
    kvc_s = hs[:, C_KVC:C_KVC + kvw_cols]
    kvs_s = hs[:, C_KVS:C_KVS + kvw_cols]
    kvw_s = hs[:, C_KVW:C_KVW + kvw_cols]
    n_phys = cache_cmp.shape[0] * cache_cmp.shape[1]
    pt_phys = page_table + layer * cache_cmp.shape[1]
    pk_s, pv_s = _cmp_part_sample(cache_cmp.reshape(n_phys, PAGE_SIZE, kvw_cols), pt_phys, w1k, w1v)
    kc_s, vc_s = _cmp_finish(pk_s, pv_s, cmp_b1_k, w2k, cmp_b1_v, w2v, k_norm_cmp_w)
    wb = st_win.shape[1]
    o_cmp_s, o_win_s, idx = _attn_sample1(q_s, kc_s, vc_s, st_win.reshape(dbsz, wb, kvw_cols),
                                          kvw_s.reshape(dbsz, 2 * NSA_KV_HEADS, HEAD_DIM), past_len)
    gates_s = hs[:, C_SM:C_SM + 3 * NSA_HEADS].reshape(dbsz, 3, NSA_KV_HEADS, NSA_GROUP, 1)
    per_page = PAGE_SIZE // SEL_BLOCK
    oa_s = _attn_sample2(idx, pt_phys, q_s, cache_slc.reshape(n_phys * per_page, SEL_BLOCK, kvw_cols),
                         kvs_s.reshape(dbsz, 2, NSA_KV_HEADS, 1, HEAD_DIM), o_cmp_s, o_win_s, gates_s, za_s,
                         past_len)
    nh3 = 3 * GDN_HEADS
    ob_s, s_s = _gdn_sample(hs[:, C_QKVB:C_QKVB + GDN_CONV_CH].reshape(dbsz, nh3, GDN_DK),
                            st_conv.reshape(dbsz, CONV_W - 1, nh3, GDN_DK),
                            conv_w.reshape(CONV_W, nh3, GDN_DK),
                            hs[:, C_SM + SM_A:C_SM + SM_A + GDN_HEADS].reshape(dbsz, GDN_HEADS, 1),
                            hs[:, C_SM + SM_B:C_SM + SM_B + GDN_HEADS].reshape(dbsz, GDN_HEADS, 1),
                            a_log, dt_bias, st_s, gdn_norm_w,
                            hs[:, C_ZB:C_ZB + GDN_V_WIDTH].reshape(dbsz, GDN_HEADS, GDN_DV))
    y_s = _out_proj(oa_s.reshape(dbsz, NSA_WIDTH), ob_s.reshape(dbsz, GDN_V_WIDTH), hs, xs2,
                    wpa, wpb, wo, tm=dbsz).reshape(dbsz, 1, d)

    kv1 = (dbsz, 1, 2, NSA_KV_HEADS, HEAD_DIM)
    win_s = jnp.concatenate([st_win, kvw_s.reshape(kv1)], axis=1)[:, -wb:]
    conv_s = jnp.concatenate([st_conv, hs[:, None, C_QKVB:C_QKVB + GDN_CONV_CH]], axis=1)[:, -(CONV_W - 1):]
    return y_p, y_s, (cmp_p, slc_p, win_p, s_p, conv_p,
                      kvc_s.reshape(kv1), kvs_s.reshape(kv1), win_s, s_s, conv_s)


def _take_layer(x, layer):
    return x.reshape(x.shape[1:]) if x.shape[0] == 1 else x[layer]


def kernel(x_prompt, x_sample, cache_cmp_kv, cache_slc_kv, state_win_kv, state_gdn_S, state_gdn_conv, page_table, norm_w, w_in, q_norm_w, k_norm_cmp_w, k_norm_slc_w, k_norm_win_w, cmp_w1_k, cmp_b1_k, cmp_w2_k, cmp_w1_v, cmp_b1_v, cmp_w2_v, conv_w, a_log, dt_bias, gdn_norm_w, w_pa, w_pb, w_o):
    weights = (norm_w, w_in, q_norm_w, k_norm_cmp_w, k_norm_slc_w, k_norm_win_w,
               cmp_w1_k, cmp_b1_k, cmp_w2_k, cmp_w1_v, cmp_b1_v, cmp_w2_v,
               conv_w, a_log, dt_bias, gdn_norm_w, w_pa, w_pb, w_o)
    depth = norm_w.shape[0]
    y_p, y_s = x_prompt, x_sample
    per_layer = []
    for layer in range(depth):
        y_p, y_s, st = _layer(y_p, y_s, cache_cmp_kv, cache_slc_kv, layer, _take_layer(state_win_kv, layer),
                              _take_layer(state_gdn_S, layer), _take_layer(state_gdn_conv, layer), page_table,
                              tuple(_take_layer(wt, layer) for wt in weights))
        per_layer.append(st)
    stacked = [jnp.stack(a) for a in zip(*per_layer)]
    return (y_p, y_s, *stacked)
```

```python
import functools

import numpy as np
import jax
import jax.numpy as jnp
from jax import lax
from jax.experimental import pallas as pl
from jax.experimental.pallas import tpu as pltpu

F32 = jnp.float32
BF16 = jnp.bfloat16
HI = lax.Precision.HIGHEST

PAGE_SIZE = 128
HEAD_DIM = 128
NSA_HEADS = 8
NSA_KV_HEADS = 2
NSA_GROUP = NSA_HEADS // NSA_KV_HEADS
NSA_WIDTH = NSA_HEADS * HEAD_DIM
NSA_KV_WIDTH = NSA_KV_HEADS * HEAD_DIM
CMP_LEN = 32
CMP_STRIDE = 16
CMP_R = CMP_LEN // CMP_STRIDE
CMP_HIDDEN = 128
SEL_BLOCK = 64
SEL_TOPK = 16
N_LOCAL_SEL = 2
WINDOW = 512
Q_BLOCK = 128
FORCE_SCORE = 1e4
ALIBI_MAX = 8.0
GDN_HEADS = 8
GDN_DK = 128
GDN_DV = 128
GDN_QK_WIDTH = GDN_HEADS * GDN_DK
GDN_V_WIDTH = GDN_HEADS * GDN_DV
GDN_CONV_CH = 2 * GDN_QK_WIDTH + GDN_V_WIDTH
CONV_W = 4
GDN_CHUNK = 64
EPS = 1e-6
SCALE = HEAD_DIM ** -0.5
NEG = -1e30

C_Q = 0
C_ZA = 1024
C_ZB = 2048
C_QKVB = 3072
C_GA = 6144
C_GB = 8192
C_KVC = 10240
C_KVS = 10752
C_KVW = 11264
C_SM = 11776
H_WIDTH = 12288
TN = 512
PROJ_TN = 1024
SM_A = 3 * NSA_HEADS
SM_B = SM_A + GDN_HEADS

LANE = 128
VMEM_LIMIT = 56 * 1024 * 1024


def _cparams(sem, vmem=VMEM_LIMIT):
    return pltpu.CompilerParams(dimension_semantics=sem, vmem_limit_bytes=vmem)


def _silu(x):
    return x * jax.nn.sigmoid(x)


def _rms(a, w):
    return a * lax.rsqrt(jnp.mean(a * a, axis=-1, keepdims=True) + EPS) * w


def _l2(a):
    return a * lax.rsqrt(jnp.sum(a * a, axis=-1, keepdims=True) + EPS)


def _dot(a, b):
    return jnp.dot(a, b, preferred_element_type=F32)


def _dot_nt(a, b):
    return lax.dot_general(a, b, (((1,), (1,)), ((), ())), preferred_element_type=F32)


def _dot_hi(a, b):
    return jnp.dot(a, b, preferred_element_type=F32, precision=HI)


def _softplus(x):
    return jnp.maximum(x, 0.0) + jnp.log1p(jnp.exp(-jnp.abs(x)))


def _masked_softmax(s, mask):
    sm = jnp.where(mask, s, NEG)
    m = jnp.max(sm, axis=-1, keepdims=True)
    e = jnp.where(mask, jnp.exp(sm - m), 0.0)
    den = jnp.sum(e, axis=-1, keepdims=True)
    return e / jnp.where(den > 0, den, 1.0)


def _proj_kernel(x_ref, nw_ref, w_ref, hw_ref, o_ref, xn_ref):
    j = pl.program_id(1)

    @pl.when(j == 0)
    def _():
        x = x_ref[...]
        y = x * lax.rsqrt(jnp.mean(x * x, axis=-1, keepdims=True) + EPS)
        xn_ref[...] = (y * nw_ref[...]).astype(BF16)

    acc = _dot(xn_ref[...], w_ref[...])

    plan = _proj_norm_plan()
    for tile, groups in plan.items():
        @pl.when(j == tile)
        def _(groups=groups):
            parts = []
            for grp in range(PROJ_TN // LANE):
                a = acc[:, grp * LANE:(grp + 1) * LANE]
                parts.append(_rms(a, hw_ref[groups[grp]:groups[grp] + 1, :]) if grp in groups else a)
            o_ref[...] = jnp.concatenate(parts, axis=1)

    plain = functools.reduce(jnp.logical_and, [j != tile for tile in plan])

    @pl.when(plain)
    def _():
        o_ref[...] = acc


def _proj_norm_plan():
    plan = {}
    for c0, width, row in ((C_Q, NSA_WIDTH, 0), (C_KVS, NSA_KV_WIDTH, 1), (C_KVW, NSA_KV_WIDTH, 2)):
        for grp in range(width // LANE):
            col = c0 + grp * LANE
            plan.setdefault(col // PROJ_TN, {})[(col % PROJ_TN) // LANE] = row
    return plan


def _proj(x2d, norm_w, w_perm, head_w, tm):
    n, d = x2d.shape
    return pl.pallas_call(
        _proj_kernel,
        grid=(n // tm, H_WIDTH // PROJ_TN),
        in_specs=[pl.BlockSpec((tm, d), lambda i, j: (i, 0)),
                  pl.BlockSpec((1, d), lambda i, j: (0, 0)),
                  pl.BlockSpec((d, PROJ_TN), lambda i, j: (0, j)),
                  pl.BlockSpec((3, LANE), lambda i, j: (0, 0))],
        out_specs=pl.BlockSpec((tm, PROJ_TN), lambda i, j: (i, j)),
        out_shape=jax.ShapeDtypeStruct((n, H_WIDTH), F32),
        scratch_shapes=[pltpu.VMEM((tm, d), BF16)],
        compiler_params=_cparams(("parallel", "arbitrary")),
        name="in_proj",
    )(x2d, norm_w.reshape(1, d), w_perm, head_w)


def _cmp_partials_body(strided_rows, w1k_ref, w1v_ref, pk_ref, pv_ref, n_chunks):
    for g in range(NSA_KV_HEADS):
        for kv, (w1_ref, p_ref) in enumerate(((w1k_ref, pk_ref), (w1v_ref, pv_ref))):
            acc = jnp.zeros((n_chunks, CMP_R * CMP_HIDDEN), F32)
            for s in range(CMP_STRIDE):
                xs = strided_rows(kv * NSA_KV_HEADS + g, s)
                acc = acc + _dot(xs.astype(BF16), w1_ref[s])
            p_ref[0, g] = acc


def _cmp_part_prompt_kernel(r0_ref, r1_ref, r2_ref, r3_ref, w1k_ref, w1v_ref, pk_ref, pv_ref, *, n_chunks):
    refs = (r0_ref, r1_ref, r2_ref, r3_ref)

    def strided_rows(c, s):
        return refs[c][pl.ds(s, n_chunks, stride=CMP_STRIDE), :]

    _cmp_partials_body(strided_rows, w1k_ref, w1v_ref, pk_ref, pv_ref, n_chunks)


def _cmp_part_prompt(h, w1k, w1v, bsz, t):
    n_chunks = t // CMP_STRIDE
    pshape = jax.ShapeDtypeStruct((bsz, NSA_KV_HEADS, n_chunks, CMP_R * CMP_HIDDEN), F32)
    pspec = pl.BlockSpec((1, NSA_KV_HEADS, n_chunks, CMP_R * CMP_HIDDEN), lambda b: (b, 0, 0, 0))
    wspec = pl.BlockSpec((CMP_STRIDE, HEAD_DIM, CMP_R * CMP_HIDDEN), lambda b: (0, 0, 0))
    return pl.pallas_call(
        functools.partial(_cmp_part_prompt_kernel, n_chunks=n_chunks),
        grid=(bsz,),
        in_specs=[pl.BlockSpec((t, HEAD_DIM), functools.partial(lambda c, b: (b, C_KVC // HEAD_DIM + c), c))
                  for c in range(2 * NSA_KV_HEADS)] + [wspec, wspec],
        out_specs=[pspec, pspec],
        out_shape=[pshape, pshape],
        compiler_params=_cparams(("parallel",)),
        name="cmp_part_prompt",
    )(h, h, h, h, w1k, w1v)


KV_ROWS = 2 * NSA_KV_HEADS
PAGE_ROWS = PAGE_SIZE * KV_ROWS


def _cmp_part_sample_kernel(pt_ref, cache_ref, w1k_ref, w1v_ref, pk_ref, pv_ref, buf_ref, sem_ref,
                            *, pages_per_step):
    i = pl.program_id(0)
    n_i = pl.num_programs(0)
    slot = i % 2
    n_chunks = pages_per_step * PAGE_SIZE // CMP_STRIDE
    chunk_rows = CMP_STRIDE * KV_ROWS

    def page_copy(step, p, to_slot):
        page = pt_ref[step * pages_per_step + p]
        return pltpu.make_async_copy(cache_ref.at[pl.ds(page * PAGE_ROWS, PAGE_ROWS)],
                                     buf_ref.at[to_slot, pl.ds(p * PAGE_ROWS, PAGE_ROWS)], sem_ref.at[to_slot])

    @pl.when(i == 0)
    def _():
        for p in range(pages_per_step):
            page_copy(0, p, 0).start()

    @pl.when(i + 1 < n_i)
    def _():
        for p in range(pages_per_step):
            page_copy(i + 1, p, 1 - slot).start()

    for p in range(pages_per_step):
        page_copy(i, p, slot).wait()

    for kv, (w1_ref, p_ref) in enumerate(((w1k_ref, pk_ref), (w1v_ref, pv_ref))):
        acc = jnp.zeros((NSA_KV_HEADS * n_chunks, CMP_R * CMP_HIDDEN), F32)
        for s in range(CMP_STRIDE):
            xs = jnp.concatenate(
                [buf_ref[slot, pl.ds(s * KV_ROWS + kv * NSA_KV_HEADS + g, n_chunks, stride=chunk_rows), :]
                 for g in range(NSA_KV_HEADS)], axis=0)
            acc = acc + _dot(xs.astype(BF16), w1_ref[s])
        for g in range(NSA_KV_HEADS):
            p_ref[0, g] = acc[g * n_chunks:(g + 1) * n_chunks]


def _cmp_part_sample(cache_rows, page_table, w1k, w1v):
    dbsz, n_pages = page_table.shape
    pages_per_step = min(32, n_pages)
    n_steps = n_pages // pages_per_step
    cps = pages_per_step * PAGE_SIZE // CMP_STRIDE
    n_chunks = n_pages * PAGE_SIZE // CMP_STRIDE
    pshape = jax.ShapeDtypeStruct((dbsz, NSA_KV_HEADS, n_chunks, CMP_R * CMP_HIDDEN), F32)
    pspec = pl.BlockSpec((1, NSA_KV_HEADS, cps, CMP_R * CMP_HIDDEN),
                         lambda i, pt: (i // n_steps, 0, i % n_steps, 0))
    wspec = pl.BlockSpec((CMP_STRIDE, HEAD_DIM, CMP_R * CMP_HIDDEN), lambda i, pt: (0, 0, 0))
    grid_spec = pltpu.PrefetchScalarGridSpec(
        num_scalar_prefetch=1,
        grid=(dbsz * n_steps,),
        in_specs=[pl.BlockSpec(memory_space=pl.ANY), wspec, wspec],
        out_specs=[pspec, pspec],
        scratch_shapes=[pltpu.VMEM((2, pages_per_step * PAGE_ROWS, HEAD_DIM), F32),
                        pltpu.SemaphoreType.DMA((2,))],
    )
    return pl.pallas_call(
        functools.partial(_cmp_part_sample_kernel, pages_per_step=pages_per_step),
        grid_spec=grid_spec,
        out_shape=[pshape, pshape],
        compiler_params=_cparams(("arbitrary",)),
        name="cmp_part_sample",
    )(page_table.reshape(-1), cache_rows, w1k, w1v)


def _cmp_finish_kernel(pk_ref, pv_ref, b1k_ref, w2k_ref, b1v_ref, w2v_ref, nk_ref, kc_ref, vc_ref,
                       *, n_chunks):
    for g in range(NSA_KV_HEADS):
        for kv, (p_ref, b1_ref, w2_ref, o_ref) in enumerate(((pk_ref, b1k_ref, w2k_ref, kc_ref),
                                                            (pv_ref, b1v_ref, w2v_ref, vc_ref))):
            part = p_ref[0, g]
            nxt = pltpu.roll(part[:, CMP_HIDDEN:], n_chunks - 1, 0)
            hid = _silu(part[:, :CMP_HIDDEN] + nxt + b1_ref[...])
            out = _dot(hid.astype(BF16), w2_ref[...])
            if kv == 0:
                out = _rms(out, nk_ref[...])
            o_ref[0, g] = out


def _cmp_finish(pk, pv, b1k, w2k, b1v, w2v, nk):
    bsz, _, n_chunks, _ = pk.shape
    pspec = pl.BlockSpec((1, NSA_KV_HEADS, n_chunks, CMP_R * CMP_HIDDEN), lambda b: (b, 0, 0, 0))
    vspec = pl.BlockSpec((1, CMP_HIDDEN), lambda b: (0, 0))
    wspec = pl.BlockSpec((CMP_HIDDEN, HEAD_DIM), lambda b: (0, 0))
    oshape = jax.ShapeDtypeStruct((bsz, NSA_KV_HEADS, n_chunks, HEAD_DIM), F32)
    ospec = pl.BlockSpec((1, NSA_KV_HEADS, n_chunks, HEAD_DIM), lambda b: (b, 0, 0, 0))
    return pl.pallas_call(
        functools.partial(_cmp_finish_kernel, n_chunks=n_chunks),
        grid=(bsz,),
        in_specs=[pspec, pspec, vspec, wspec, vspec, wspec, vspec],
        out_specs=[ospec, ospec],
        out_shape=[oshape, oshape],
        compiler_params=_cparams(("parallel",)),
        name="cmp_finish",
    )(pk, pv, b1k.reshape(1, -1), w2k, b1v.reshape(1, -1), w2v, nk.reshape(1, -1))


def _sel_map(n_cmp, n_sel, n_sel_pad):
    cs = np.arange(n_cmp) * CMP_STRIDE
    ss = np.arange(n_sel) * SEL_BLOCK
    ov = np.minimum(cs[:, None] + CMP_LEN, ss[None] + SEL_BLOCK) - np.maximum(cs[:, None], ss[None])
    m = np.zeros((n_cmp, n_sel_pad), np.float32)
    m[:, :n_sel] = np.clip(ov, 0, None).astype(np.float32) / CMP_LEN
    return jnp.asarray(m, dtype=BF16)


def _topk_rank_rows(score, n_valid):
    j = lax.broadcasted_iota(jnp.int32, score.shape, 1)
    cnt = jnp.zeros(score.shape, F32)
    for jp in range(n_valid):
        col = score[:, jp:jp + 1]
        earlier = (j > jp).astype(F32)
        cnt = cnt + jnp.where(col > score, 1.0, jnp.where(col == score, earlier, 0.0))
    return cnt


QT = 256
MASK_BIG = 16384.0


def _attn_prompt_kernel(q_ref, ks_ref, vs_ref, kw_ref, vw_ref, kc_ref, vc_ref, sm_ref, za_ref,
                        selmap_ref, kaug_ref, caug_ref, padaug_ref, band_ref, o_ref,
                        ksa_ref, vsb_ref, kwa_ref, vwb_ref, kca_ref, vcb_ref, *, t_len):
    g = pl.program_id(1)
    qb = pl.program_id(2)
    qstart = pl.multiple_of(qb * QT, QT)
    n_sel = t_len // SEL_BLOCK
    rows = NSA_GROUP * QT

    @pl.when(qb == 0)
    def _():
        ksa_ref[:, 0:HEAD_DIM] = ks_ref[...].astype(BF16)
        ksa_ref[:, HEAD_DIM:] = kaug_ref[...]
        vsb_ref[...] = vs_ref[...].astype(BF16)
        kwa_ref[0:WINDOW, 0:HEAD_DIM] = jnp.zeros((WINDOW, HEAD_DIM), BF16)
        kwa_ref[0:WINDOW, HEAD_DIM:] = padaug_ref[...]
        kwa_ref[WINDOW:, 0:HEAD_DIM] = kw_ref[...].astype(BF16)
        kwa_ref[WINDOW:, HEAD_DIM:] = kaug_ref[...]
        vwb_ref[0:WINDOW, :] = jnp.zeros((WINDOW, HEAD_DIM), BF16)
        vwb_ref[WINDOW:, :] = vw_ref[...].astype(BF16)
        kca_ref[:, 0:HEAD_DIM] = kc_ref[0, 0].astype(BF16)
        kca_ref[:, HEAD_DIM:] = caug_ref[...]
        vcb_ref[...] = vc_ref[0, 0].astype(BF16)

    q = q_ref[...]
    qsb = (jnp.concatenate([q[:, r * HEAD_DIM:(r + 1) * HEAD_DIM] for r in range(NSA_GROUP)], axis=0)
           * SCALE).astype(BF16)
    row = lax.broadcasted_iota(jnp.int32, (rows, 1), 0)
    r_of = row // QT
    tloc = row % QT
    qpos = qstart + tloc
    slope = jnp.zeros((rows, 1), F32)
    for r in range(NSA_GROUP):
        s_r = jnp.where(g == 0, 2.0 ** -(r + 1.0), 2.0 ** -(r + 1.0 + NSA_GROUP)).astype(F32)
        slope = jnp.where(r_of == r, s_r, slope)
    lane = lax.broadcasted_iota(jnp.int32, (rows, LANE), 1)
    pos_cols = jnp.where(lane == n_sel, slope * 256.0, jnp.where(lane == n_sel + 1, slope, 0.0))
    q_w = jnp.concatenate([qsb, jnp.where(lane == n_sel + 2, -MASK_BIG, pos_cols).astype(BF16)], axis=1)

    n_cmp = kca_ref.shape[0]
    kend = lax.broadcasted_iota(jnp.int32, (1, n_cmp), 1) * CMP_STRIDE + (CMP_LEN - 1)
    p_cmp = _masked_softmax(_dot_nt(q_w, kca_ref[...]), (qpos - kend) >= 0).astype(BF16)
    o_cmp = _dot(p_cmp, vcb_ref[...])
    imp4 = _dot(p_cmp, selmap_ref[...])
    imp = imp4[0:QT]
    for r in range(1, NSA_GROUP):
        imp = imp + imp4[r * QT:(r + 1) * QT]

    tq = qstart + lax.broadcasted_iota(jnp.int32, (QT, 1), 0)
    j = lax.broadcasted_iota(jnp.int32, (QT, LANE), 1)
    visible = j * SEL_BLOCK <= tq
    back = tq // SEL_BLOCK - j
    forced = (j == 0) | ((back >= 0) & (back < N_LOCAL_SEL))
    score = jnp.where(visible, jnp.where(forced, FORCE_SCORE, imp), -FORCE_SCORE)
    score = jnp.where(j < n_sel, score, -3e38)
    rank = _topk_rank_rows(score, n_sel)
    sel = jnp.where((rank < min(SEL_TOPK, n_sel)) & visible & (j < n_sel), 1.0, 0.0)
    sel4 = jnp.concatenate([sel] * NSA_GROUP, axis=0)
    q_s = jnp.concatenate([qsb, jnp.where(lane < n_sel, (sel4 - 1.0) * MASK_BIG, pos_cols).astype(BF16)],
                          axis=1)

    def online(sc, k0, carry):
        m, l, acc = carry
        m_new = jnp.maximum(m, jnp.max(sc, axis=-1, keepdims=True))
        alpha = jnp.exp(m - m_new)
        p = jnp.exp(sc - m_new)
        l = alpha * l + jnp.sum(p, axis=-1, keepdims=True)
        acc = alpha * acc + _dot(p.astype(BF16), vsb_ref[pl.ds(k0, QT), :])
        return m_new, l, acc

    def past_chunk(c, carry):
        k0 = pl.multiple_of(c * QT, QT)
        return online(_dot_nt(q_s, ksa_ref[pl.ds(k0, QT), :]), k0, carry)

    init = (jnp.full((rows, 1), NEG, F32), jnp.zeros((rows, 1), F32), jnp.zeros((rows, HEAD_DIM), F32))
    carry = lax.fori_loop(0, qb, past_chunk, init)
    ki = lax.broadcasted_iota(jnp.int32, (1, QT), 1)
    diag = jnp.where(tloc >= ki, _dot_nt(q_s, ksa_ref[pl.ds(qstart, QT), :]), -MASK_BIG)
    _, l_s, acc_s = online(diag, qstart, carry)
    o_slc = acc_s / l_s

    span = QT + WINDOW
    band = band_ref[...]
    sw = _dot_nt(q_w, kwa_ref[pl.ds(qstart, span), :]) + jnp.concatenate([band] * NSA_GROUP, axis=0)
    e_win = jnp.exp(sw - jnp.max(sw, axis=-1, keepdims=True))
    o_win = (_dot(e_win.astype(BF16), vwb_ref[pl.ds(qstart, span), :])
             / jnp.sum(e_win, axis=-1, keepdims=True))

    sm = sm_ref[...]
    outs = []
    for r in range(NSA_GROUP):
        tot = jnp.zeros((QT, HEAD_DIM), F32)
        for br, o_br in enumerate((o_cmp, o_slc, o_win)):
            c0 = br * NSA_HEADS + r
            gate = jnp.where(g == 0, sm[:, c0:c0 + 1], sm[:, c0 + NSA_GROUP:c0 + NSA_GROUP + 1])
            tot = tot + jax.nn.sigmoid(gate) * o_br[r * QT:(r + 1) * QT]
        outs.append(tot * _silu(za_ref[:, r * HEAD_DIM:(r + 1) * HEAD_DIM]))
    o_ref[...] = jnp.concatenate(outs, axis=1)


def _attn_prompt(h, kc, vc, bsz, t):
    nqb = t // QT
    n_cmp = t // CMP_STRIDE
    n_sel = t // SEL_BLOCK
    span = QT + WINDOW
    assert n_sel + 3 <= LANE and t % QT == 0 and t + CMP_LEN < 256 * 256
    selmap = _sel_map(n_cmp, n_sel, LANE)

    def aug_cols(pos, onehot_block):
        a = np.zeros((pos.shape[0], LANE), np.float32)
        if onehot_block:
            a[np.arange(pos.shape[0]), pos // SEL_BLOCK] = 1.0
        a[:, n_sel] = pos // 256
        a[:, n_sel + 1] = pos % 256
        return jnp.asarray(a, dtype=BF16)

    kaug = aug_cols(np.arange(t), True)
    caug = aug_cols(np.arange(n_cmp) * CMP_STRIDE + CMP_LEN - 1, False)
    pad = np.zeros((WINDOW, LANE), np.float32)
    pad[:, n_sel + 2] = 1.0
    padaug = jnp.asarray(pad, dtype=BF16)
    ti = np.arange(QT)[:, None]
    ii = np.arange(span)[None]
    band = jnp.asarray(np.where((ii >= ti) & (ii <= ti + WINDOW), 0.0, -MASK_BIG).astype(np.float32))
    gw = NSA_GROUP * HEAD_DIM
    row_blk = lambda b, g, i: b * nqb + i
    kv_spec = lambda c0: pl.BlockSpec((t, HEAD_DIM), lambda b, g, i: (b, c0 // HEAD_DIM + g))
    const = lambda shape: pl.BlockSpec(shape, lambda b, g, i: (0,) * len(shape))
    return pl.pallas_call(
        functools.partial(_attn_prompt_kernel, t_len=t),
        grid=(bsz, NSA_KV_HEADS, nqb),
        in_specs=[pl.BlockSpec((QT, gw), lambda b, g, i: (row_blk(b, g, i), C_Q // gw + g)),
                  kv_spec(C_KVS), kv_spec(C_KVS + NSA_KV_WIDTH),
                  kv_spec(C_KVW), kv_spec(C_KVW + NSA_KV_WIDTH),
                  pl.BlockSpec((1, 1, n_cmp, HEAD_DIM), lambda b, g, i: (b, g, 0, 0)),
                  pl.BlockSpec((1, 1, n_cmp, HEAD_DIM), lambda b, g, i: (b, g, 0, 0)),
                  pl.BlockSpec((QT, TN), lambda b, g, i: (row_blk(b, g, i), C_SM // TN)),
                  pl.BlockSpec((QT, gw), lambda b, g, i: (row_blk(b, g, i), C_ZA // gw + g)),
                  const((n_cmp, LANE)), const((t, LANE)), const((n_cmp, LANE)), const((WINDOW, LANE)),
                  const((QT, span))],
        out_specs=pl.BlockSpec((QT, gw), lambda b, g, i: (row_blk(b, g, i), g)),
        out_shape=jax.ShapeDtypeStruct((bsz * t, NSA_WIDTH), F32),
        scratch_shapes=[pltpu.VMEM((t, 2 * HEAD_DIM), BF16), pltpu.VMEM((t, HEAD_DIM), BF16),
                        pltpu.VMEM((WINDOW + t, 2 * HEAD_DIM), BF16), pltpu.VMEM((WINDOW + t, HEAD_DIM), BF16),
                        pltpu.VMEM((n_cmp, 2 * HEAD_DIM), BF16), pltpu.VMEM((n_cmp, HEAD_DIM), BF16)],
        compiler_params=_cparams(("parallel", "parallel", "arbitrary")),
        name="attn_prompt",
    )(h, h, h, h, h, kc, vc, h, h, selmap, kaug, caug, padaug, band)


SUBLANES = 8
GDN_PAIR = 2
GDN_PAIRS = GDN_HEADS // GDN_PAIR
GDN_PROWS = GDN_PAIR * GDN_CHUNK


def _split2(x):
    hi = x.astype(BF16)
    return hi, (x - hi.astype(F32)).astype(BF16)


def _split3(x):
    hi = x.astype(BF16)
    r = x - hi.astype(F32)
    mid = r.astype(BF16)
    return hi, mid, (r - mid.astype(F32)).astype(BF16)


def _dot_x3(a_parts, b_parts):
    a_hi, a_lo = a_parts
    b_hi, b_lo = b_parts
    return _dot(jnp.concatenate([a_hi, a_lo, a_hi], axis=1), jnp.concatenate([b_hi, b_hi, b_lo], axis=0))


def _gdn_intra_kernel(x_ref, prev_ref, cw_ref, acol_ref, bcol_ref, arow_ref, alogc_ref, dtbc_ref, alogr_ref,
                      dtbr_ref, u_ref, w_ref, att_ref, qd_ref, kd_ref, edl_ref, *, chunks_per_seq):
    pr = GDN_PROWS
    first = pl.program_id(0) % chunks_per_seq == 0
    full = jnp.concatenate([jnp.where(first, 0.0, prev_ref[...]), x_ref[...]], axis=0)
    conv = None
    for i in range(CONV_W):
        dly = CONV_W - 1 - i
        xs = full if dly == 0 else pltpu.roll(full, dly, 0)
        term = xs[SUBLANES:] * cw_ref[i:i + 1, :]
        conv = term if conv is None else conv + term
    conv = _silu(conv)

    def head_cols(kind, hh):
        c0 = kind * GDN_QK_WIDTH + hh * GDN_DK
        return conv[:, c0:c0 + GDN_DK]

    def stack(kind, p):
        parts = [head_cols(kind, GDN_PAIR * p + i) for i in range(GDN_PAIR)]
        if kind == 0:
            parts = [_l2(x) * (GDN_DK ** -0.5) for x in parts]
        elif kind == 1:
            parts = [_l2(x) for x in parts]
        return jnp.concatenate(parts, axis=0)

    ri = lax.broadcasted_iota(jnp.int32, (pr, pr), 0)
    ci = lax.broadcasted_iota(jnp.int32, (pr, pr), 1)
    same = (ri // GDN_CHUNK) == (ci // GDN_CHUNK)
    tril = same & (ri >= ci)
    stril = same & (ri > ci)
    eye = (ri == ci).astype(F32)
    lo_b = tril.astype(BF16)
    ut_b = (same & (ri <= ci)).astype(BF16)
    rcol = lax.broadcasted_iota(jnp.int32, (pr, 1), 0)

    gcol = -jnp.exp(alogc_ref[...]) * _softplus(acol_ref[0] + dtbc_ref[...])
    beta = jax.nn.sigmoid(bcol_ref[0])
    grow = -jnp.exp(alogr_ref[...]) * _softplus(arow_ref[0] + dtbr_ref[...])
    dec_col = sum(_dot(lo_b, part) for part in _split3(gcol))
    dec_row = sum(_dot(part, ut_b) for part in _split3(grow))

    ks, kbs, kbfs, lmats, vbs, qs, dcs = [], [], [], [], [], [], []
    for p in range(GDN_PAIRS):
        k2 = stack(1, p)
        dc = dec_col[:, p:p + 1]
        b2 = beta[:, p:p + 1]
        dmat = jnp.where(tril, dc - dec_row[p:p + 1, :], 0.0)
        ks.append(k2)
        kbs.append(k2 * b2)
        kbfs.append(k2.astype(BF16))
        lmats.append(jnp.where(tril, jnp.exp(dmat), 0.0))
        vbs.append(stack(2, p) * b2)
        qs.append(stack(0, p))
        dcs.append(dc)

    npows = [-jnp.where(stril, _dot_nt(kbs[p].astype(BF16), kbfs[p]) * lmats[p], 0.0) for p in range(GDN_PAIRS)]
    tms = [eye + n for n in npows]
    for _ in range(5):
        parts = [_split2(n) for n in npows]
        npows = [_dot_x3(parts[p], parts[p]) for p in range(GDN_PAIRS)]
        parts = [_split2(n) for n in npows]
        tms = [tms[p] + _dot_x3(_split2(tms[p]), parts[p]) for p in range(GDN_PAIRS)]

    edl_rows = []
    for p in range(GDN_PAIRS):
        tm_b = tms[p].astype(BF16)
        dc = dcs[p]
        edec = jnp.exp(dc)
        edl_rows += [jnp.broadcast_to(jnp.exp(dc[(i + 1) * GDN_CHUNK - 1:(i + 1) * GDN_CHUNK, :]), (1, LANE))
                     for i in range(GDN_PAIR)]
        dl = jnp.where(rcol < GDN_CHUNK, dc[GDN_CHUNK - 1:GDN_CHUNK, :], dc[pr - 1:pr, :])
        u_ref[0, p] = _dot(tm_b, vbs[p].astype(BF16))
        w_ref[0, p] = _dot(tm_b, (kbs[p] * edec).astype(BF16)).astype(BF16)
        att_ref[0, p] = jnp.where(tril, _dot_nt(qs[p].astype(BF16), kbfs[p]) * lmats[p], 0.0).astype(BF16)
        qd_ref[0, p] = (qs[p] * edec).astype(BF16)
        kd_ref[0, p] = (ks[p] * jnp.exp(dl - dc)).astype(BF16)
    edl_ref[0] = jnp.concatenate(edl_rows, axis=0)


def _gdn_intra(h, conv_w, a_col, b_col, a_row, a_log, dt_bias, bsz, t):
    nc = t // GDN_CHUNK
    n_chunks = bsz * nc
    pr = GDN_PROWS
    per_chunk = GDN_CHUNK // SUBLANES
    alog_c = jnp.tile(a_log.reshape(GDN_PAIRS, GDN_PAIR).T[:, None, :], (1, GDN_CHUNK, 1)).reshape(pr, GDN_PAIRS)
    dtb_c = jnp.tile(dt_bias.reshape(GDN_PAIRS, GDN_PAIR).T[:, None, :], (1, GDN_CHUNK, 1)).reshape(pr, GDN_PAIRS)
    alog_r = jnp.repeat(a_log.reshape(GDN_PAIRS, GDN_PAIR), GDN_CHUNK, axis=1)
    dtb_r = jnp.repeat(dt_bias.reshape(GDN_PAIRS, GDN_PAIR), GDN_CHUNK, axis=1)
    cspec = pl.BlockSpec((1, pr, GDN_PAIRS), lambda i: (i, 0, 0))
    cpar = pl.BlockSpec((pr, GDN_PAIRS), lambda i: (0, 0))
    rpar = pl.BlockSpec((GDN_PAIRS, pr), lambda i: (0, 0))
    mspec = pl.BlockSpec((1, GDN_PAIRS, pr, GDN_DK), lambda i: (i, 0, 0, 0))
    mshape = lambda dt: jax.ShapeDtypeStruct((n_chunks, GDN_PAIRS, pr, GDN_DK), dt)
    qkv_blk = C_QKVB // GDN_CONV_CH
    return pl.pallas_call(
        functools.partial(_gdn_intra_kernel, chunks_per_seq=nc),
        grid=(n_chunks,),
        in_specs=[pl.BlockSpec((GDN_CHUNK, GDN_CONV_CH), lambda i: (i, qkv_blk)),
                  pl.BlockSpec((SUBLANES, GDN_CONV_CH), lambda i: (jnp.maximum(i * per_chunk - 1, 0), qkv_blk)),
                  pl.BlockSpec((CONV_W, GDN_CONV_CH), lambda i: (0, 0)),
                  cspec, cspec,
                  pl.BlockSpec((1, GDN_PAIRS, pr), lambda i: (i, 0, 0)),
                  cpar, cpar, rpar, rpar],
        out_specs=[mspec] * 5 + [pl.BlockSpec((1, GDN_HEADS, LANE), lambda i: (i, 0, 0))],
        out_shape=[mshape(F32), mshape(BF16), mshape(BF16), mshape(BF16), mshape(BF16),
                   jax.ShapeDtypeStruct((n_chunks, GDN_HEADS, LANE), F32)],
        compiler_params=_cparams(("parallel",)),
        name="gdn_intra",
    )(h, h, conv_w, a_col, b_col, a_row, alog_c, dtb_c, alog_r, dtb_r)


def _gdn_inter_kernel(u_ref, w_ref, att_ref, qd_ref, kd_ref, edl_ref, zb_ref, gnw_ref,
                      o_ref, s_out_ref, s_ref):
    c = pl.program_id(1)
    n_c = pl.num_programs(1)
    cs = GDN_CHUNK

    @pl.when(c == 0)
    def _():
        s_ref[...] = jnp.zeros_like(s_ref)

    heads = [(p, i, GDN_PAIR * p + i, slice(i * cs, (i + 1) * cs))
             for p in range(GDN_PAIRS) for i in range(GDN_PAIR)]
    s_old = [s_ref[hh] for (_, _, hh, _) in heads]
    s_bf = [s.astype(BF16) for s in s_old]
    ws = [_dot(w_ref[0, p, rows, :], s_bf[hh]) for (p, _, hh, rows) in heads]
    qss = [_dot(qd_ref[0, p, rows, :], s_bf[hh]) for (p, _, hh, rows) in heads]
    vn_b, o2 = [], []
    for p in range(GDN_PAIRS):
        vn = u_ref[0, p] - jnp.concatenate(ws[GDN_PAIR * p:GDN_PAIR * (p + 1)], axis=0)
        vn_b.append(vn.astype(BF16))
        o2.append(jnp.concatenate(qss[GDN_PAIR * p:GDN_PAIR * (p + 1)], axis=0) + _dot(att_ref[0, p], vn_b[p]))
    outs = []
    for (p, i, hh, rows) in heads:
        edl = edl_ref[0, hh:hh + 1, :]
        upd = lax.dot_general(kd_ref[0, p, rows, :], vn_b[p][rows], (((0,), (0,)), ((), ())),
                              preferred_element_type=F32)
        s_ref[hh] = s_old[hh] * edl + upd
        outs.append(_rms(o2[p][rows], gnw_ref[...]) * _silu(zb_ref[:, hh * GDN_DV:(hh + 1) * GDN_DV]))
    o_ref[...] = jnp.concatenate(outs, axis=1)

    @pl.when(c == n_c - 1)
    def _():
        s_out_ref[0] = s_ref[...]


def _gdn_inter(u, w, att, qd, kd, edl, h, gnw, bsz, t):
    nc = t // GDN_CHUNK
    wd = GDN_V_WIDTH
    pr = GDN_PROWS
    row = lambda b, c: b * nc + c
    mspec = pl.BlockSpec((1, GDN_PAIRS, pr, GDN_DK), lambda b, c: (row(b, c), 0, 0, 0))
    return pl.pallas_call(
        _gdn_inter_kernel,
        grid=(bsz, nc),
        in_specs=[mspec] * 5 + [pl.BlockSpec((1, GDN_HEADS, LANE), lambda b, c: (row(b, c), 0, 0)),
                                pl.BlockSpec((GDN_CHUNK, wd), lambda b, c: (row(b, c), C_ZB // wd)),
                                pl.BlockSpec((1, GDN_DV), lambda b, c: (0, 0))],
        out_specs=[pl.BlockSpec((GDN_CHUNK, wd), lambda b, c: (row(b, c), 0)),
                   pl.BlockSpec((1, GDN_HEADS, GDN_DK, GDN_DV), lambda b, c: (b, 0, 0, 0))],
        out_shape=[jax.ShapeDtypeStruct((bsz * t, wd), F32),
                   jax.ShapeDtypeStruct((bsz, GDN_HEADS, GDN_DK, GDN_DV), F32)],
        scratch_shapes=[pltpu.VMEM((GDN_HEADS, GDN_DK, GDN_DV), F32)],
        compiler_params=_cparams(("parallel", "arbitrary")),
        name="gdn_inter",
    )(u, w, att, qd, kd, edl, h, gnw.reshape(1, -1))


def _out_kernel(oa_ref, ob_ref, ga_ref, gb_ref, x_ref, wpa_ref, wpb_ref, wo_ref, y_ref):
    ya = _dot(oa_ref[...].astype(BF16), wpa_ref[...])
    yb = _dot(ob_ref[...].astype(BF16), wpb_ref[...])
    m = jax.nn.sigmoid(ga_ref[...]) * ya + jax.nn.sigmoid(gb_ref[...]) * yb
    y_ref[...] = x_ref[...] + _dot(m.astype(BF16), wo_ref[...])


def _out_proj(oa, ob, h, x2d, w_pa, w_pb, w_o, tm):
    n, d = x2d.shape
    once = pl.Buffered(1)
    return pl.pallas_call(
        _out_kernel,
        grid=(n // tm,),
        in_specs=[pl.BlockSpec((tm, NSA_WIDTH), lambda i: (i, 0)),
                  pl.BlockSpec((tm, GDN_V_WIDTH), lambda i: (i, 0)),
                  pl.BlockSpec((tm, d), lambda i: (i, C_GA // d)),
                  pl.BlockSpec((tm, d), lambda i: (i, C_GB // d)),
                  pl.BlockSpec((tm, d), lambda i: (i, 0)),
                  pl.BlockSpec((NSA_WIDTH, d), lambda i: (0, 0), pipeline_mode=once),
                  pl.BlockSpec((GDN_V_WIDTH, d), lambda i: (0, 0), pipeline_mode=once),
                  pl.BlockSpec((d, d), lambda i: (0, 0), pipeline_mode=once)],
        out_specs=pl.BlockSpec((tm, d), lambda i: (i, 0)),
        out_shape=jax.ShapeDtypeStruct((n, d), F32),
        compiler_params=_cparams(("parallel",)),
        name="out_proj",
    )(oa, ob, h, h, x2d, w_pa, w_pb, w_o)


def _group_slopes(g):
    r_of = lax.broadcasted_iota(jnp.int32, (NSA_GROUP, 1), 0)
    slope = jnp.zeros((NSA_GROUP, 1), F32)
    for r in range(NSA_GROUP):
        s_r = jnp.where(g == 0, 2.0 ** -(r + 1.0), 2.0 ** -(r + 1.0 + NSA_GROUP)).astype(F32)
        slope = jnp.where(r_of == r, s_r, slope)
    return slope


def _attn_sample1_kernel(q_ref, kc_ref, vc_ref, selmap_ref, stw_ref, kvw_ref,
                         ocmp_ref, owin_ref, idx_ref, *, past_len, n_sel, n_sel_pad):
    qpos = past_len
    n_cmp = kc_ref.shape[2]
    wb = stw_ref.shape[1] // KV_ROWS
    rowi = lax.broadcasted_iota(jnp.int32, (LANE, n_sel_pad), 0)
    jrow = lax.broadcasted_iota(jnp.int32, (1, n_sel_pad), 1)
    scores = []
    for g in range(NSA_KV_HEADS):
        slope = _group_slopes(g)
        qg = q_ref[0, g].astype(BF16)
        kend = lax.broadcasted_iota(jnp.int32, (1, n_cmp), 1) * CMP_STRIDE + (CMP_LEN - 1)
        d = qpos - kend
        s = _dot_nt(qg, kc_ref[0, g].astype(BF16)) * SCALE - slope * d.astype(F32)
        p = _masked_softmax(s, d >= 0).astype(BF16)
        ocmp_ref[0, g] = _dot(p, vc_ref[0, g].astype(BF16))
        imp = jnp.sum(_dot(p, selmap_ref[...]), axis=0, keepdims=True)
        visible = jrow * SEL_BLOCK <= qpos
        back = qpos // SEL_BLOCK - jrow
        forced = (jrow == 0) | ((back >= 0) & (back < N_LOCAL_SEL))
        sc = jnp.where(visible, jnp.where(forced, FORCE_SCORE, imp), -FORCE_SCORE)
        scores.append(jnp.where(jrow < n_sel, sc, -3e38))
        kw = stw_ref[0, pl.ds(g, wb, stride=KV_ROWS), :].astype(BF16)
        vw = stw_ref[0, pl.ds(NSA_KV_HEADS + g, wb, stride=KV_ROWS), :].astype(BF16)
        knew = kvw_ref[0, g:g + 1, :].astype(BF16).astype(F32)
        vnew = kvw_ref[0, NSA_KV_HEADS + g:NSA_KV_HEADS + g + 1, :].astype(BF16).astype(F32)
        dwin = wb - lax.broadcasted_iota(jnp.int32, (1, wb), 1)
        sw = _dot_nt(qg, kw) * SCALE - slope * dwin.astype(F32)
        ok = (dwin >= 0) & (dwin <= WINDOW)
        sw = jnp.where(ok, sw, NEG)
        s_new = jnp.sum(qg.astype(F32) * knew, axis=-1, keepdims=True) * SCALE
        m = jnp.maximum(jnp.max(sw, axis=-1, keepdims=True), s_new)
        e = jnp.where(ok, jnp.exp(sw - m), 0.0)
        e_new = jnp.exp(s_new - m)
        den = jnp.sum(e, axis=-1, keepdims=True) + e_new
        owin_ref[0, g] = (_dot((e / den).astype(BF16), vw)
                          + (e_new / den).astype(BF16).astype(F32) * vnew)

    stacked = jnp.zeros((LANE, n_sel_pad), F32)
    for g in range(NSA_KV_HEADS):
        stacked = jnp.where(rowi == g, scores[g], stacked)
    cols = stacked.T
    ii = lax.broadcasted_iota(jnp.int32, (n_sel_pad, n_sel_pad), 0)
    jj = lax.broadcasted_iota(jnp.int32, (n_sel_pad, n_sel_pad), 1)
    earlier = (ii < jj).astype(F32)
    kcol = lax.broadcasted_iota(jnp.int32, (SEL_TOPK, 1), 0).astype(F32)
    for g in range(NSA_KV_HEADS):
        ci = cols[:, g:g + 1]
        rj = scores[g]
        beats = jnp.where(ci > rj, 1.0, jnp.where(ci == rj, earlier, 0.0))
        rank = jnp.sum(beats, axis=0, keepdims=True)
        onehot = rank == kcol
        idx = jnp.sum(jnp.where(onehot, jrow.astype(F32), 0.0), axis=-1, keepdims=True)
        idx_ref[0, g] = idx.astype(jnp.int32)


def _attn_sample1(q_s, kc_s, vc_s, st_win_rows, kvw_new, past_len):
    dbsz = q_s.shape[0]
    n_cmp = kc_s.shape[2]
    n_sel = -(-(past_len + 1) // SEL_BLOCK)
    n_sel_pad = -(-n_sel // LANE) * LANE
    assert n_sel >= SEL_TOPK
    selmap = _sel_map(n_cmp, n_sel, n_sel_pad)
    wrows = st_win_rows.shape[1]
    hshape = jax.ShapeDtypeStruct((dbsz, NSA_KV_HEADS, NSA_GROUP, HEAD_DIM), F32)
    hspec = pl.BlockSpec((1, NSA_KV_HEADS, NSA_GROUP, HEAD_DIM), lambda b: (b, 0, 0, 0))
    cspec = pl.BlockSpec((1, NSA_KV_HEADS, n_cmp, HEAD_DIM), lambda b: (b, 0, 0, 0))
    return pl.pallas_call(
        functools.partial(_attn_sample1_kernel, past_len=past_len, n_sel=n_sel, n_sel_pad=n_sel_pad),
        grid=(dbsz,),
        in_specs=[hspec, cspec, cspec,
                  pl.BlockSpec((n_cmp, n_sel_pad), lambda b: (0, 0)),
                  pl.BlockSpec((1, wrows, HEAD_DIM), lambda b: (b, 0, 0)),
                  pl.BlockSpec((1, 2 * NSA_KV_HEADS, HEAD_DIM), lambda b: (b, 0, 0))],
        out_specs=[hspec, hspec,
                   pl.BlockSpec((1, NSA_KV_HEADS, SEL_TOPK, 1), lambda b: (b, 0, 0, 0))],
        out_shape=[hshape, hshape,
                   jax.ShapeDtypeStruct((dbsz, NSA_KV_HEADS, SEL_TOPK, 1), jnp.int32)],
        compiler_params=_cparams(("parallel",)),
        name="attn_sample_cmp_win",
    )(q_s, kc_s, vc_s, selmap, st_win_rows, kvw_new)


BLOCK_ROWS = SEL_BLOCK * KV_ROWS


def _attn_sample2_kernel(idx_ref, pt_ref, q_ref, cache_ref, kvn_ref, ocmp_ref, owin_ref, gate_ref, za_ref,
                         o_ref, buf_ref, sem_ref, *, past_len, n_pages):
    i = pl.program_id(0)
    n_i = pl.num_programs(0)
    slot = i % 2
    g = i % NSA_KV_HEADS
    qpos = past_len
    n_past_blocks = past_len // SEL_BLOCK
    per_page = PAGE_SIZE // SEL_BLOCK
    n_keys = SEL_TOPK * SEL_BLOCK

    def block_copy(step, k, to_slot):
        blk = jnp.minimum(idx_ref[step * SEL_TOPK + k], n_past_blocks - 1)
        phys = pt_ref[(step // NSA_KV_HEADS) * n_pages + blk // per_page] * per_page + blk % per_page
        return pltpu.make_async_copy(cache_ref.at[pl.ds(phys * BLOCK_ROWS, BLOCK_ROWS)],
                                     buf_ref.at[to_slot, pl.ds(k * BLOCK_ROWS, BLOCK_ROWS)], sem_ref.at[to_slot])

    @pl.when(i == 0)
    def _():
        for k in range(SEL_TOPK):
            block_copy(0, k, 0).start()

    @pl.when(i + 1 < n_i)
    def _():
        for k in range(SEL_TOPK):
            block_copy(i + 1, k, 1 - slot).start()

    for k in range(SEL_TOPK):
        block_copy(i, k, slot).wait()

    slope = _group_slopes(g)
    q = q_ref[0, 0].astype(BF16)
    kk = buf_ref[slot, pl.ds(g, n_keys, stride=KV_ROWS), :].astype(BF16)
    vv = buf_ref[slot, pl.ds(NSA_KV_HEADS + g, n_keys, stride=KV_ROWS), :].astype(BF16)
    lane_blk = lax.broadcasted_iota(jnp.int32, (1, n_keys), 1) // SEL_BLOCK
    tok = lax.broadcasted_iota(jnp.int32, (1, n_keys), 1) % SEL_BLOCK
    for k in range(SEL_TOPK):
        blk = idx_ref[i * SEL_TOPK + k]
        tok = tok + jnp.where(lane_blk == k, jnp.where(blk < n_past_blocks, blk * SEL_BLOCK, qpos + 1), 0)
    dd = qpos - tok
    mask = dd >= 0
    sc = jnp.where(mask, _dot_nt(q, kk) * SCALE - slope * dd.astype(F32), NEG)
    knew = kvn_ref[0, 0, 0].astype(BF16).astype(F32)
    vnew = kvn_ref[0, 1, 0].astype(BF16).astype(F32)
    s_new = jnp.sum(q.astype(F32) * knew, axis=-1, keepdims=True) * SCALE
    m = jnp.maximum(jnp.max(sc, axis=-1, keepdims=True), s_new)
    p = jnp.where(mask, jnp.exp(sc - m), 0.0)
    p_new = jnp.exp(s_new - m)
    den = jnp.sum(p, axis=-1, keepdims=True) + p_new
    o_slc = (_dot(p.astype(BF16), vv) + p_new.astype(BF16).astype(F32) * vnew) / den
    tot = (jax.nn.sigmoid(gate_ref[0, 0, 0]) * ocmp_ref[0, 0]
           + jax.nn.sigmoid(gate_ref[0, 1, 0]) * o_slc
           + jax.nn.sigmoid(gate_ref[0, 2, 0]) * owin_ref[0, 0])
    o_ref[0, 0] = tot * _silu(za_ref[0, 0])


def _attn_sample2(idx, page_table, q_s, cache_rows, kvs_new, o_cmp, o_win, gates, za_s, past_len):
    dbsz, n_pages = page_table.shape
    ng = NSA_KV_HEADS
    head = pl.BlockSpec((1, 1, NSA_GROUP, HEAD_DIM), lambda i, x, p: (i // ng, i % ng, 0, 0))
    grid_spec = pltpu.PrefetchScalarGridSpec(
        num_scalar_prefetch=2,
        grid=(dbsz * ng,),
        in_specs=[head,
                  pl.BlockSpec(memory_space=pl.ANY),
                  pl.BlockSpec((1, 2, 1, 1, HEAD_DIM), lambda i, x, p: (i // ng, 0, i % ng, 0, 0)),
                  head, head,
                  pl.BlockSpec((1, 3, 1, NSA_GROUP, 1), lambda i, x, p: (i // ng, 0, i % ng, 0, 0)),
                  head],
        out_specs=head,
        scratch_shapes=[pltpu.VMEM((2, SEL_TOPK * BLOCK_ROWS, HEAD_DIM), F32),
                        pltpu.SemaphoreType.DMA((2,))],
    )
    return pl.pallas_call(
        functools.partial(_attn_sample2_kernel, past_len=past_len, n_pages=n_pages),
        grid_spec=grid_spec,
        out_shape=jax.ShapeDtypeStruct((dbsz, ng, NSA_GROUP, HEAD_DIM), F32),
        compiler_params=_cparams(("arbitrary",)),
        name="attn_sample_slc",
    )(idx.reshape(-1), page_table.reshape(-1), q_s, cache_rows, kvs_new, o_cmp, o_win, gates, za_s)


def _gdn_sample_kernel(x_ref, cst_ref, cw_ref, a_ref, b_ref, alog_ref, dtb_ref, s_ref, gnw_ref, zb_ref,
                       o_ref, s_out_ref):
    acc = None
    for i in range(CONV_W):
        row = cst_ref[0, i] if i < CONV_W - 1 else x_ref[0]
        term = row * cw_ref[i]
        acc = term if acc is None else acc + term
    c = _silu(acc)
    q = _l2(c[0:GDN_HEADS]) * (GDN_DK ** -0.5)
    k = _l2(c[GDN_HEADS:2 * GDN_HEADS])
    v = c[2 * GDN_HEADS:3 * GDN_HEADS]
    gdec = -jnp.exp(alog_ref[...]) * _softplus(a_ref[0] + dtb_ref[...])
    beta = jax.nn.sigmoid(b_ref[0])
    eg = jnp.exp(gdec)
    qk = jnp.sum(q * k, axis=-1, keepdims=True)
    pad = jnp.concatenate([q, k, jnp.zeros((LANE - 2 * GDN_HEADS, GDN_DK), F32)], axis=0)
    cols = pad.T
    outs = []
    for hh in range(GDN_HEADS):
        s_h = s_ref[0, hh]
        qcol = cols[:, hh:hh + 1]
        kcol = cols[:, GDN_HEADS + hh:GDN_HEADS + hh + 1]
        b_h = beta[hh:hh + 1]
        e_h = eg[hh:hh + 1]
        k_s = jnp.sum(kcol * s_h, axis=0, keepdims=True)
        q_s = jnp.sum(qcol * s_h, axis=0, keepdims=True)
        vn = b_h * v[hh:hh + 1] - (b_h * e_h) * k_s
        outs.append(e_h * q_s + qk[hh:hh + 1] * vn)
        s_out_ref[0, hh] = s_h * e_h + kcol * vn
    o = jnp.concatenate(outs, axis=0)
    o_ref[0] = _rms(o, gnw_ref[...]) * _silu(zb_ref[0])


def _gdn_sample(x_s, cst, cw, a_s, b_s, a_log, dt_bias, s0, gnw, zb_s):
    dbsz = x_s.shape[0]
    nh3 = 3 * GDN_HEADS
    col = pl.BlockSpec((1, GDN_HEADS, 1), lambda b: (b, 0, 0))
    par = pl.BlockSpec((GDN_HEADS, 1), lambda b: (0, 0))
    head = pl.BlockSpec((1, GDN_HEADS, GDN_DV), lambda b: (b, 0, 0))
    sspec = pl.BlockSpec((1, GDN_HEADS, GDN_DK, GDN_DV), lambda b: (b, 0, 0, 0))
    return pl.pallas_call(
        _gdn_sample_kernel,
        grid=(dbsz,),
        in_specs=[pl.BlockSpec((1, nh3, GDN_DK), lambda b: (b, 0, 0)),
                  pl.BlockSpec((1, CONV_W - 1, nh3, GDN_DK), lambda b: (b, 0, 0, 0)),
                  pl.BlockSpec((CONV_W, nh3, GDN_DK), lambda b: (0, 0, 0)),
                  col, col, par, par, sspec,
                  pl.BlockSpec((1, GDN_DV), lambda b: (0, 0)),
                  head],
        out_specs=[head, sspec],
        out_shape=[jax.ShapeDtypeStruct((dbsz, GDN_HEADS, GDN_DV), F32),
                   jax.ShapeDtypeStruct(s0.shape, F32)],
        compiler_params=_cparams(("parallel",)),
        name="gdn_sample",
    )(x_s, cst, cw, a_s, b_s, a_log.reshape(-1, 1), dt_bias.reshape(-1, 1), s0, gnw.reshape(1, -1), zb_s)


def _permute_w_in(w_in):
    splits = (NSA_WIDTH, 2 * NSA_KV_WIDTH, 2 * NSA_KV_WIDTH, 2 * NSA_KV_WIDTH, 3 * NSA_HEADS, NSA_WIDTH,
              GDN_CONV_CH, GDN_HEADS, GDN_HEADS, GDN_V_WIDTH, w_in.shape[0], w_in.shape[0])
    bounds = np.cumsum(np.array(splits))[:-1].tolist()
    (q_a, kvc, kvs, kvw, g_nsa, z_a, qkv_b, a_b, b_b, z_b, gate_a, gate_b) = jnp.split(w_in, bounds, axis=1)
    small_pad = jnp.zeros((w_in.shape[0], H_WIDTH - C_SM - SM_B - GDN_HEADS), w_in.dtype)
    return jnp.concatenate([q_a, z_a, z_b, qkv_b, gate_a, gate_b, kvc, kvs, kvw, g_nsa, a_b, b_b, small_pad],
                           axis=1).astype(BF16)


def _cmp_w1(w1):
    return jnp.transpose(w1, (1, 2, 0, 3)).reshape(CMP_STRIDE, HEAD_DIM, CMP_R * CMP_HIDDEN).astype(BF16)


def _layer(x_p, x_s, cache_cmp, cache_slc, layer, st_win, st_s, st_conv, page_table, w):
    (norm_w, w_in, q_norm_w, k_norm_cmp_w, k_norm_slc_w, k_norm_win_w,
     cmp_w1_k, cmp_b1_k, cmp_w2_k, cmp_w1_v, cmp_b1_v, cmp_w2_v,
     conv_w, a_log, dt_bias, gdn_norm_w, w_pa, w_pb, w_o) = w
    bsz, t, d = x_p.shape
    dbsz, dseq, _ = x_s.shape
    assert dseq == 1 and d == C_GB - C_GA
    n_pages = page_table.shape[1]
    past_len = n_pages * PAGE_SIZE

    w_perm = _permute_w_in(w_in)
    head_w = jnp.stack([q_norm_w, k_norm_slc_w, k_norm_win_w])
    w1k, w1v = _cmp_w1(cmp_w1_k), _cmp_w1(cmp_w1_v)
    w2k, w2v = cmp_w2_k.astype(BF16), cmp_w2_v.astype(BF16)
    wpa, wpb, wo = w_pa.astype(BF16), w_pb.astype(BF16), w_o.astype(BF16)
    kvw_cols = 2 * NSA_KV_WIDTH

    n = bsz * t
    x2 = x_p.reshape(n, d)
    h = _proj(x2, norm_w, w_perm, head_w, tm=min(1024, n))
    pk, pv = _cmp_part_prompt(h, w1k, w1v, bsz, t)
    kc, vc = _cmp_finish(pk, pv, cmp_b1_k, w2k, cmp_b1_v, w2v, k_norm_cmp_w)
    oa = _attn_prompt(h, kc, vc, bsz, t)
    nc = t // GDN_CHUNK
    ab5 = h[:, C_SM + SM_A:C_SM + SM_B + GDN_HEADS].reshape(bsz * nc, GDN_CHUNK, 2, GDN_PAIRS, GDN_PAIR)
    col_form = lambda x: x.transpose(0, 3, 1, 2).reshape(bsz * nc, GDN_PROWS, GDN_PAIRS)
    a_row = ab5[:, :, 0].transpose(0, 2, 3, 1).reshape(bsz * nc, GDN_PAIRS, GDN_PROWS)
    u, wmat, att, qd, kd, edl = _gdn_intra(h, conv_w, col_form(ab5[:, :, 0]), col_form(ab5[:, :, 1]), a_row,
                                           a_log, dt_bias, bsz, t)
    ob, s_p = _gdn_inter(u, wmat, att, qd, kd, edl, h, gdn_norm_w, bsz, t)
    y_p = _out_proj(oa, ob, h, x2, wpa, wpb, wo, tm=min(256, n)).reshape(bsz, t, d)

    kv_shape = (bsz, t, 2, NSA_KV_HEADS, HEAD_DIM)
    cmp_p = h[:, C_KVC:C_KVC + kvw_cols].reshape(kv_shape)
    slc_p = h[:, C_KVS:C_KVS + kvw_cols].reshape(kv_shape)
    wlen = min(WINDOW, t)
    win_p = h.reshape(bsz, t, H_WIDTH)[:, t - wlen:, C_KVW:C_KVW + kvw_cols].reshape(
        bsz, wlen, 2, NSA_KV_HEADS, HEAD_DIM)
    conv_p = h.reshape(bsz, t, H_WIDTH)[:, t - (CONV_W - 1):, C_QKVB:C_QKVB + GDN_CONV_CH]

    xs2 = x_s.reshape(dbsz, d)
    hs = _proj(xs2, norm_w, w_perm, head_w, tm=dbsz)
    grp = (dbsz, NSA_KV_HEADS, NSA_GROUP, HEAD_DIM)
    q_s = hs[:, C_Q:C_Q + NSA_WIDTH].reshape(grp)
    za_s = hs[:, C_ZA:C_ZA + NSA_WIDTH].reshape(grp)
    kvc_s = hs[:, C_KVC:C_KVC + kvw_cols]
    kvs_s = hs[:, C_KVS:C_KVS + kvw_cols]
    kvw_s = hs[:, C_KVW:C_KVW + kvw_cols]
    pt_phys = page_table + layer * cache_cmp.shape[1]
    pk_s, pv_s = _cmp_part_sample(cache_cmp.reshape(-1, HEAD_DIM), pt_phys, w1k, w1v)
    kc_s, vc_s = _cmp_finish(pk_s, pv_s, cmp_b1_k, w2k, cmp_b1_v, w2v, k_norm_cmp_w)
    wb = st_win.shape[1]
    o_cmp_s, o_win_s, idx = _attn_sample1(q_s, kc_s, vc_s, st_win.reshape(dbsz, wb * KV_ROWS, HEAD_DIM),
                                          kvw_s.reshape(dbsz, 2 * NSA_KV_HEADS, HEAD_DIM), past_len)
    gates_s = hs[:, C_SM:C_SM + 3 * NSA_HEADS].reshape(dbsz, 3, NSA_KV_HEADS, NSA_GROUP, 1)
    oa_s = _attn_sample2(idx, pt_phys, q_s, cache_slc.reshape(-1, HEAD_DIM),
                         kvs_s.reshape(dbsz, 2, NSA_KV_HEADS, 1, HEAD_DIM), o_cmp_s, o_win_s, gates_s, za_s,
                         past_len)
    nh3 = 3 * GDN_HEADS
    ob_s, s_s = _gdn_sample(hs[:, C_QKVB:C_QKVB + GDN_CONV_CH].reshape(dbsz, nh3, GDN_DK),
                            st_conv.reshape(dbsz, CONV_W - 1, nh3, GDN_DK),
                            conv_w.reshape(CONV_W, nh3, GDN_DK),
                            hs[:, C_SM + SM_A:C_SM + SM_A + GDN_HEADS].reshape(dbsz, GDN_HEADS, 1),
                            hs[:, C_SM + SM_B:C_SM + SM_B + GDN_HEADS].reshape(dbsz, GDN_HEADS, 1),
                            a_log, dt_bias, st_s, gdn_norm_w,
                            hs[:, C_ZB:C_ZB + GDN_V_WIDTH].reshape(dbsz, GDN_HEADS, GDN_DV))
    y_s = _out_proj(oa_s.reshape(dbsz, NSA_WIDTH), ob_s.reshape(dbsz, GDN_V_WIDTH), hs, xs2,
                    wpa, wpb, wo, tm=dbsz).reshape(dbsz, 1, d)

    kv1 = (dbsz, 1, 2, NSA_KV_HEADS, HEAD_DIM)
    win_s = jnp.concatenate([st_win, kvw_s.reshape(kv1)], axis=1)[:, -wb:]
    conv_s = jnp.concatenate([st_conv, hs[:, None, C_QKVB:C_QKVB + GDN_CONV_CH]], axis=1)[:, -(CONV_W - 1):]
    return y_p, y_s, (cmp_p, slc_p, win_p, s_p, conv_p,
                      kvc_s.reshape(kv1), kvs_s.reshape(kv1), win_s, s_s, conv_s)


def _take_layer(x, layer):
    return x.reshape(x.shape[1:]) if x.shape[0] == 1 else x[layer]


def kernel(x_prompt, x_sample, cache_cmp_kv, cache_slc_kv, state_win_kv, state_gdn_S, state_gdn_conv, page_table, norm_w, w_in, q_norm_w, k_norm_cmp_w, k_norm_slc_w, k_norm_win_w, cmp_w1_k, cmp_b1_k, cmp_w2_k, cmp_w1_v, cmp_b1_v, cmp_w2_v, conv_w, a_log, dt_bias, gdn_norm_w, w_pa, w_pb, w_o):
    weights = (norm_w, w_in, q_norm_w, k_norm_cmp_w, k_norm_slc_w, k_norm_win_w,
               cmp_w1_k, cmp_b1_k, cmp_w2_k, cmp_w1_v, cmp_b1_v, cmp_w2_v,
               conv_w, a_log, dt_bias, gdn_norm_w, w_pa, w_pb, w_o)
    depth = norm_w.shape[0]
    y_p, y_s = x_prompt, x_sample
    per_layer = []
    for layer in range(depth):
        y_p, y_s, st = _layer(y_p, y_s, cache_cmp_kv, cache_slc_kv, layer, _take_layer(state_win_kv, layer),
                              _take_layer(state_gdn_S, layer), _take_layer(state_gdn_conv, layer), page_table,
                              tuple(_take_layer(wt, layer) for wt in weights))
        per_layer.append(st)
    stacked = [jnp.stack(a) for a in zip(*per_layer)]
    return (y_p, y_s, *stacked)
```

```python
import functools

import numpy as np
import jax
import jax.numpy as jnp
from jax import lax
from jax.experimental import pallas as pl
from jax.experimental.pallas import tpu as pltpu

F32 = jnp.float32
BF16 = jnp.bfloat16
HI = lax.Precision.HIGHEST

PAGE_SIZE = 128
HEAD_DIM = 128
NSA_HEADS = 8
NSA_KV_HEADS = 2
NSA_GROUP = NSA_HEADS // NSA_KV_HEADS
NSA_WIDTH = NSA_HEADS * HEAD_DIM
NSA_KV_WIDTH = NSA_KV_HEADS * HEAD_DIM
CMP_LEN = 32
CMP_STRIDE = 16
CMP_R = CMP_LEN // CMP_STRIDE
CMP_HIDDEN = 128
SEL_BLOCK = 64
SEL_TOPK = 16
N_LOCAL_SEL = 2
WINDOW = 512
Q_BLOCK = 128
FORCE_SCORE = 1e4
ALIBI_MAX = 8.0
GDN_HEADS = 8
GDN_DK = 128
GDN_DV = 128
GDN_QK_WIDTH = GDN_HEADS * GDN_DK
GDN_V_WIDTH = GDN_HEADS * GDN_DV
GDN_CONV_CH = 2 * GDN_QK_WIDTH + GDN_V_WIDTH
CONV_W = 4
GDN_CHUNK = 64
EPS = 1e-6
SCALE = HEAD_DIM ** -0.5
NEG = -1e30

C_Q = 0
C_ZA = 1024
C_ZB = 2048
C_QKVB = 3072
C_GA = 6144
C_GB = 8192
C_KVC = 10240
C_KVS = 10752
C_KVW = 11264
C_SM = 11776
H_WIDTH = 12288
TN = 512
PROJ_TN = 1024
SM_A = 3 * NSA_HEADS
SM_B = SM_A + GDN_HEADS

LANE = 128
KV_ROWS = 2 * NSA_KV_HEADS
PAGE_ROWS = PAGE_SIZE * KV_ROWS
VMEM_LIMIT = 56 * 1024 * 1024


def _cparams(sem, vmem=VMEM_LIMIT):
    return pltpu.CompilerParams(dimension_semantics=sem, vmem_limit_bytes=vmem)


def _silu(x):
    return x * jax.nn.sigmoid(x)


def _rms(a, w):
    return a * lax.rsqrt(jnp.mean(a * a, axis=-1, keepdims=True) + EPS) * w


def _l2(a):
    return a * lax.rsqrt(jnp.sum(a * a, axis=-1, keepdims=True) + EPS)


def _dot(a, b):
    return jnp.dot(a, b, preferred_element_type=F32)


def _dot_nt(a, b):
    return lax.dot_general(a, b, (((1,), (1,)), ((), ())), preferred_element_type=F32)


def _dot_hi(a, b):
    return jnp.dot(a, b, preferred_element_type=F32, precision=HI)


def _softplus(x):
    return jnp.maximum(x, 0.0) + jnp.log1p(jnp.exp(-jnp.abs(x)))


def _masked_softmax(s, mask):
    sm = jnp.where(mask, s, NEG)
    m = jnp.max(sm, axis=-1, keepdims=True)
    e = jnp.where(mask, jnp.exp(sm - m), 0.0)
    den = jnp.sum(e, axis=-1, keepdims=True)
    return e / jnp.where(den > 0, den, 1.0)


def _proj_kernel(x_ref, nw_ref, w_ref, hw_ref, o_ref, kvc_ref, kvs_ref, kvw_ref, xn_ref):
    j = pl.program_id(1)
    tm = x_ref.shape[0]
    kv_refs = (kvc_ref, kvs_ref, kvw_ref)
    kv_dest = _proj_kv_dest()

    @pl.when(j == 0)
    def _():
        x = x_ref[...]
        y = x * lax.rsqrt(jnp.mean(x * x, axis=-1, keepdims=True) + EPS)
        xn_ref[...] = (y * nw_ref[...]).astype(BF16)

    acc = _dot(xn_ref[...], w_ref[...])

    plan = _proj_norm_plan()
    for tile, groups in plan.items():
        @pl.when(j == tile)
        def _(groups=groups):
            parts = []
            for grp in range(PROJ_TN // LANE):
                a = acc[:, grp * LANE:(grp + 1) * LANE]
                parts.append(_rms(a, hw_ref[groups[grp]:groups[grp] + 1, :]) if grp in groups else a)
            o_ref[...] = jnp.concatenate(parts, axis=1)
            for grp, (seg, c) in kv_dest.get(tile, {}).items():
                kv_refs[seg][pl.ds(c, tm, stride=KV_ROWS), :] = parts[grp]

    assert set(kv_dest) <= set(plan)
    plain = functools.reduce(jnp.logical_and, [j != tile for tile in plan])

    @pl.when(plain)
    def _():
        o_ref[...] = acc


def _proj_kv_dest():
    dest = {}
    for seg, c0 in enumerate((C_KVC, C_KVS, C_KVW)):
        for c in range(KV_ROWS):
            col = c0 + c * LANE
            dest.setdefault(col // PROJ_TN, {})[(col % PROJ_TN) // LANE] = (seg, c)
    return dest


def _proj_norm_plan():
    plan = {}
    for c0, width, row in ((C_Q, NSA_WIDTH, 0), (C_KVS, NSA_KV_WIDTH, 1), (C_KVW, NSA_KV_WIDTH, 2)):
        for grp in range(width // LANE):
            col = c0 + grp * LANE
            plan.setdefault(col // PROJ_TN, {})[(col % PROJ_TN) // LANE] = row
    return plan


def _proj(x2d, norm_w, w_perm, head_w, tm):
    n, d = x2d.shape
    return pl.pallas_call(
        _proj_kernel,
        grid=(n // tm, H_WIDTH // PROJ_TN),
        in_specs=[pl.BlockSpec((tm, d), lambda i, j: (i, 0)),
                  pl.BlockSpec((1, d), lambda i, j: (0, 0)),
                  pl.BlockSpec((d, PROJ_TN), lambda i, j: (0, j)),
                  pl.BlockSpec((3, LANE), lambda i, j: (0, 0))],
        out_specs=[pl.BlockSpec((tm, PROJ_TN), lambda i, j: (i, j))]
        + [pl.BlockSpec((tm * KV_ROWS, LANE), lambda i, j: (i, 0), pipeline_mode=pl.Buffered(1))] * 3,
        out_shape=[jax.ShapeDtypeStruct((n, H_WIDTH), F32)]
        + [jax.ShapeDtypeStruct((n * KV_ROWS, LANE), F32)] * 3,
        scratch_shapes=[pltpu.VMEM((tm, d), BF16)],
        compiler_params=_cparams(("parallel", "arbitrary")),
        name="in_proj",
    )(x2d, norm_w.reshape(1, d), w_perm, head_w)


def _cmp_partials_body(strided_rows, w1k_ref, w1v_ref, pk_ref, pv_ref, n_chunks):
    for g in range(NSA_KV_HEADS):
        for kv, (w1_ref, p_ref) in enumerate(((w1k_ref, pk_ref), (w1v_ref, pv_ref))):
            acc = jnp.zeros((n_chunks, CMP_R * CMP_HIDDEN), F32)
            for s in range(CMP_STRIDE):
                xs = strided_rows(kv * NSA_KV_HEADS + g, s)
                acc = acc + _dot(xs.astype(BF16), w1_ref[s])
            p_ref[0, g] = acc


def _cmp_part_prompt_kernel(r0_ref, r1_ref, r2_ref, r3_ref, w1k_ref, w1v_ref, pk_ref, pv_ref, *, n_chunks):
    refs = (r0_ref, r1_ref, r2_ref, r3_ref)

    def strided_rows(c, s):
        return refs[c][pl.ds(s, n_chunks, stride=CMP_STRIDE), :]

    _cmp_partials_body(strided_rows, w1k_ref, w1v_ref, pk_ref, pv_ref, n_chunks)


def _cmp_part_prompt(h, w1k, w1v, bsz, t):
    n_chunks = t // CMP_STRIDE
    pshape = jax.ShapeDtypeStruct((bsz, NSA_KV_HEADS, n_chunks, CMP_R * CMP_HIDDEN), F32)
    pspec = pl.BlockSpec((1, NSA_KV_HEADS, n_chunks, CMP_R * CMP_HIDDEN), lambda b: (b, 0, 0, 0))
    wspec = pl.BlockSpec((CMP_STRIDE, HEAD_DIM, CMP_R * CMP_HIDDEN), lambda b: (0, 0, 0))
    return pl.pallas_call(
        functools.partial(_cmp_part_prompt_kernel, n_chunks=n_chunks),
        grid=(bsz,),
        in_specs=[pl.BlockSpec((t, HEAD_DIM), functools.partial(lambda c, b: (b, C_KVC // HEAD_DIM + c), c))
                  for c in range(2 * NSA_KV_HEADS)] + [wspec, wspec],
        out_specs=[pspec, pspec],
        out_shape=[pshape, pshape],
        compiler_params=_cparams(("parallel",)),
        name="cmp_part_prompt",
    )(h, h, h, h, w1k, w1v)


def _cmp_part_sample_kernel(pt_ref, cache_ref, w1k_ref, w1v_ref, pk_ref, pv_ref, buf_ref, sem_ref,
                            *, pages_per_step):
    i = pl.program_id(0)
    n_i = pl.num_programs(0)
    slot = i % 2
    n_chunks = pages_per_step * PAGE_SIZE // CMP_STRIDE
    chunk_rows = CMP_STRIDE * KV_ROWS

    def page_copy(step, p, to_slot):
        page = pt_ref[step * pages_per_step + p]
        return pltpu.make_async_copy(cache_ref.at[pl.ds(page * PAGE_ROWS, PAGE_ROWS)],
                                     buf_ref.at[to_slot, pl.ds(p * PAGE_ROWS, PAGE_ROWS)], sem_ref.at[to_slot])

    @pl.when(i == 0)
    def _():
        for p in range(pages_per_step):
            page_copy(0, p, 0).start()

    @pl.when(i + 1 < n_i)
    def _():
        for p in range(pages_per_step):
            page_copy(i + 1, p, 1 - slot).start()

    for p in range(pages_per_step):
        page_copy(i, p, slot).wait()

    for kv, (w1_ref, p_ref) in enumerate(((w1k_ref, pk_ref), (w1v_ref, pv_ref))):
        acc = jnp.zeros((NSA_KV_HEADS * n_chunks, CMP_R * CMP_HIDDEN), F32)
        for s in range(CMP_STRIDE):
            xs = jnp.concatenate(
                [buf_ref[slot, pl.ds(s * KV_ROWS + kv * NSA_KV_HEADS + g, n_chunks, stride=chunk_rows), :]
                 for g in range(NSA_KV_HEADS)], axis=0)
            acc = acc + _dot(xs.astype(BF16), w1_ref[s])
        for g in range(NSA_KV_HEADS):
            p_ref[0, g] = acc[g * n_chunks:(g + 1) * n_chunks]


def _cmp_part_sample(cache_rows, page_table, w1k, w1v):
    dbsz, n_pages = page_table.shape
    pages_per_step = min(32, n_pages)
    n_steps = n_pages // pages_per_step
    cps = pages_per_step * PAGE_SIZE // CMP_STRIDE
    n_chunks = n_pages * PAGE_SIZE // CMP_STRIDE
    pshape = jax.ShapeDtypeStruct((dbsz, NSA_KV_HEADS, n_chunks, CMP_R * CMP_HIDDEN), F32)
    pspec = pl.BlockSpec((1, NSA_KV_HEADS, cps, CMP_R * CMP_HIDDEN),
                         lambda i, pt: (i // n_steps, 0, i % n_steps, 0))
    wspec = pl.BlockSpec((CMP_STRIDE, HEAD_DIM, CMP_R * CMP_HIDDEN), lambda i, pt: (0, 0, 0))
    grid_spec = pltpu.PrefetchScalarGridSpec(
        num_scalar_prefetch=1,
        grid=(dbsz * n_steps,),
        in_specs=[pl.BlockSpec(memory_space=pl.ANY), wspec, wspec],
        out_specs=[pspec, pspec],
        scratch_shapes=[pltpu.VMEM((2, pages_per_step * PAGE_ROWS, HEAD_DIM), F32),
                        pltpu.SemaphoreType.DMA((2,))],
    )
    return pl.pallas_call(
        functools.partial(_cmp_part_sample_kernel, pages_per_step=pages_per_step),
        grid_spec=grid_spec,
        out_shape=[pshape, pshape],
        compiler_params=_cparams(("arbitrary",)),
        name="cmp_part_sample",
    )(page_table.reshape(-1), cache_rows, w1k, w1v)


def _cmp_finish_kernel(pk_ref, pv_ref, b1k_ref, w2k_ref, b1v_ref, w2v_ref, nk_ref, kc_ref, vc_ref,
                       *, n_chunks):
    for g in range(NSA_KV_HEADS):
        for kv, (p_ref, b1_ref, w2_ref, o_ref) in enumerate(((pk_ref, b1k_ref, w2k_ref, kc_ref),
                                                            (pv_ref, b1v_ref, w2v_ref, vc_ref))):
            part = p_ref[0, g]
            nxt = pltpu.roll(part[:, CMP_HIDDEN:], n_chunks - 1, 0)
            hid = _silu(part[:, :CMP_HIDDEN] + nxt + b1_ref[...])
            out = _dot(hid.astype(BF16), w2_ref[...])
            if kv == 0:
                out = _rms(out, nk_ref[...])
            o_ref[0, g] = out


def _cmp_finish(pk, pv, b1k, w2k, b1v, w2v, nk):
    bsz, _, n_chunks, _ = pk.shape
    pspec = pl.BlockSpec((1, NSA_KV_HEADS, n_chunks, CMP_R * CMP_HIDDEN), lambda b: (b, 0, 0, 0))
    vspec = pl.BlockSpec((1, CMP_HIDDEN), lambda b: (0, 0))
    wspec = pl.BlockSpec((CMP_HIDDEN, HEAD_DIM), lambda b: (0, 0))
    oshape = jax.ShapeDtypeStruct((bsz, NSA_KV_HEADS, n_chunks, HEAD_DIM), F32)
    ospec = pl.BlockSpec((1, NSA_KV_HEADS, n_chunks, HEAD_DIM), lambda b: (b, 0, 0, 0))
    return pl.pallas_call(
        functools.partial(_cmp_finish_kernel, n_chunks=n_chunks),
        grid=(bsz,),
        in_specs=[pspec, pspec, vspec, wspec, vspec, wspec, vspec],
        out_specs=[ospec, ospec],
        out_shape=[oshape, oshape],
        compiler_params=_cparams(("parallel",)),
        name="cmp_finish",
    )(pk, pv, b1k.reshape(1, -1), w2k, b1v.reshape(1, -1), w2v, nk.reshape(1, -1))


def _sel_map(n_cmp, n_sel, n_sel_pad):
    cs = np.arange(n_cmp) * CMP_STRIDE
    ss = np.arange(n_sel) * SEL_BLOCK
    ov = np.minimum(cs[:, None] + CMP_LEN, ss[None] + SEL_BLOCK) - np.maximum(cs[:, None], ss[None])
    m = np.zeros((n_cmp, n_sel_pad), np.float32)
    m[:, :n_sel] = np.clip(ov, 0, None).astype(np.float32) / CMP_LEN
    return jnp.asarray(m, dtype=BF16)


def _topk_rank_rows(score, n_valid):
    j = lax.broadcasted_iota(jnp.int32, score.shape, 1)
    cnt = jnp.zeros(score.shape, F32)
    for jp in range(n_valid):
        col = score[:, jp:jp + 1]
        earlier = (j > jp).astype(F32)
        cnt = cnt + jnp.where(col > score, 1.0, jnp.where(col == score, earlier, 0.0))
    return cnt


QT = 256
MASK_BIG = 16384.0


def _attn_prompt_kernel(q_ref, ks_ref, vs_ref, kw_ref, vw_ref, kc_ref, vc_ref, sm_ref, za_ref,
                        selmap_ref, kaug_ref, caug_ref, padaug_ref, band_ref, o_ref,
                        ksa_ref, vsb_ref, kwa_ref, vwb_ref, kca_ref, vcb_ref, *, t_len):
    g = pl.program_id(1)
    qb = pl.program_id(2)
    qstart = pl.multiple_of(qb * QT, QT)
    n_sel = t_len // SEL_BLOCK
    rows = NSA_GROUP * QT

    @pl.when(qb == 0)
    def _():
        ksa_ref[:, 0:HEAD_DIM] = ks_ref[...].astype(BF16)
        ksa_ref[:, HEAD_DIM:] = kaug_ref[...]
        vsb_ref[...] = vs_ref[...].astype(BF16)
        kwa_ref[0:WINDOW, 0:HEAD_DIM] = jnp.zeros((WINDOW, HEAD_DIM), BF16)
        kwa_ref[0:WINDOW, HEAD_DIM:] = padaug_ref[...]
        kwa_ref[WINDOW:, 0:HEAD_DIM] = kw_ref[...].astype(BF16)
        kwa_ref[WINDOW:, HEAD_DIM:] = kaug_ref[...]
        vwb_ref[0:WINDOW, :] = jnp.zeros((WINDOW, HEAD_DIM), BF16)
        vwb_ref[WINDOW:, :] = vw_ref[...].astype(BF16)
        kca_ref[:, 0:HEAD_DIM] = kc_ref[0, 0].astype(BF16)
        kca_ref[:, HEAD_DIM:] = caug_ref[...]
        vcb_ref[...] = vc_ref[0, 0].astype(BF16)

    q = q_ref[...]
    qsb = (jnp.concatenate([q[:, r * HEAD_DIM:(r + 1) * HEAD_DIM] for r in range(NSA_GROUP)], axis=0)
           * SCALE).astype(BF16)
    row = lax.broadcasted_iota(jnp.int32, (rows, 1), 0)
    r_of = row // QT
    tloc = row % QT
    qpos = qstart + tloc
    slope = jnp.zeros((rows, 1), F32)
    for r in range(NSA_GROUP):
        s_r = jnp.where(g == 0, 2.0 ** -(r + 1.0), 2.0 ** -(r + 1.0 + NSA_GROUP)).astype(F32)
        slope = jnp.where(r_of == r, s_r, slope)
    lane = lax.broadcasted_iota(jnp.int32, (rows, LANE), 1)
    pos_cols = jnp.where(lane == n_sel, slope * 256.0, jnp.where(lane == n_sel + 1, slope, 0.0))
    q_w = jnp.concatenate([qsb, jnp.where(lane == n_sel + 2, -MASK_BIG, pos_cols).astype(BF16)], axis=1)

    n_cmp = kca_ref.shape[0]
    kend = lax.broadcasted_iota(jnp.int32, (1, n_cmp), 1) * CMP_STRIDE + (CMP_LEN - 1)
    p_cmp = _masked_softmax(_dot_nt(q_w, kca_ref[...]), (qpos - kend) >= 0).astype(BF16)
    o_cmp = _dot(p_cmp, vcb_ref[...])
    imp4 = _dot(p_cmp, selmap_ref[...])
    imp = imp4[0:QT]
    for r in range(1, NSA_GROUP):
        imp = imp + imp4[r * QT:(r + 1) * QT]

    tq = qstart + lax.broadcasted_iota(jnp.int32, (QT, 1), 0)
    j = lax.broadcasted_iota(jnp.int32, (QT, LANE), 1)
    visible = j * SEL_BLOCK <= tq
    back = tq // SEL_BLOCK - j
    forced = (j == 0) | ((back >= 0) & (back < N_LOCAL_SEL))
    score = jnp.where(visible, jnp.where(forced, FORCE_SCORE, imp), -FORCE_SCORE)
    score = jnp.where(j < n_sel, score, -3e38)
    rank = _topk_rank_rows(score, n_sel)
    sel = jnp.where((rank < min(SEL_TOPK, n_sel)) & visible & (j < n_sel), 1.0, 0.0)
    sel4 = jnp.concatenate([sel] * NSA_GROUP, axis=0)
    q_s = jnp.concatenate([qsb, jnp.where(lane < n_sel, (sel4 - 1.0) * MASK_BIG, pos_cols).astype(BF16)],
                          axis=1)

    def online(sc, k0, carry):
        m, l, acc = carry
        m_new = jnp.maximum(m, jnp.max(sc, axis=-1, keepdims=True))
        alpha = jnp.exp(m - m_new)
        p = jnp.exp(sc - m_new)
        l = alpha * l + jnp.sum(p, axis=-1, keepdims=True)
        acc = alpha * acc + _dot(p.astype(BF16), vsb_ref[pl.ds(k0, QT), :])
        return m_new, l, acc

    def past_chunk(c, carry):
        k0 = pl.multiple_of(c * QT, QT)
        return online(_dot_nt(q_s, ksa_ref[pl.ds(k0, QT), :]), k0, carry)

    init = (jnp.full((rows, 1), NEG, F32), jnp.zeros((rows, 1), F32), jnp.zeros((rows, HEAD_DIM), F32))
    carry = lax.fori_loop(0, qb, past_chunk, init)
    ki = lax.broadcasted_iota(jnp.int32, (1, QT), 1)
    diag = jnp.where(tloc >= ki, _dot_nt(q_s, ksa_ref[pl.ds(qstart, QT), :]), -MASK_BIG)
    _, l_s, acc_s = online(diag, qstart, carry)
    o_slc = acc_s / l_s

    span = QT + WINDOW
    band = band_ref[...]
    sw = _dot_nt(q_w, kwa_ref[pl.ds(qstart, span), :]) + jnp.concatenate([band] * NSA_GROUP, axis=0)
    e_win = jnp.exp(sw - jnp.max(sw, axis=-1, keepdims=True))
    o_win = (_dot(e_win.astype(BF16), vwb_ref[pl.ds(qstart, span), :])
             / jnp.sum(e_win, axis=-1, keepdims=True))

    sm = sm_ref[...]
    outs = []
    for r in range(NSA_GROUP):
        tot = jnp.zeros((QT, HEAD_DIM), F32)
        for br, o_br in enumerate((o_cmp, o_slc, o_win)):
            c0 = br * NSA_HEADS + r
            gate = jnp.where(g == 0, sm[:, c0:c0 + 1], sm[:, c0 + NSA_GROUP:c0 + NSA_GROUP + 1])
            tot = tot + jax.nn.sigmoid(gate) * o_br[r * QT:(r + 1) * QT]
        outs.append(tot * _silu(za_ref[:, r * HEAD_DIM:(r + 1) * HEAD_DIM]))
    o_ref[...] = jnp.concatenate(outs, axis=1)


def _attn_prompt(h, kc, vc, bsz, t):
    nqb = t // QT
    n_cmp = t // CMP_STRIDE
    n_sel = t // SEL_BLOCK
    span = QT + WINDOW
    assert n_sel + 3 <= LANE and t % QT == 0 and t + CMP_LEN < 256 * 256
    selmap = _sel_map(n_cmp, n_sel, LANE)

    def aug_cols(pos, onehot_block):
        a = np.zeros((pos.shape[0], LANE), np.float32)
        if onehot_block:
            a[np.arange(pos.shape[0]), pos // SEL_BLOCK] = 1.0
        a[:, n_sel] = pos // 256
        a[:, n_sel + 1] = pos % 256
        return jnp.asarray(a, dtype=BF16)

    kaug = aug_cols(np.arange(t), True)
    caug = aug_cols(np.arange(n_cmp) * CMP_STRIDE + CMP_LEN - 1, False)
    pad = np.zeros((WINDOW, LANE), np.float32)
    pad[:, n_sel + 2] = 1.0
    padaug = jnp.asarray(pad, dtype=BF16)
    ti = np.arange(QT)[:, None]
    ii = np.arange(span)[None]
    band = jnp.asarray(np.where((ii >= ti) & (ii <= ti + WINDOW), 0.0, -MASK_BIG).astype(np.float32))
    gw = NSA_GROUP * HEAD_DIM
    row_blk = lambda b, g, i: b * nqb + i
    kv_spec = lambda c0: pl.BlockSpec((t, HEAD_DIM), lambda b, g, i: (b, c0 // HEAD_DIM + g))
    const = lambda shape: pl.BlockSpec(shape, lambda b, g, i: (0,) * len(shape))
    return pl.pallas_call(
        functools.partial(_attn_prompt_kernel, t_len=t),
        grid=(bsz, NSA_KV_HEADS, nqb),
        in_specs=[pl.BlockSpec((QT, gw), lambda b, g, i: (row_blk(b, g, i), C_Q // gw + g)),
                  kv_spec(C_KVS), kv_spec(C_KVS + NSA_KV_WIDTH),
                  kv_spec(C_KVW), kv_spec(C_KVW + NSA_KV_WIDTH),
                  pl.BlockSpec((1, 1, n_cmp, HEAD_DIM), lambda b, g, i: (b, g, 0, 0)),
                  pl.BlockSpec((1, 1, n_cmp, HEAD_DIM), lambda b, g, i: (b, g, 0, 0)),
                  pl.BlockSpec((QT, TN), lambda b, g, i: (row_blk(b, g, i), C_SM // TN)),
                  pl.BlockSpec((QT, gw), lambda b, g, i: (row_blk(b, g, i), C_ZA // gw + g)),
                  const((n_cmp, LANE)), const((t, LANE)), const((n_cmp, LANE)), const((WINDOW, LANE)),
                  const((QT, span))],
        out_specs=pl.BlockSpec((QT, gw), lambda b, g, i: (row_blk(b, g, i), g)),
        out_shape=jax.ShapeDtypeStruct((bsz * t, NSA_WIDTH), F32),
        scratch_shapes=[pltpu.VMEM((t, 2 * HEAD_DIM), BF16), pltpu.VMEM((t, HEAD_DIM), BF16),
                        pltpu.VMEM((WINDOW + t, 2 * HEAD_DIM), BF16), pltpu.VMEM((WINDOW + t, HEAD_DIM), BF16),
                        pltpu.VMEM((n_cmp, 2 * HEAD_DIM), BF16), pltpu.VMEM((n_cmp, HEAD_DIM), BF16)],
        compiler_params=_cparams(("parallel", "parallel", "arbitrary")),
        name="attn_prompt",
    )(h, h, h, h, h, kc, vc, h, h, selmap, kaug, caug, padaug, band)


SUBLANES = 8
GDN_PAIR = 2
GDN_PAIRS = GDN_HEADS // GDN_PAIR
GDN_PROWS = GDN_PAIR * GDN_CHUNK


def _split2(x):
    hi = x.astype(BF16)
    return hi, (x - hi.astype(F32)).astype(BF16)


def _split3(x):
    hi = x.astype(BF16)
    r = x - hi.astype(F32)
    mid = r.astype(BF16)
    return hi, mid, (r - mid.astype(F32)).astype(BF16)


def _dot_x3(a_parts, b_parts):
    a_hi, a_lo = a_parts
    b_hi, b_lo = b_parts
    return _dot(jnp.concatenate([a_hi, a_lo, a_hi], axis=1), jnp.concatenate([b_hi, b_hi, b_lo], axis=0))


def _gdn_intra_kernel(x_ref, prev_ref, cw_ref, acol_ref, bcol_ref, arow_ref, alogc_ref, dtbc_ref, alogr_ref,
                      dtbr_ref, u_ref, w_ref, att_ref, qd_ref, kd_ref, edl_ref, *, chunks_per_seq):
    pr = GDN_PROWS
    first = pl.program_id(0) % chunks_per_seq == 0
    full = jnp.concatenate([jnp.where(first, 0.0, prev_ref[...]), x_ref[...]], axis=0)
    conv = None
    for i in range(CONV_W):
        dly = CONV_W - 1 - i
        xs = full if dly == 0 else pltpu.roll(full, dly, 0)
        term = xs[SUBLANES:] * cw_ref[i:i + 1, :]
        conv = term if conv is None else conv + term
    conv = _silu(conv)

    def head_cols(kind, hh):
        c0 = kind * GDN_QK_WIDTH + hh * GDN_DK
        return conv[:, c0:c0 + GDN_DK]

    def stack(kind, p):
        parts = [head_cols(kind, GDN_PAIR * p + i) for i in range(GDN_PAIR)]
        if kind == 0:
            parts = [_l2(x) * (GDN_DK ** -0.5) for x in parts]
        elif kind == 1:
            parts = [_l2(x) for x in parts]
        return jnp.concatenate(parts, axis=0)

    ri = lax.broadcasted_iota(jnp.int32, (pr, pr), 0)
    ci = lax.broadcasted_iota(jnp.int32, (pr, pr), 1)
    same = (ri // GDN_CHUNK) == (ci // GDN_CHUNK)
    tril = same & (ri >= ci)
    stril = same & (ri > ci)
    eye = (ri == ci).astype(F32)
    lo_b = tril.astype(BF16)
    ut_b = (same & (ri <= ci)).astype(BF16)
    rcol = lax.broadcasted_iota(jnp.int32, (pr, 1), 0)

    gcol = -jnp.exp(alogc_ref[...]) * _softplus(acol_ref[0] + dtbc_ref[...])
    beta = jax.nn.sigmoid(bcol_ref[0])
    grow = -jnp.exp(alogr_ref[...]) * _softplus(arow_ref[0] + dtbr_ref[...])
    dec_col = sum(_dot(lo_b, part) for part in _split3(gcol))
    dec_row = sum(_dot(part, ut_b) for part in _split3(grow))

    ks, kbs, kbfs, lmats, vbs, qs, dcs = [], [], [], [], [], [], []
    for p in range(GDN_PAIRS):
        k2 = stack(1, p)
        dc = dec_col[:, p:p + 1]
        b2 = beta[:, p:p + 1]
        dmat = jnp.where(tril, dc - dec_row[p:p + 1, :], 0.0)
        ks.append(k2)
        kbs.append(k2 * b2)
        kbfs.append(k2.astype(BF16))
        lmats.append(jnp.where(tril, jnp.exp(dmat), 0.0))
        vbs.append(stack(2, p) * b2)
        qs.append(stack(0, p))
        dcs.append(dc)

    npows = [-jnp.where(stril, _dot_nt(kbs[p].astype(BF16), kbfs[p]) * lmats[p], 0.0) for p in range(GDN_PAIRS)]
    tms = [eye + n for n in npows]
    parts = [_split2(n) for n in npows]
    npows = [_dot_x3(parts[p], parts[p]) for p in range(GDN_PAIRS)]
    for _ in range(4):
        for p in range(GDN_PAIRS):
            n_hi, n_lo = _split2(npows[p])
            s_hi, s_lo = _split2(tms[p])
            both = _dot_x3((n_hi, n_lo), (jnp.concatenate([n_hi, s_hi], axis=1),
                                          jnp.concatenate([n_lo, s_lo], axis=1)))
            npows[p] = both[:, :pr]
            tms[p] = tms[p] + both[:, pr:]
    tms = [tms[p] + _dot_x3(_split2(npows[p]), _split2(tms[p])) for p in range(GDN_PAIRS)]

    edl_rows = []
    for p in range(GDN_PAIRS):
        tm_b = tms[p].astype(BF16)
        dc = dcs[p]
        edec = jnp.exp(dc)
        edl_rows += [jnp.broadcast_to(jnp.exp(dc[(i + 1) * GDN_CHUNK - 1:(i + 1) * GDN_CHUNK, :]), (1, LANE))
                     for i in range(GDN_PAIR)]
        dl = jnp.where(rcol < GDN_CHUNK, dc[GDN_CHUNK - 1:GDN_CHUNK, :], dc[pr - 1:pr, :])
        uw = _dot(tm_b, jnp.concatenate([vbs[p].astype(BF16), (kbs[p] * edec).astype(BF16)], axis=1))
        u_ref[0, p] = uw[:, :GDN_DV]
        w_ref[0, p] = uw[:, GDN_DV:].astype(BF16)
        att_ref[0, p] = jnp.where(tril, _dot_nt(qs[p].astype(BF16), kbfs[p]) * lmats[p], 0.0).astype(BF16)
        qd_ref[0, p] = (qs[p] * edec).astype(BF16)
        kd_ref[0, p] = (ks[p] * jnp.exp(dl - dc)).astype(BF16)
    edl_ref[0] = jnp.concatenate(edl_rows, axis=0)


def _gdn_intra(h, conv_w, a_col, b_col, a_row, a_log, dt_bias, bsz, t):
    nc = t // GDN_CHUNK
    n_chunks = bsz * nc
    pr = GDN_PROWS
    per_chunk = GDN_CHUNK // SUBLANES
    alog_c = jnp.tile(a_log.reshape(GDN_PAIRS, GDN_PAIR).T[:, None, :], (1, GDN_CHUNK, 1)).reshape(pr, GDN_PAIRS)
    dtb_c = jnp.tile(dt_bias.reshape(GDN_PAIRS, GDN_PAIR).T[:, None, :], (1, GDN_CHUNK, 1)).reshape(pr, GDN_PAIRS)
    alog_r = jnp.repeat(a_log.reshape(GDN_PAIRS, GDN_PAIR), GDN_CHUNK, axis=1)
    dtb_r = jnp.repeat(dt_bias.reshape(GDN_PAIRS, GDN_PAIR), GDN_CHUNK, axis=1)
    cspec = pl.BlockSpec((1, pr, GDN_PAIRS), lambda i: (i, 0, 0))
    cpar = pl.BlockSpec((pr, GDN_PAIRS), lambda i: (0, 0))
    rpar = pl.BlockSpec((GDN_PAIRS, pr), lambda i: (0, 0))
    mspec = pl.BlockSpec((1, GDN_PAIRS, pr, GDN_DK), lambda i: (i, 0, 0, 0))
    mshape = lambda dt: jax.ShapeDtypeStruct((n_chunks, GDN_PAIRS, pr, GDN_DK), dt)
    qkv_blk = C_QKVB // GDN_CONV_CH
    return pl.pallas_call(
        functools.partial(_gdn_intra_kernel, chunks_per_seq=nc),
        grid=(n_chunks,),
        in_specs=[pl.BlockSpec((GDN_CHUNK, GDN_CONV_CH), lambda i: (i, qkv_blk)),
                  pl.BlockSpec((SUBLANES, GDN_CONV_CH), lambda i: (jnp.maximum(i * per_chunk - 1, 0), qkv_blk)),
                  pl.BlockSpec((CONV_W, GDN_CONV_CH), lambda i: (0, 0)),
                  cspec, cspec,
                  pl.BlockSpec((1, GDN_PAIRS, pr), lambda i: (i, 0, 0)),
                  cpar, cpar, rpar, rpar],
        out_specs=[mspec] * 5 + [pl.BlockSpec((1, GDN_HEADS, LANE), lambda i: (i, 0, 0))],
        out_shape=[mshape(F32), mshape(BF16), mshape(BF16), mshape(BF16), mshape(BF16),
                   jax.ShapeDtypeStruct((n_chunks, GDN_HEADS, LANE), F32)],
        compiler_params=_cparams(("parallel",)),
        name="gdn_intra",
    )(h, h, conv_w, a_col, b_col, a_row, alog_c, dtb_c, alog_r, dtb_r)


def _gdn_inter_kernel(u_ref, w_ref, att_ref, qd_ref, kd_ref, edl_ref, zb_ref, gnw_ref,
                      o_ref, s_out_ref, s_ref):
    c = pl.program_id(1)
    n_c = pl.num_programs(1)
    cs = GDN_CHUNK

    @pl.when(c == 0)
    def _():
        s_ref[...] = jnp.zeros_like(s_ref)

    heads = [(p, i, GDN_PAIR * p + i, slice(i * cs, (i + 1) * cs))
             for p in range(GDN_PAIRS) for i in range(GDN_PAIR)]
    s_old = [s_ref[hh] for (_, _, hh, _) in heads]
    s_bf = [s.astype(BF16) for s in s_old]
    ws = [_dot(w_ref[0, p, rows, :], s_bf[hh]) for (p, _, hh, rows) in heads]
    qss = [_dot(qd_ref[0, p, rows, :], s_bf[hh]) for (p, _, hh, rows) in heads]
    vn_b, o2 = [], []
    for p in range(GDN_PAIRS):
        vn = u_ref[0, p] - jnp.concatenate(ws[GDN_PAIR * p:GDN_PAIR * (p + 1)], axis=0)
        vn_b.append(vn.astype(BF16))
        o2.append(jnp.concatenate(qss[GDN_PAIR * p:GDN_PAIR * (p + 1)], axis=0) + _dot(att_ref[0, p], vn_b[p]))
    outs = []
    for (p, i, hh, rows) in heads:
        edl = edl_ref[0, hh:hh + 1, :]
        upd = lax.dot_general(kd_ref[0, p, rows, :], vn_b[p][rows], (((0,), (0,)), ((), ())),
                              preferred_element_type=F32)
        s_ref[hh] = s_old[hh] * edl + upd
        outs.append(_rms(o2[p][rows], gnw_ref[...]) * _silu(zb_ref[:, hh * GDN_DV:(hh + 1) * GDN_DV]))
    o_ref[...] = jnp.concatenate(outs, axis=1)

    @pl.when(c == n_c - 1)
    def _():
        s_out_ref[0] = s_ref[...]


def _gdn_inter(u, w, att, qd, kd, edl, h, gnw, bsz, t):
    nc = t // GDN_CHUNK
    wd = GDN_V_WIDTH
    pr = GDN_PROWS
    row = lambda b, c: b * nc + c
    mspec = pl.BlockSpec((1, GDN_PAIRS, pr, GDN_DK), lambda b, c: (row(b, c), 0, 0, 0))
    return pl.pallas_call(
        _gdn_inter_kernel,
        grid=(bsz, nc),
        in_specs=[mspec] * 5 + [pl.BlockSpec((1, GDN_HEADS, LANE), lambda b, c: (row(b, c), 0, 0)),
                                pl.BlockSpec((GDN_CHUNK, wd), lambda b, c: (row(b, c), C_ZB // wd)),
                                pl.BlockSpec((1, GDN_DV), lambda b, c: (0, 0))],
        out_specs=[pl.BlockSpec((GDN_CHUNK, wd), lambda b, c: (row(b, c), 0)),
                   pl.BlockSpec((1, GDN_HEADS, GDN_DK, GDN_DV), lambda b, c: (b, 0, 0, 0))],
        out_shape=[jax.ShapeDtypeStruct((bsz * t, wd), F32),
                   jax.ShapeDtypeStruct((bsz, GDN_HEADS, GDN_DK, GDN_DV), F32)],
        scratch_shapes=[pltpu.VMEM((GDN_HEADS, GDN_DK, GDN_DV), F32)],
        compiler_params=_cparams(("parallel", "arbitrary")),
        name="gdn_inter",
    )(u, w, att, qd, kd, edl, h, gnw.reshape(1, -1))


def _out_kernel(oa_ref, ob_ref, ga_ref, gb_ref, x_ref, wpa_ref, wpb_ref, wo_ref, y_ref):
    ya = _dot(oa_ref[...].astype(BF16), wpa_ref[...])
    yb = _dot(ob_ref[...].astype(BF16), wpb_ref[...])
    m = jax.nn.sigmoid(ga_ref[...]) * ya + jax.nn.sigmoid(gb_ref[...]) * yb
    y_ref[...] = x_ref[...] + _dot(m.astype(BF16), wo_ref[...])


def _out_proj(oa, ob, h, x2d, w_pa, w_pb, w_o, tm):
    n, d = x2d.shape
    once = pl.Buffered(1)
    return pl.pallas_call(
        _out_kernel,
        grid=(n // tm,),
        in_specs=[pl.BlockSpec((tm, NSA_WIDTH), lambda i: (i, 0)),
                  pl.BlockSpec((tm, GDN_V_WIDTH), lambda i: (i, 0)),
                  pl.BlockSpec((tm, d), lambda i: (i, C_GA // d)),
                  pl.BlockSpec((tm, d), lambda i: (i, C_GB // d)),
                  pl.BlockSpec((tm, d), lambda i: (i, 0)),
                  pl.BlockSpec((NSA_WIDTH, d), lambda i: (0, 0), pipeline_mode=once),
                  pl.BlockSpec((GDN_V_WIDTH, d), lambda i: (0, 0), pipeline_mode=once),
                  pl.BlockSpec((d, d), lambda i: (0, 0), pipeline_mode=once)],
        out_specs=pl.BlockSpec((tm, d), lambda i: (i, 0)),
        out_shape=jax.ShapeDtypeStruct((n, d), F32),
        compiler_params=_cparams(("parallel",)),
        name="out_proj",
    )(oa, ob, h, h, x2d, w_pa, w_pb, w_o)


def _group_slopes(g):
    r_of = lax.broadcasted_iota(jnp.int32, (NSA_GROUP, 1), 0)
    slope = jnp.zeros((NSA_GROUP, 1), F32)
    for r in range(NSA_GROUP):
        s_r = jnp.where(g == 0, 2.0 ** -(r + 1.0), 2.0 ** -(r + 1.0 + NSA_GROUP)).astype(F32)
        slope = jnp.where(r_of == r, s_r, slope)
    return slope


def _attn_sample1_kernel(q_ref, kc_ref, vc_ref, selmap_ref, stw_ref, kvw_ref,
                         ocmp_ref, owin_ref, idx_ref, *, past_len, n_sel, n_sel_pad):
    qpos = past_len
    n_cmp = kc_ref.shape[2]
    wb = stw_ref.shape[1] // KV_ROWS
    rowi = lax.broadcasted_iota(jnp.int32, (LANE, n_sel_pad), 0)
    jrow = lax.broadcasted_iota(jnp.int32, (1, n_sel_pad), 1)
    scores = []
    for g in range(NSA_KV_HEADS):
        slope = _group_slopes(g)
        qg = q_ref[0, g].astype(BF16)
        kend = lax.broadcasted_iota(jnp.int32, (1, n_cmp), 1) * CMP_STRIDE + (CMP_LEN - 1)
        d = qpos - kend
        s = _dot_nt(qg, kc_ref[0, g].astype(BF16)) * SCALE - slope * d.astype(F32)
        p = _masked_softmax(s, d >= 0).astype(BF16)
        ocmp_ref[0, g] = _dot(p, vc_ref[0, g].astype(BF16))
        imp = jnp.sum(_dot(p, selmap_ref[...]), axis=0, keepdims=True)
        visible = jrow * SEL_BLOCK <= qpos
        back = qpos // SEL_BLOCK - jrow
        forced = (jrow == 0) | ((back >= 0) & (back < N_LOCAL_SEL))
        sc = jnp.where(visible, jnp.where(forced, FORCE_SCORE, imp), -FORCE_SCORE)
        scores.append(jnp.where(jrow < n_sel, sc, -3e38))
        kw = stw_ref[0, pl.ds(g, wb, stride=KV_ROWS), :].astype(BF16)
        vw = stw_ref[0, pl.ds(NSA_KV_HEADS + g, wb, stride=KV_ROWS), :].astype(BF16)
        knew = kvw_ref[0, g:g + 1, :].astype(BF16).astype(F32)
        vnew = kvw_ref[0, NSA_KV_HEADS + g:NSA_KV_HEADS + g + 1, :].astype(BF16).astype(F32)
        dwin = wb - lax.broadcasted_iota(jnp.int32, (1, wb), 1)
        sw = _dot_nt(qg, kw) * SCALE - slope * dwin.astype(F32)
        ok = (dwin >= 0) & (dwin <= WINDOW)
        sw = jnp.where(ok, sw, NEG)
        s_new = jnp.sum(qg.astype(F32) * knew, axis=-1, keepdims=True) * SCALE
        m = jnp.maximum(jnp.max(sw, axis=-1, keepdims=True), s_new)
        e = jnp.where(ok, jnp.exp(sw - m), 0.0)
        e_new = jnp.exp(s_new - m)
        den = jnp.sum(e, axis=-1, keepdims=True) + e_new
        owin_ref[0, g] = (_dot((e / den).astype(BF16), vw)
                          + (e_new / den).astype(BF16).astype(F32) * vnew)

    stacked = jnp.zeros((LANE, n_sel_pad), F32)
    for g in range(NSA_KV_HEADS):
        stacked = jnp.where(rowi == g, scores[g], stacked)
    cols = stacked.T
    ii = lax.broadcasted_iota(jnp.int32, (n_sel_pad, n_sel_pad), 0)
    jj = lax.broadcasted_iota(jnp.int32, (n_sel_pad, n_sel_pad), 1)
    earlier = (ii < jj).astype(F32)
    kcol = lax.broadcasted_iota(jnp.int32, (SEL_TOPK, 1), 0).astype(F32)
    for g in range(NSA_KV_HEADS):
        ci = cols[:, g:g + 1]
        rj = scores[g]
        beats = jnp.where(ci > rj, 1.0, jnp.where(ci == rj, earlier, 0.0))
        rank = jnp.sum(beats, axis=0, keepdims=True)
        onehot = rank == kcol
        idx = jnp.sum(jnp.where(onehot, jrow.astype(F32), 0.0), axis=-1, keepdims=True)
        idx_ref[0, g] = idx.astype(jnp.int32)


def _attn_sample1(q_s, kc_s, vc_s, st_win_rows, kvw_new, past_len):
    dbsz = q_s.shape[0]
    n_cmp = kc_s.shape[2]
    n_sel = -(-(past_len + 1) // SEL_BLOCK)
    n_sel_pad = -(-n_sel // LANE) * LANE
    assert n_sel >= SEL_TOPK
    selmap = _sel_map(n_cmp, n_sel, n_sel_pad)
    wrows = st_win_rows.shape[1]
    hshape = jax.ShapeDtypeStruct((dbsz, NSA_KV_HEADS, NSA_GROUP, HEAD_DIM), F32)
    hspec = pl.BlockSpec((1, NSA_KV_HEADS, NSA_GROUP, HEAD_DIM), lambda b: (b, 0, 0, 0))
    cspec = pl.BlockSpec((1, NSA_KV_HEADS, n_cmp, HEAD_DIM), lambda b: (b, 0, 0, 0))
    return pl.pallas_call(
        functools.partial(_attn_sample1_kernel, past_len=past_len, n_sel=n_sel, n_sel_pad=n_sel_pad),
        grid=(dbsz,),
        in_specs=[hspec, cspec, cspec,
                  pl.BlockSpec((n_cmp, n_sel_pad), lambda b: (0, 0)),
                  pl.BlockSpec((1, wrows, HEAD_DIM), lambda b: (b, 0, 0)),
                  pl.BlockSpec((1, 2 * NSA_KV_HEADS, HEAD_DIM), lambda b: (b, 0, 0))],
        out_specs=[hspec, hspec,
                   pl.BlockSpec((1, NSA_KV_HEADS, SEL_TOPK, 1), lambda b: (b, 0, 0, 0))],
        out_shape=[hshape, hshape,
                   jax.ShapeDtypeStruct((dbsz, NSA_KV_HEADS, SEL_TOPK, 1), jnp.int32)],
        compiler_params=_cparams(("parallel",)),
        name="attn_sample_cmp_win",
    )(q_s, kc_s, vc_s, selmap, st_win_rows, kvw_new)


BLOCK_ROWS = SEL_BLOCK * KV_ROWS


def _attn_sample2_kernel(idx_ref, pt_ref, q_ref, cache_ref, kvn_ref, ocmp_ref, owin_ref, gate_ref, za_ref,
                         o_ref, buf_ref, sem_ref, *, past_len, n_pages):
    i = pl.program_id(0)
    n_i = pl.num_programs(0)
    slot = i % 2
    g = i % NSA_KV_HEADS
    qpos = past_len
    n_past_blocks = past_len // SEL_BLOCK
    per_page = PAGE_SIZE // SEL_BLOCK
    n_keys = SEL_TOPK * SEL_BLOCK

    def block_copy(step, k, to_slot):
        blk = jnp.minimum(idx_ref[step * SEL_TOPK + k], n_past_blocks - 1)
        phys = pt_ref[(step // NSA_KV_HEADS) * n_pages + blk // per_page] * per_page + blk % per_page
        return pltpu.make_async_copy(cache_ref.at[pl.ds(phys * BLOCK_ROWS, BLOCK_ROWS)],
                                     buf_ref.at[to_slot, pl.ds(k * BLOCK_ROWS, BLOCK_ROWS)], sem_ref.at[to_slot])

    @pl.when(i == 0)
    def _():
        for k in range(SEL_TOPK):
            block_copy(0, k, 0).start()

    @pl.when(i + 1 < n_i)
    def _():
        for k in range(SEL_TOPK):
            block_copy(i + 1, k, 1 - slot).start()

    for k in range(SEL_TOPK):
        block_copy(i, k, slot).wait()

    slope = _group_slopes(g)
    q = q_ref[0, 0].astype(BF16)
    kk = buf_ref[slot, pl.ds(g, n_keys, stride=KV_ROWS), :].astype(BF16)
    vv = buf_ref[slot, pl.ds(NSA_KV_HEADS + g, n_keys, stride=KV_ROWS), :].astype(BF16)
    lane_blk = lax.broadcasted_iota(jnp.int32, (1, n_keys), 1) // SEL_BLOCK
    tok = lax.broadcasted_iota(jnp.int32, (1, n_keys), 1) % SEL_BLOCK
    for k in range(SEL_TOPK):
        blk = idx_ref[i * SEL_TOPK + k]
        tok = tok + jnp.where(lane_blk == k, jnp.where(blk < n_past_blocks, blk * SEL_BLOCK, qpos + 1), 0)
    dd = qpos - tok
    mask = dd >= 0
    sc = jnp.where(mask, _dot_nt(q, kk) * SCALE - slope * dd.astype(F32), NEG)
    knew = kvn_ref[0, 0, 0].astype(BF16).astype(F32)
    vnew = kvn_ref[0, 1, 0].astype(BF16).astype(F32)
    s_new = jnp.sum(q.astype(F32) * knew, axis=-1, keepdims=True) * SCALE
    m = jnp.maximum(jnp.max(sc, axis=-1, keepdims=True), s_new)
    p = jnp.where(mask, jnp.exp(sc - m), 0.0)
    p_new = jnp.exp(s_new - m)
    den = jnp.sum(p, axis=-1, keepdims=True) + p_new
    o_slc = (_dot(p.astype(BF16), vv) + p_new.astype(BF16).astype(F32) * vnew) / den
    tot = (jax.nn.sigmoid(gate_ref[0, 0, 0]) * ocmp_ref[0, 0]
           + jax.nn.sigmoid(gate_ref[0, 1, 0]) * o_slc
           + jax.nn.sigmoid(gate_ref[0, 2, 0]) * owin_ref[0, 0])
    o_ref[0, 0] = tot * _silu(za_ref[0, 0])


def _attn_sample2(idx, page_table, q_s, cache_rows, kvs_new, o_cmp, o_win, gates, za_s, past_len):
    dbsz, n_pages = page_table.shape
    ng = NSA_KV_HEADS
    head = pl.BlockSpec((1, 1, NSA_GROUP, HEAD_DIM), lambda i, x, p: (i // ng, i % ng, 0, 0))
    grid_spec = pltpu.PrefetchScalarGridSpec(
        num_scalar_prefetch=2,
        grid=(dbsz * ng,),
        in_specs=[head,
                  pl.BlockSpec(memory_space=pl.ANY),
                  pl.BlockSpec((1, 2, 1, 1, HEAD_DIM), lambda i, x, p: (i // ng, 0, i % ng, 0, 0)),
                  head, head,
                  pl.BlockSpec((1, 3, 1, NSA_GROUP, 1), lambda i, x, p: (i // ng, 0, i % ng, 0, 0)),
                  head],
        out_specs=head,
        scratch_shapes=[pltpu.VMEM((2, SEL_TOPK * BLOCK_ROWS, HEAD_DIM), F32),
                        pltpu.SemaphoreType.DMA((2,))],
    )
    return pl.pallas_call(
        functools.partial(_attn_sample2_kernel, past_len=past_len, n_pages=n_pages),
        grid_spec=grid_spec,
        out_shape=jax.ShapeDtypeStruct((dbsz, ng, NSA_GROUP, HEAD_DIM), F32),
        compiler_params=_cparams(("arbitrary",)),
        name="attn_sample_slc",
    )(idx.reshape(-1), page_table.reshape(-1), q_s, cache_rows, kvs_new, o_cmp, o_win, gates, za_s)


def _gdn_sample_kernel(x_ref, cst_ref, cw_ref, a_ref, b_ref, alog_ref, dtb_ref, s_ref, gnw_ref, zb_ref,
                       o_ref, s_out_ref):
    acc = None
    for i in range(CONV_W):
        row = cst_ref[0, i] if i < CONV_W - 1 else x_ref[0]
        term = row * cw_ref[i]
        acc = term if acc is None else acc + term
    c = _silu(acc)
    q = _l2(c[0:GDN_HEADS]) * (GDN_DK ** -0.5)
    k = _l2(c[GDN_HEADS:2 * GDN_HEADS])
    v = c[2 * GDN_HEADS:3 * GDN_HEADS]
    gdec = -jnp.exp(alog_ref[...]) * _softplus(a_ref[0] + dtb_ref[...])
    beta = jax.nn.sigmoid(b_ref[0])
    eg = jnp.exp(gdec)
    qk = jnp.sum(q * k, axis=-1, keepdims=True)
    pad = jnp.concatenate([q, k, jnp.zeros((LANE - 2 * GDN_HEADS, GDN_DK), F32)], axis=0)
    cols = pad.T
    outs = []
    for hh in range(GDN_HEADS):
        s_h = s_ref[0, hh]
        qcol = cols[:, hh:hh + 1]
        kcol = cols[:, GDN_HEADS + hh:GDN_HEADS + hh + 1]
        b_h = beta[hh:hh + 1]
        e_h = eg[hh:hh + 1]
        k_s = jnp.sum(kcol * s_h, axis=0, keepdims=True)
        q_s = jnp.sum(qcol * s_h, axis=0, keepdims=True)
        vn = b_h * v[hh:hh + 1] - (b_h * e_h) * k_s
        outs.append(e_h * q_s + qk[hh:hh + 1] * vn)
        s_out_ref[0, hh] = s_h * e_h + kcol * vn
    o = jnp.concatenate(outs, axis=0)
    o_ref[0] = _rms(o, gnw_ref[...]) * _silu(zb_ref[0])


def _gdn_sample(x_s, cst, cw, a_s, b_s, a_log, dt_bias, s0, gnw, zb_s):
    dbsz = x_s.shape[0]
    nh3 = 3 * GDN_HEADS
    col = pl.BlockSpec((1, GDN_HEADS, 1), lambda b: (b, 0, 0))
    par = pl.BlockSpec((GDN_HEADS, 1), lambda b: (0, 0))
    head = pl.BlockSpec((1, GDN_HEADS, GDN_DV), lambda b: (b, 0, 0))
    sspec = pl.BlockSpec((1, GDN_HEADS, GDN_DK, GDN_DV), lambda b: (b, 0, 0, 0))
    return pl.pallas_call(
        _gdn_sample_kernel,
        grid=(dbsz,),
        in_specs=[pl.BlockSpec((1, nh3, GDN_DK), lambda b: (b, 0, 0)),
                  pl.BlockSpec((1, CONV_W - 1, nh3, GDN_DK), lambda b: (b, 0, 0, 0)),
                  pl.BlockSpec((CONV_W, nh3, GDN_DK), lambda b: (0, 0, 0)),
                  col, col, par, par, sspec,
                  pl.BlockSpec((1, GDN_DV), lambda b: (0, 0)),
                  head],
        out_specs=[head, sspec],
        out_shape=[jax.ShapeDtypeStruct((dbsz, GDN_HEADS, GDN_DV), F32),
                   jax.ShapeDtypeStruct(s0.shape, F32)],
        compiler_params=_cparams(("parallel",)),
        name="gdn_sample",
    )(x_s, cst, cw, a_s, b_s, a_log.reshape(-1, 1), dt_bias.reshape(-1, 1), s0, gnw.reshape(1, -1), zb_s)


def _permute_w_in(w_in):
    splits = (NSA_WIDTH, 2 * NSA_KV_WIDTH, 2 * NSA_KV_WIDTH, 2 * NSA_KV_WIDTH, 3 * NSA_HEADS, NSA_WIDTH,
              GDN_CONV_CH, GDN_HEADS, GDN_HEADS, GDN_V_WIDTH, w_in.shape[0], w_in.shape[0])
    bounds = np.cumsum(np.array(splits))[:-1].tolist()
    (q_a, kvc, kvs, kvw, g_nsa, z_a, qkv_b, a_b, b_b, z_b, gate_a, gate_b) = jnp.split(w_in, bounds, axis=1)
    small_pad = jnp.zeros((w_in.shape[0], H_WIDTH - C_SM - SM_B - GDN_HEADS), w_in.dtype)
    return jnp.concatenate([q_a, z_a, z_b, qkv_b, gate_a, gate_b, kvc, kvs, kvw, g_nsa, a_b, b_b, small_pad],
                           axis=1).astype(BF16)


def _cmp_w1(w1):
    return jnp.transpose(w1, (1, 2, 0, 3)).reshape(CMP_STRIDE, HEAD_DIM, CMP_R * CMP_HIDDEN).astype(BF16)


def _layer(x_p, x_s, cache_cmp, cache_slc, layer, st_win, st_s, st_conv, page_table, w):
    (norm_w, w_in, q_norm_w, k_norm_cmp_w, k_norm_slc_w, k_norm_win_w,
     cmp_w1_k, cmp_b1_k, cmp_w2_k, cmp_w1_v, cmp_b1_v, cmp_w2_v,
     conv_w, a_log, dt_bias, gdn_norm_w, w_pa, w_pb, w_o) = w
    bsz, t, d = x_p.shape
    dbsz, dseq, _ = x_s.shape
    assert dseq == 1 and d == C_GB - C_GA
    n_pages = page_table.shape[1]
    past_len = n_pages * PAGE_SIZE

    w_perm = _permute_w_in(w_in)
    head_w = jnp.stack([q_norm_w, k_norm_slc_w, k_norm_win_w])
    w1k, w1v = _cmp_w1(cmp_w1_k), _cmp_w1(cmp_w1_v)
    w2k, w2v = cmp_w2_k.astype(BF16), cmp_w2_v.astype(BF16)
    wpa, wpb, wo = w_pa.astype(BF16), w_pb.astype(BF16), w_o.astype(BF16)
    kvw_cols = 2 * NSA_KV_WIDTH

    n = bsz * t
    x2 = x_p.reshape(n, d)
    h, kvc_rows, kvs_rows, kvw_rows = _proj(x2, norm_w, w_perm, head_w, tm=min(1024, n))
    pk, pv = _cmp_part_prompt(h, w1k, w1v, bsz, t)
    kc, vc = _cmp_finish(pk, pv, cmp_b1_k, w2k, cmp_b1_v, w2v, k_norm_cmp_w)
    oa = _attn_prompt(h, kc, vc, bsz, t)
    nc = t // GDN_CHUNK
    ab5 = h[:, C_SM + SM_A:C_SM + SM_B + GDN_HEADS].reshape(bsz * nc, GDN_CHUNK, 2, GDN_PAIRS, GDN_PAIR)
    col_form = lambda x: x.transpose(0, 3, 1, 2).reshape(bsz * nc, GDN_PROWS, GDN_PAIRS)
    a_row = ab5[:, :, 0].transpose(0, 2, 3, 1).reshape(bsz * nc, GDN_PAIRS, GDN_PROWS)
    u, wmat, att, qd, kd, edl = _gdn_intra(h, conv_w, col_form(ab5[:, :, 0]), col_form(ab5[:, :, 1]), a_row,
                                           a_log, dt_bias, bsz, t)
    ob, s_p = _gdn_inter(u, wmat, att, qd, kd, edl, h, gdn_norm_w, bsz, t)
    y_p = _out_proj(oa, ob, h, x2, wpa, wpb, wo, tm=min(256, n)).reshape(bsz, t, d)

    kv_shape = (bsz, t, 2, NSA_KV_HEADS, HEAD_DIM)
    cmp_p = kvc_rows.reshape(kv_shape)
    slc_p = kvs_rows.reshape(kv_shape)
    wlen = min(WINDOW, t)
    win_p = kvw_rows.reshape(kv_shape)[:, t - wlen:]
    conv_p = h.reshape(bsz, t, H_WIDTH)[:, t - (CONV_W - 1):, C_QKVB:C_QKVB + GDN_CONV_CH]

    xs2 = x_s.reshape(dbsz, d)
    hs, kvc_s, kvs_s, kvw_s = _proj(xs2, norm_w, w_perm, head_w, tm=dbsz)
    grp = (dbsz, NSA_KV_HEADS, NSA_GROUP, HEAD_DIM)
    q_s = hs[:, C_Q:C_Q + NSA_WIDTH].reshape(grp)
    za_s = hs[:, C_ZA:C_ZA + NSA_WIDTH].reshape(grp)
    pt_phys = page_table + layer * cache_cmp.shape[1]
    pk_s, pv_s = _cmp_part_sample(cache_cmp.reshape(-1, HEAD_DIM), pt_phys, w1k, w1v)
    kc_s, vc_s = _cmp_finish(pk_s, pv_s, cmp_b1_k, w2k, cmp_b1_v, w2v, k_norm_cmp_w)
    wb = st_win.shape[1]
    o_cmp_s, o_win_s, idx = _attn_sample1(q_s, kc_s, vc_s, st_win.reshape(dbsz, wb * KV_ROWS, HEAD_DIM),
                                          kvw_s.reshape(dbsz, 2 * NSA_KV_HEADS, HEAD_DIM), past_len)
    gates_s = hs[:, C_SM:C_SM + 3 * NSA_HEADS].reshape(dbsz, 3, NSA_KV_HEADS, NSA_GROUP, 1)
    oa_s = _attn_sample2(idx, pt_phys, q_s, cache_slc.reshape(-1, HEAD_DIM),
                         kvs_s.reshape(dbsz, 2, NSA_KV_HEADS, 1, HEAD_DIM), o_cmp_s, o_win_s, gates_s, za_s,
                         past_len)
    nh3 = 3 * GDN_HEADS
    ob_s, s_s = _gdn_sample(hs[:, C_QKVB:C_QKVB + GDN_CONV_CH].reshape(dbsz, nh3, GDN_DK),
                            st_conv.reshape(dbsz, CONV_W - 1, nh3, GDN_DK),
                            conv_w.reshape(CONV_W, nh3, GDN_DK),
                            hs[:, C_SM + SM_A:C_SM + SM_A + GDN_HEADS].reshape(dbsz, GDN_HEADS, 1),
                            hs[:, C_SM + SM_B:C_SM + SM_B + GDN_HEADS].reshape(dbsz, GDN_HEADS, 1),
                            a_log, dt_bias, st_s, gdn_norm_w,
                            hs[:, C_ZB:C_ZB + GDN_V_WIDTH].reshape(dbsz, GDN_HEADS, GDN_DV))
    y_s = _out_proj(oa_s.reshape(dbsz, NSA_WIDTH), ob_s.reshape(dbsz, GDN_V_WIDTH), hs, xs2,
                    wpa, wpb, wo, tm=dbsz).reshape(dbsz, 1, d)

    kv1 = (dbsz, 1, 2, NSA_KV_HEADS, HEAD_DIM)
    win_s = jnp.concatenate([st_win, kvw_s.reshape(kv1)], axis=1)[:, -wb:]
    conv_s = jnp.concatenate([st_conv, hs[:, None, C_QKVB:C_QKVB + GDN_CONV_CH]], axis=1)[:, -(CONV_W - 1):]
    return y_p, y_s, (cmp_p, slc_p, win_p, s_p, conv_p,
                      kvc_s.reshape(kv1), kvs_s.reshape(kv1), win_s, s_s, conv_s)


def _take_layer(x, layer):
    return x.reshape(x.shape[1:]) if x.shape[0] == 1 else x[layer]


def kernel(x_prompt, x_sample, cache_cmp_kv, cache_slc_kv, state_win_kv, state_gdn_S, state_gdn_conv, page_table, norm_w, w_in, q_norm_w, k_norm_cmp_w, k_norm_slc_w, k_norm_win_w, cmp_w1_k, cmp_b1_k, cmp_w2_k, cmp_w1_v, cmp_b1_v, cmp_w2_v, conv_w, a_log, dt_bias, gdn_norm_w, w_pa, w_pb, w_o):
    weights = (norm_w, w_in, q_norm_w, k_norm_cmp_w, k_norm_slc_w, k_norm_win_w,
               cmp_w1_k, cmp_b1_k, cmp_w2_k, cmp_w1_v, cmp_b1_v, cmp_w2_v,
               conv_w, a_log, dt_bias, gdn_norm_w, w_pa, w_pb, w_o)
    depth = norm_w.shape[0]
    y_p, y_s = x_prompt, x_sample
    per_layer = []
    for layer in range(depth):
        y_p, y_s, st = _layer(y_p, y_s, cache_cmp_kv, cache_slc_kv, layer, _take_layer(state_win_kv, layer),
                              _take_layer(state_gdn_S, layer), _take_layer(state_gdn_conv, layer), page_table,
                              tuple(_take_layer(wt, layer) for wt in weights))
        per_layer.append(st)
    stacked = [jnp.stack(a) for a in zip(*per_layer)]
    return (y_p, y_s, *stacked)
```

```python
import functools

import numpy as np
import jax
import jax.numpy as jnp
from jax import lax
from jax.experimental import pallas as pl
from jax.experimental.pallas import tpu as pltpu

F32 = jnp.float32
BF16 = jnp.bfloat16
HI = lax.Precision.HIGHEST

PAGE_SIZE = 128
HEAD_DIM = 128
NSA_HEADS = 8
NSA_KV_HEADS = 2
NSA_GROUP = NSA_HEADS // NSA_KV_HEADS
NSA_WIDTH = NSA_HEADS * HEAD_DIM
NSA_KV_WIDTH = NSA_KV_HEADS * HEAD_DIM
CMP_LEN = 32
CMP_STRIDE = 16
CMP_R = CMP_LEN // CMP_STRIDE
CMP_HIDDEN = 128
SEL_BLOCK = 64
SEL_TOPK = 16
N_LOCAL_SEL = 2
WINDOW = 512
Q_BLOCK = 128
FORCE_SCORE = 1e4
ALIBI_MAX = 8.0
GDN_HEADS = 8
GDN_DK = 128
GDN_DV = 128
GDN_QK_WIDTH = GDN_HEADS * GDN_DK
GDN_V_WIDTH = GDN_HEADS * GDN_DV
GDN_CONV_CH = 2 * GDN_QK_WIDTH + GDN_V_WIDTH
CONV_W = 4
GDN_CHUNK = 64
EPS = 1e-6
SCALE = HEAD_DIM ** -0.5
NEG = -1e30

C_Q = 0
C_ZA = 1024
C_ZB = 2048
C_QKVB = 3072
C_GA = 6144
C_GB = 8192
C_KVC = 10240
C_KVS = 10752
C_KVW = 11264
C_SM = 11776
H_WIDTH = 12288
TN = 512
PROJ_TN = 1024
SM_A = 3 * NSA_HEADS
SM_B = SM_A + GDN_HEADS

LANE = 128
KV_ROWS = 2 * NSA_KV_HEADS
PAGE_ROWS = PAGE_SIZE * KV_ROWS
VMEM_LIMIT = 56 * 1024 * 1024


def _cparams(sem, vmem=VMEM_LIMIT):
    return pltpu.CompilerParams(dimension_semantics=sem, vmem_limit_bytes=vmem)


def _silu(x):
    return x * jax.nn.sigmoid(x)


def _rms(a, w):
    return a * lax.rsqrt(jnp.mean(a * a, axis=-1, keepdims=True) + EPS) * w


def _l2(a):
    return a * lax.rsqrt(jnp.sum(a * a, axis=-1, keepdims=True) + EPS)


def _dot(a, b):
    return jnp.dot(a, b, preferred_element_type=F32)


def _dot_nt(a, b):
    return lax.dot_general(a, b, (((1,), (1,)), ((), ())), preferred_element_type=F32)


def _dot_hi(a, b):
    return jnp.dot(a, b, preferred_element_type=F32, precision=HI)


def _softplus(x):
    return jnp.maximum(x, 0.0) + jnp.log1p(jnp.exp(-jnp.abs(x)))


def _masked_softmax(s, mask):
    sm = jnp.where(mask, s, NEG)
    m = jnp.max(sm, axis=-1, keepdims=True)
    e = jnp.where(mask, jnp.exp(sm - m), 0.0)
    den = jnp.sum(e, axis=-1, keepdims=True)
    return e / jnp.where(den > 0, den, 1.0)


def _proj_kernel(x_ref, nw_ref, w_ref, hw_ref, o_ref, kvc_ref, kvs_ref, kvw_ref, xn_ref):
    j = pl.program_id(1)
    tm = x_ref.shape[0]
    kv_refs = (kvc_ref, kvs_ref, kvw_ref)
    kv_dest = _proj_kv_dest()

    @pl.when(j == 0)
    def _():
        x = x_ref[...]
        y = x * lax.rsqrt(jnp.mean(x * x, axis=-1, keepdims=True) + EPS)
        xn_ref[...] = (y * nw_ref[...]).astype(BF16)

    acc = _dot(xn_ref[...], w_ref[...])

    plan = _proj_norm_plan()
    for tile, groups in plan.items():
        @pl.when(j == tile)
        def _(groups=groups):
            parts = []
            for grp in range(PROJ_TN // LANE):
                a = acc[:, grp * LANE:(grp + 1) * LANE]
                parts.append(_rms(a, hw_ref[groups[grp]:groups[grp] + 1, :]) if grp in groups else a)
            o_ref[...] = jnp.concatenate(parts, axis=1)
            for grp, (seg, c) in kv_dest.get(tile, {}).items():
                kv_refs[seg][pl.ds(c, tm, stride=KV_ROWS), :] = parts[grp]

    assert set(kv_dest) <= set(plan)
    plain = functools.reduce(jnp.logical_and, [j != tile for tile in plan])

    @pl.when(plain)
    def _():
        o_ref[...] = acc


def _proj_kv_dest():
    dest = {}
    for seg, c0 in enumerate((C_KVC, C_KVS, C_KVW)):
        for c in range(KV_ROWS):
            col = c0 + c * LANE
            dest.setdefault(col // PROJ_TN, {})[(col % PROJ_TN) // LANE] = (seg, c)
    return dest


def _proj_norm_plan():
    plan = {}
    for c0, width, row in ((C_Q, NSA_WIDTH, 0), (C_KVS, NSA_KV_WIDTH, 1), (C_KVW, NSA_KV_WIDTH, 2)):
        for grp in range(width // LANE):
            col = c0 + grp * LANE
            plan.setdefault(col // PROJ_TN, {})[(col % PROJ_TN) // LANE] = row
    return plan


def _proj(x2d, norm_w, w_perm, head_w, tm):
    n, d = x2d.shape
    return pl.pallas_call(
        _proj_kernel,
        grid=(n // tm, H_WIDTH // PROJ_TN),
        in_specs=[pl.BlockSpec((tm, d), lambda i, j: (i, 0)),
                  pl.BlockSpec((1, d), lambda i, j: (0, 0)),
                  pl.BlockSpec((d, PROJ_TN), lambda i, j: (0, j)),
                  pl.BlockSpec((3, LANE), lambda i, j: (0, 0))],
        out_specs=[pl.BlockSpec((tm, PROJ_TN), lambda i, j: (i, j))]
        + [pl.BlockSpec((tm * KV_ROWS, LANE), lambda i, j: (i, 0), pipeline_mode=pl.Buffered(1))] * 3,
        out_shape=[jax.ShapeDtypeStruct((n, H_WIDTH), F32)]
        + [jax.ShapeDtypeStruct((n * KV_ROWS, LANE), F32)] * 3,
        scratch_shapes=[pltpu.VMEM((tm, d), BF16)],
        compiler_params=_cparams(("parallel", "arbitrary")),
        name="in_proj",
    )(x2d, norm_w.reshape(1, d), w_perm, head_w)


def _cmp_partials_body(strided_rows, w1k_ref, w1v_ref, pk_ref, pv_ref, n_chunks):
    for g in range(NSA_KV_HEADS):
        for kv, (w1_ref, p_ref) in enumerate(((w1k_ref, pk_ref), (w1v_ref, pv_ref))):
            acc = jnp.zeros((n_chunks, CMP_R * CMP_HIDDEN), F32)
            for s in range(CMP_STRIDE):
                xs = strided_rows(kv * NSA_KV_HEADS + g, s)
                acc = acc + _dot(xs.astype(BF16), w1_ref[s])
            p_ref[0, g] = acc


def _cmp_part_prompt_kernel(r0_ref, r1_ref, r2_ref, r3_ref, w1k_ref, w1v_ref, pk_ref, pv_ref, *, n_chunks):
    refs = (r0_ref, r1_ref, r2_ref, r3_ref)

    def strided_rows(c, s):
        return refs[c][pl.ds(s, n_chunks, stride=CMP_STRIDE), :]

    _cmp_partials_body(strided_rows, w1k_ref, w1v_ref, pk_ref, pv_ref, n_chunks)


def _cmp_part_prompt(h, w1k, w1v, bsz, t):
    n_chunks = t // CMP_STRIDE
    pshape = jax.ShapeDtypeStruct((bsz, NSA_KV_HEADS, n_chunks, CMP_R * CMP_HIDDEN), F32)
    pspec = pl.BlockSpec((1, NSA_KV_HEADS, n_chunks, CMP_R * CMP_HIDDEN), lambda b: (b, 0, 0, 0))
    wspec = pl.BlockSpec((CMP_STRIDE, HEAD_DIM, CMP_R * CMP_HIDDEN), lambda b: (0, 0, 0))
    return pl.pallas_call(
        functools.partial(_cmp_part_prompt_kernel, n_chunks=n_chunks),
        grid=(bsz,),
        in_specs=[pl.BlockSpec((t, HEAD_DIM), functools.partial(lambda c, b: (b, C_KVC // HEAD_DIM + c), c))
                  for c in range(2 * NSA_KV_HEADS)] + [wspec, wspec],
        out_specs=[pspec, pspec],
        out_shape=[pshape, pshape],
        compiler_params=_cparams(("parallel",)),
        name="cmp_part_prompt",
    )(h, h, h, h, w1k, w1v)


def _cmp_part_sample_kernel(pt_ref, cache_ref, w1k_ref, w1v_ref, pk_ref, pv_ref, buf_ref, sem_ref,
                            *, pages_per_step):
    i = pl.program_id(0)
    n_i = pl.num_programs(0)
    slot = i % 2
    n_chunks = pages_per_step * PAGE_SIZE // CMP_STRIDE
    chunk_rows = CMP_STRIDE * KV_ROWS

    def page_copy(step, p, to_slot):
        page = pt_ref[step * pages_per_step + p]
        return pltpu.make_async_copy(cache_ref.at[pl.ds(page * PAGE_ROWS, PAGE_ROWS)],
                                     buf_ref.at[to_slot, pl.ds(p * PAGE_ROWS, PAGE_ROWS)], sem_ref.at[to_slot])

    @pl.when(i == 0)
    def _():
        for p in range(pages_per_step):
            page_copy(0, p, 0).start()

    @pl.when(i + 1 < n_i)
    def _():
        for p in range(pages_per_step):
            page_copy(i + 1, p, 1 - slot).start()

    for p in range(pages_per_step):
        page_copy(i, p, slot).wait()

    for kv, (w1_ref, p_ref) in enumerate(((w1k_ref, pk_ref), (w1v_ref, pv_ref))):
        acc = jnp.zeros((NSA_KV_HEADS * n_chunks, CMP_R * CMP_HIDDEN), F32)
        for s in range(CMP_STRIDE):
            xs = jnp.concatenate(
                [buf_ref[slot, pl.ds(s * KV_ROWS + kv * NSA_KV_HEADS + g, n_chunks, stride=chunk_rows), :]
                 for g in range(NSA_KV_HEADS)], axis=0)
            acc = acc + _dot(xs.astype(BF16), w1_ref[s])
        for g in range(NSA_KV_HEADS):
            p_ref[0, g] = acc[g * n_chunks:(g + 1) * n_chunks]


def _cmp_part_sample(cache_rows, page_table, w1k, w1v):
    dbsz, n_pages = page_table.shape
    pages_per_step = min(32, n_pages)
    n_steps = n_pages // pages_per_step
    cps = pages_per_step * PAGE_SIZE // CMP_STRIDE
    n_chunks = n_pages * PAGE_SIZE // CMP_STRIDE
    pshape = jax.ShapeDtypeStruct((dbsz, NSA_KV_HEADS, n_chunks, CMP_R * CMP_HIDDEN), F32)
    pspec = pl.BlockSpec((1, NSA_KV_HEADS, cps, CMP_R * CMP_HIDDEN),
                         lambda i, pt: (i // n_steps, 0, i % n_steps, 0))
    wspec = pl.BlockSpec((CMP_STRIDE, HEAD_DIM, CMP_R * CMP_HIDDEN), lambda i, pt: (0, 0, 0))
    grid_spec = pltpu.PrefetchScalarGridSpec(
        num_scalar_prefetch=1,
        grid=(dbsz * n_steps,),
        in_specs=[pl.BlockSpec(memory_space=pl.ANY), wspec, wspec],
        out_specs=[pspec, pspec],
        scratch_shapes=[pltpu.VMEM((2, pages_per_step * PAGE_ROWS, HEAD_DIM), F32),
                        pltpu.SemaphoreType.DMA((2,))],
    )
    return pl.pallas_call(
        functools.partial(_cmp_part_sample_kernel, pages_per_step=pages_per_step),
        grid_spec=grid_spec,
        out_shape=[pshape, pshape],
        compiler_params=_cparams(("arbitrary",)),
        name="cmp_part_sample",
    )(page_table.reshape(-1), cache_rows, w1k, w1v)


def _cmp_finish_kernel(pk_ref, pv_ref, b1k_ref, w2k_ref, b1v_ref, w2v_ref, nk_ref, kc_ref, vc_ref,
                       *, n_chunks):
    for g in range(NSA_KV_HEADS):
        for kv, (p_ref, b1_ref, w2_ref, o_ref) in enumerate(((pk_ref, b1k_ref, w2k_ref, kc_ref),
                                                            (pv_ref, b1v_ref, w2v_ref, vc_ref))):
            part = p_ref[0, g]
            nxt = pltpu.roll(part[:, CMP_HIDDEN:], n_chunks - 1, 0)
            hid = _silu(part[:, :CMP_HIDDEN] + nxt + b1_ref[...])
            out = _dot(hid.astype(BF16), w2_ref[...])
            if kv == 0:
                out = _rms(out, nk_ref[...])
            o_ref[0, g] = out


def _cmp_finish(pk, pv, b1k, w2k, b1v, w2v, nk):
    bsz, _, n_chunks, _ = pk.shape
    pspec = pl.BlockSpec((1, NSA_KV_HEADS, n_chunks, CMP_R * CMP_HIDDEN), lambda b: (b, 0, 0, 0))
    vspec = pl.BlockSpec((1, CMP_HIDDEN), lambda b: (0, 0))
    wspec = pl.BlockSpec((CMP_HIDDEN, HEAD_DIM), lambda b: (0, 0))
    oshape = jax.ShapeDtypeStruct((bsz, NSA_KV_HEADS, n_chunks, HEAD_DIM), F32)
    ospec = pl.BlockSpec((1, NSA_KV_HEADS, n_chunks, HEAD_DIM), lambda b: (b, 0, 0, 0))
    return pl.pallas_call(
        functools.partial(_cmp_finish_kernel, n_chunks=n_chunks),
        grid=(bsz,),
        in_specs=[pspec, pspec, vspec, wspec, vspec, wspec, vspec],
        out_specs=[ospec, ospec],
        out_shape=[oshape, oshape],
        compiler_params=_cparams(("parallel",)),
        name="cmp_finish",
    )(pk, pv, b1k.reshape(1, -1), w2k, b1v.reshape(1, -1), w2v, nk.reshape(1, -1))


def _sel_map(n_cmp, n_sel, n_sel_pad):
    cs = np.arange(n_cmp) * CMP_STRIDE
    ss = np.arange(n_sel) * SEL_BLOCK
    ov = np.minimum(cs[:, None] + CMP_LEN, ss[None] + SEL_BLOCK) - np.maximum(cs[:, None], ss[None])
    m = np.zeros((n_cmp, n_sel_pad), np.float32)
    m[:, :n_sel] = np.clip(ov, 0, None).astype(np.float32) / CMP_LEN
    return jnp.asarray(m, dtype=BF16)


def _topk_rank_rows(score, n_valid):
    j = lax.broadcasted_iota(jnp.int32, score.shape, 1)
    cnt = jnp.zeros(score.shape, F32)
    for jp in range(n_valid):
        col = score[:, jp:jp + 1]
        earlier = (j > jp).astype(F32)
        cnt = cnt + jnp.where(col > score, 1.0, jnp.where(col == score, earlier, 0.0))
    return cnt


QT = 256
MASK_BIG = 16384.0


def _attn_prompt_kernel(q_ref, ks_ref, vs_ref, kw_ref, vw_ref, kc_ref, vc_ref, sm_ref, za_ref,
                        selmap_ref, kaug_ref, caug_ref, padaug_ref, band_ref, o_ref,
                        ksa_ref, vsb_ref, kwa_ref, vwb_ref, kca_ref, vcb_ref, *, t_len):
    g = pl.program_id(1)
    qb = pl.program_id(2)
    qstart = pl.multiple_of(qb * QT, QT)
    n_sel = t_len // SEL_BLOCK
    rows = NSA_GROUP * QT

    @pl.when(qb == 0)
    def _():
        ksa_ref[:, 0:HEAD_DIM] = ks_ref[...].astype(BF16)
        ksa_ref[:, HEAD_DIM:] = kaug_ref[...]
        vsb_ref[...] = vs_ref[...].astype(BF16)
        kwa_ref[0:WINDOW, 0:HEAD_DIM] = jnp.zeros((WINDOW, HEAD_DIM), BF16)
        kwa_ref[0:WINDOW, HEAD_DIM:] = padaug_ref[...]
        kwa_ref[WINDOW:, 0:HEAD_DIM] = kw_ref[...].astype(BF16)
        kwa_ref[WINDOW:, HEAD_DIM:] = kaug_ref[...]
        vwb_ref[0:WINDOW, :] = jnp.zeros((WINDOW, HEAD_DIM), BF16)
        vwb_ref[WINDOW:, :] = vw_ref[...].astype(BF16)
        kca_ref[:, 0:HEAD_DIM] = kc_ref[0, 0].astype(BF16)
        kca_ref[:, HEAD_DIM:] = caug_ref[...]
        vcb_ref[...] = vc_ref[0, 0].astype(BF16)

    q = q_ref[...]
    qsb = (jnp.concatenate([q[:, r * HEAD_DIM:(r + 1) * HEAD_DIM] for r in range(NSA_GROUP)], axis=0)
           * SCALE).astype(BF16)
    row = lax.broadcasted_iota(jnp.int32, (rows, 1), 0)
    r_of = row // QT
    tloc = row % QT
    qpos = qstart + tloc
    slope = jnp.zeros((rows, 1), F32)
    for r in range(NSA_GROUP):
        s_r = jnp.where(g == 0, 2.0 ** -(r + 1.0), 2.0 ** -(r + 1.0 + NSA_GROUP)).astype(F32)
        slope = jnp.where(r_of == r, s_r, slope)
    lane = lax.broadcasted_iota(jnp.int32, (rows, LANE), 1)
    pos_cols = jnp.where(lane == n_sel, slope * 256.0, jnp.where(lane == n_sel + 1, slope, 0.0))
    q_w = jnp.concatenate([qsb, jnp.where(lane == n_sel + 2, -MASK_BIG, pos_cols).astype(BF16)], axis=1)

    n_cmp = kca_ref.shape[0]
    kend = lax.broadcasted_iota(jnp.int32, (1, n_cmp), 1) * CMP_STRIDE + (CMP_LEN - 1)
    p_cmp = _masked_softmax(_dot_nt(q_w, kca_ref[...]), (qpos - kend) >= 0).astype(BF16)
    o_cmp = _dot(p_cmp, vcb_ref[...])
    imp4 = _dot(p_cmp, selmap_ref[...])
    imp = imp4[0:QT]
    for r in range(1, NSA_GROUP):
        imp = imp + imp4[r * QT:(r + 1) * QT]

    n_sel_rows = -(-n_sel // SUBLANES) * SUBLANES
    imp_t = imp.T[0:n_sel_rows]
    tq = qstart + lax.broadcasted_iota(jnp.int32, (n_sel_rows, QT), 1)
    j = lax.broadcasted_iota(jnp.int32, (n_sel_rows, QT), 0)
    visible = (j * SEL_BLOCK <= tq) & (j < n_sel)
    back = tq // SEL_BLOCK - j
    forced = (j == 0) | ((back >= 0) & (back < N_LOCAL_SEL))
    score = jnp.where(visible, jnp.where(forced, FORCE_SCORE, imp_t), -FORCE_SCORE)
    score = jnp.where(j < n_sel, score, -3e38)
    rank = jnp.zeros((n_sel_rows, QT), F32)
    for jp in range(n_sel):
        other = score[jp:jp + 1, :]
        rank = rank + jnp.where(other > score, 1.0, jnp.where(other == score, (j > jp).astype(F32), 0.0))
    sel_t = jnp.where((rank < min(SEL_TOPK, n_sel)) & visible, 1.0, 0.0)
    sel = jnp.concatenate([sel_t, jnp.zeros((LANE - n_sel_rows, QT), F32)], axis=0).T
    sel4 = jnp.concatenate([sel] * NSA_GROUP, axis=0)
    q_s = jnp.concatenate([qsb, jnp.where(lane < n_sel, (sel4 - 1.0) * MASK_BIG, pos_cols).astype(BF16)],
                          axis=1)

    def online(sc, k0, carry):
        m, l, acc = carry
        m_new = jnp.maximum(m, jnp.max(sc, axis=-1, keepdims=True))
        alpha = jnp.exp(m - m_new)
        p = jnp.exp(sc - m_new)
        l = alpha * l + jnp.sum(p, axis=-1, keepdims=True)
        acc = alpha * acc + _dot(p.astype(BF16), vsb_ref[pl.ds(k0, QT), :])
        return m_new, l, acc

    def past_chunk(c, carry):
        k0 = pl.multiple_of(c * QT, QT)
        return online(_dot_nt(q_s, ksa_ref[pl.ds(k0, QT), :]), k0, carry)

    init = (jnp.full((rows, 1), NEG, F32), jnp.zeros((rows, 1), F32), jnp.zeros((rows, HEAD_DIM), F32))
    carry = lax.fori_loop(0, qb, past_chunk, init)
    ki = lax.broadcasted_iota(jnp.int32, (1, QT), 1)
    diag = jnp.where(tloc >= ki, _dot_nt(q_s, ksa_ref[pl.ds(qstart, QT), :]), -MASK_BIG)
    _, l_s, acc_s = online(diag, qstart, carry)
    o_slc = acc_s / l_s

    span = QT + WINDOW
    band = band_ref[...]
    sw = _dot_nt(q_w, kwa_ref[pl.ds(qstart, span), :]) + jnp.concatenate([band] * NSA_GROUP, axis=0)
    e_win = jnp.exp(sw - jnp.max(sw, axis=-1, keepdims=True))
    o_win = (_dot(e_win.astype(BF16), vwb_ref[pl.ds(qstart, span), :])
             / jnp.sum(e_win, axis=-1, keepdims=True))

    sm = sm_ref[...]
    outs = []
    for r in range(NSA_GROUP):
        tot = jnp.zeros((QT, HEAD_DIM), F32)
        for br, o_br in enumerate((o_cmp, o_slc, o_win)):
            c0 = br * NSA_HEADS + r
            gate = jnp.where(g == 0, sm[:, c0:c0 + 1], sm[:, c0 + NSA_GROUP:c0 + NSA_GROUP + 1])
            tot = tot + jax.nn.sigmoid(gate) * o_br[r * QT:(r + 1) * QT]
        outs.append(tot * _silu(za_ref[:, r * HEAD_DIM:(r + 1) * HEAD_DIM]))
    o_ref[...] = jnp.concatenate(outs, axis=1)


def _attn_prompt(h, kc, vc, bsz, t):
    nqb = t // QT
    n_cmp = t // CMP_STRIDE
    n_sel = t // SEL_BLOCK
    span = QT + WINDOW
    assert n_sel + 3 <= LANE and t % QT == 0 and t + CMP_LEN < 256 * 256
    selmap = _sel_map(n_cmp, n_sel, LANE)

    def aug_cols(pos, onehot_block):
        a = np.zeros((pos.shape[0], LANE), np.float32)
        if onehot_block:
            a[np.arange(pos.shape[0]), pos // SEL_BLOCK] = 1.0
        a[:, n_sel] = pos // 256
        a[:, n_sel + 1] = pos % 256
        return jnp.asarray(a, dtype=BF16)

    kaug = aug_cols(np.arange(t), True)
    caug = aug_cols(np.arange(n_cmp) * CMP_STRIDE + CMP_LEN - 1, False)
    pad = np.zeros((WINDOW, LANE), np.float32)
    pad[:, n_sel + 2] = 1.0
    padaug = jnp.asarray(pad, dtype=BF16)
    ti = np.arange(QT)[:, None]
    ii = np.arange(span)[None]
    band = jnp.asarray(np.where((ii >= ti) & (ii <= ti + WINDOW), 0.0, -MASK_BIG).astype(np.float32))
    gw = NSA_GROUP * HEAD_DIM
    row_blk = lambda b, g, i: b * nqb + i
    kv_spec = lambda c0: pl.BlockSpec((t, HEAD_DIM), lambda b, g, i: (b, c0 // HEAD_DIM + g))
    const = lambda shape: pl.BlockSpec(shape, lambda b, g, i: (0,) * len(shape))
    return pl.pallas_call(
        functools.partial(_attn_prompt_kernel, t_len=t),
        grid=(bsz, NSA_KV_HEADS, nqb),
        in_specs=[pl.BlockSpec((QT, gw), lambda b, g, i: (row_blk(b, g, i), C_Q // gw + g)),
                  kv_spec(C_KVS), kv_spec(C_KVS + NSA_KV_WIDTH),
                  kv_spec(C_KVW), kv_spec(C_KVW + NSA_KV_WIDTH),
                  pl.BlockSpec((1, 1, n_cmp, HEAD_DIM), lambda b, g, i: (b, g, 0, 0)),
                  pl.BlockSpec((1, 1, n_cmp, HEAD_DIM), lambda b, g, i: (b, g, 0, 0)),
                  pl.BlockSpec((QT, TN), lambda b, g, i: (row_blk(b, g, i), C_SM // TN)),
                  pl.BlockSpec((QT, gw), lambda b, g, i: (row_blk(b, g, i), C_ZA // gw + g)),
                  const((n_cmp, LANE)), const((t, LANE)), const((n_cmp, LANE)), const((WINDOW, LANE)),
                  const((QT, span))],
        out_specs=pl.BlockSpec((QT, gw), lambda b, g, i: (row_blk(b, g, i), g)),
        out_shape=jax.ShapeDtypeStruct((bsz * t, NSA_WIDTH), F32),
        scratch_shapes=[pltpu.VMEM((t, 2 * HEAD_DIM), BF16), pltpu.VMEM((t, HEAD_DIM), BF16),
                        pltpu.VMEM((WINDOW + t, 2 * HEAD_DIM), BF16), pltpu.VMEM((WINDOW + t, HEAD_DIM), BF16),
                        pltpu.VMEM((n_cmp, 2 * HEAD_DIM), BF16), pltpu.VMEM((n_cmp, HEAD_DIM), BF16)],
        compiler_params=_cparams(("parallel", "parallel", "arbitrary")),
        name="attn_prompt",
    )(h, h, h, h, h, kc, vc, h, h, selmap, kaug, caug, padaug, band)


SUBLANES = 8
GDN_PAIR = 2
GDN_PAIRS = GDN_HEADS // GDN_PAIR
GDN_PROWS = GDN_PAIR * GDN_CHUNK


def _split2(x):
    hi = x.astype(BF16)
    return hi, (x - hi.astype(F32)).astype(BF16)


def _split3(x):
    hi = x.astype(BF16)
    r = x - hi.astype(F32)
    mid = r.astype(BF16)
    return hi, mid, (r - mid.astype(F32)).astype(BF16)


def _dot_x3(a_parts, b_parts):
    a_hi, a_lo = a_parts
    b_hi, b_lo = b_parts
    return _dot(jnp.concatenate([a_hi, a_lo, a_hi], axis=1), jnp.concatenate([b_hi, b_hi, b_lo], axis=0))


def _gdn_intra_kernel(x_ref, prev_ref, cw_ref, acol_ref, bcol_ref, arow_ref, alogc_ref, dtbc_ref, alogr_ref,
                      dtbr_ref, u_ref, w_ref, att_ref, qd_ref, kd_ref, edl_ref, *, chunks_per_seq):
    pr = GDN_PROWS
    first = pl.program_id(0) % chunks_per_seq == 0
    full = jnp.concatenate([jnp.where(first, 0.0, prev_ref[...]), x_ref[...]], axis=0)
    conv = None
    for i in range(CONV_W):
        dly = CONV_W - 1 - i
        xs = full if dly == 0 else pltpu.roll(full, dly, 0)
        term = xs[SUBLANES:] * cw_ref[i:i + 1, :]
        conv = term if conv is None else conv + term
    conv = _silu(conv)

    def head_cols(kind, hh):
        c0 = kind * GDN_QK_WIDTH + hh * GDN_DK
        return conv[:, c0:c0 + GDN_DK]

    def stack(kind, p):
        parts = [head_cols(kind, GDN_PAIR * p + i) for i in range(GDN_PAIR)]
        if kind == 0:
            parts = [_l2(x) * (GDN_DK ** -0.5) for x in parts]
        elif kind == 1:
            parts = [_l2(x) for x in parts]
        return jnp.concatenate(parts, axis=0)

    ri = lax.broadcasted_iota(jnp.int32, (pr, pr), 0)
    ci = lax.broadcasted_iota(jnp.int32, (pr, pr), 1)
    same = (ri // GDN_CHUNK) == (ci // GDN_CHUNK)
    tril = same & (ri >= ci)
    stril = same & (ri > ci)
    eye = (ri == ci).astype(F32)
    lo_b = tril.astype(BF16)
    ut_b = (same & (ri <= ci)).astype(BF16)
    rcol = lax.broadcasted_iota(jnp.int32, (pr, 1), 0)

    gcol = -jnp.exp(alogc_ref[...]) * _softplus(acol_ref[0] + dtbc_ref[...])
    beta = jax.nn.sigmoid(bcol_ref[0])
    grow = -jnp.exp(alogr_ref[...]) * _softplus(arow_ref[0] + dtbr_ref[...])
    dec_col = sum(_dot(lo_b, part) for part in _split3(gcol))
    dec_row = sum(_dot(part, ut_b) for part in _split3(grow))

    ks, kbs, kbfs, lmats, vbs, qs, dcs = [], [], [], [], [], [], []
    for p in range(GDN_PAIRS):
        k2 = stack(1, p)
        dc = dec_col[:, p:p + 1]
        b2 = beta[:, p:p + 1]
        dmat = jnp.where(tril, dc - dec_row[p:p + 1, :], 0.0)
        ks.append(k2)
        kbs.append(k2 * b2)
        kbfs.append(k2.astype(BF16))
        lmats.append(jnp.where(tril, jnp.exp(dmat), 0.0))
        vbs.append(stack(2, p) * b2)
        qs.append(stack(0, p))
        dcs.append(dc)

    npows = [-jnp.where(stril, _dot_nt(kbs[p].astype(BF16), kbfs[p]) * lmats[p], 0.0) for p in range(GDN_PAIRS)]
    tms = [eye + n for n in npows]
    parts = [_split2(n) for n in npows]
    npows = [_dot_x3(parts[p], parts[p]) for p in range(GDN_PAIRS)]
    for _ in range(4):
        for p in range(GDN_PAIRS):
            n_hi, n_lo = _split2(npows[p])
            s_hi, s_lo = _split2(tms[p])
            both = _dot_x3((n_hi, n_lo), (jnp.concatenate([n_hi, s_hi], axis=1),
                                          jnp.concatenate([n_lo, s_lo], axis=1)))
            npows[p] = both[:, :pr]
            tms[p] = tms[p] + both[:, pr:]
    tms = [tms[p] + _dot_x3(_split2(npows[p]), _split2(tms[p])) for p in range(GDN_PAIRS)]

    edl_rows = []
    for p in range(GDN_PAIRS):
        tm_b = tms[p].astype(BF16)
        dc = dcs[p]
        edec = jnp.exp(dc)
        edl_rows += [jnp.broadcast_to(jnp.exp(dc[(i + 1) * GDN_CHUNK - 1:(i + 1) * GDN_CHUNK, :]), (1, LANE))
                     for i in range(GDN_PAIR)]
        dl = jnp.where(rcol < GDN_CHUNK, dc[GDN_CHUNK - 1:GDN_CHUNK, :], dc[pr - 1:pr, :])
        uw = _dot(tm_b, jnp.concatenate([vbs[p].astype(BF16), (kbs[p] * edec).astype(BF16)], axis=1))
        u_ref[0, p] = uw[:, :GDN_DV]
        w_ref[0, p] = uw[:, GDN_DV:].astype(BF16)
        att_ref[0, p] = jnp.where(tril, _dot_nt(qs[p].astype(BF16), kbfs[p]) * lmats[p], 0.0).astype(BF16)
        qd_ref[0, p] = (qs[p] * edec).astype(BF16)
        kd_ref[0, p] = (ks[p] * jnp.exp(dl - dc)).astype(BF16)
    edl_ref[0] = jnp.concatenate(edl_rows, axis=0)


def _gdn_intra(h, conv_w, a_col, b_col, a_row, a_log, dt_bias, bsz, t):
    nc = t // GDN_CHUNK
    n_chunks = bsz * nc
    pr = GDN_PROWS
    per_chunk = GDN_CHUNK // SUBLANES
    alog_c = jnp.tile(a_log.reshape(GDN_PAIRS, GDN_PAIR).T[:, None, :], (1, GDN_CHUNK, 1)).reshape(pr, GDN_PAIRS)
    dtb_c = jnp.tile(dt_bias.reshape(GDN_PAIRS, GDN_PAIR).T[:, None, :], (1, GDN_CHUNK, 1)).reshape(pr, GDN_PAIRS)
    alog_r = jnp.repeat(a_log.reshape(GDN_PAIRS, GDN_PAIR), GDN_CHUNK, axis=1)
    dtb_r = jnp.repeat(dt_bias.reshape(GDN_PAIRS, GDN_PAIR), GDN_CHUNK, axis=1)
    cspec = pl.BlockSpec((1, pr, GDN_PAIRS), lambda i: (i, 0, 0))
    cpar = pl.BlockSpec((pr, GDN_PAIRS), lambda i: (0, 0))
    rpar = pl.BlockSpec((GDN_PAIRS, pr), lambda i: (0, 0))
    mspec = pl.BlockSpec((1, GDN_PAIRS, pr, GDN_DK), lambda i: (i, 0, 0, 0))
    mshape = lambda dt: jax.ShapeDtypeStruct((n_chunks, GDN_PAIRS, pr, GDN_DK), dt)
    qkv_blk = C_QKVB // GDN_CONV_CH
    return pl.pallas_call(
        functools.partial(_gdn_intra_kernel, chunks_per_seq=nc),
        grid=(n_chunks,),
        in_specs=[pl.BlockSpec((GDN_CHUNK, GDN_CONV_CH), lambda i: (i, qkv_blk)),
                  pl.BlockSpec((SUBLANES, GDN_CONV_CH), lambda i: (jnp.maximum(i * per_chunk - 1, 0), qkv_blk)),
                  pl.BlockSpec((CONV_W, GDN_CONV_CH), lambda i: (0, 0)),
                  cspec, cspec,
                  pl.BlockSpec((1, GDN_PAIRS, pr), lambda i: (i, 0, 0)),
                  cpar, cpar, rpar, rpar],
        out_specs=[mspec] * 5 + [pl.BlockSpec((1, GDN_HEADS, LANE), lambda i: (i, 0, 0))],
        out_shape=[mshape(F32), mshape(BF16), mshape(BF16), mshape(BF16), mshape(BF16),
                   jax.ShapeDtypeStruct((n_chunks, GDN_HEADS, LANE), F32)],
        compiler_params=_cparams(("parallel",)),
        name="gdn_intra",
    )(h, h, conv_w, a_col, b_col, a_row, alog_c, dtb_c, alog_r, dtb_r)


def _gdn_inter_kernel(u_ref, w_ref, att_ref, qd_ref, kd_ref, edl_ref, zb_ref, gnw_ref,
                      o_ref, s_out_ref, s_ref):
    c = pl.program_id(1)
    n_c = pl.num_programs(1)
    cs = GDN_CHUNK

    @pl.when(c == 0)
    def _():
        s_ref[...] = jnp.zeros_like(s_ref)

    n_seq = u_ref.shape[1]
    heads = [(sq, p, i, GDN_PAIR * p + i, slice(i * cs, (i + 1) * cs))
             for sq in range(n_seq) for p in range(GDN_PAIRS) for i in range(GDN_PAIR)]
    s_old = {(sq, hh): s_ref[sq, hh] for (sq, _, _, hh, _) in heads}
    s_bf = {key: s.astype(BF16) for key, s in s_old.items()}
    ws = {(sq, hh): _dot(w_ref[0, sq, 0, p, rows, :], s_bf[sq, hh]) for (sq, p, _, hh, rows) in heads}
    qss = {(sq, hh): _dot(qd_ref[0, sq, 0, p, rows, :], s_bf[sq, hh]) for (sq, p, _, hh, rows) in heads}
    vn_b, o2 = {}, {}
    for sq in range(n_seq):
        for p in range(GDN_PAIRS):
            hs = [GDN_PAIR * p + i for i in range(GDN_PAIR)]
            vn = u_ref[0, sq, 0, p] - jnp.concatenate([ws[sq, hh] for hh in hs], axis=0)
            vn_b[sq, p] = vn.astype(BF16)
            o2[sq, p] = (jnp.concatenate([qss[sq, hh] for hh in hs], axis=0)
                         + _dot(att_ref[0, sq, 0, p], vn_b[sq, p]))
    outs = {sq: [] for sq in range(n_seq)}
    for (sq, p, i, hh, rows) in heads:
        edl = edl_ref[0, sq, 0, hh:hh + 1, :]
        upd = lax.dot_general(kd_ref[0, sq, 0, p, rows, :], vn_b[sq, p][rows], (((0,), (0,)), ((), ())),
                              preferred_element_type=F32)
        s_ref[sq, hh] = s_old[sq, hh] * edl + upd
        outs[sq].append(_rms(o2[sq, p][rows], gnw_ref[...])
                        * _silu(zb_ref[0, sq, :, hh * GDN_DV:(hh + 1) * GDN_DV]))
    for sq in range(n_seq):
        o_ref[0, sq] = jnp.concatenate(outs[sq], axis=1)

    @pl.when(c == n_c - 1)
    def _():
        s_out_ref[0] = s_ref[...]


def _gdn_inter(u, w, att, qd, kd, edl, h, gnw, bsz, t):
    nc = t // GDN_CHUNK
    wd = GDN_V_WIDTH
    pr = GDN_PROWS
    n_seq = next(k for k in (4, 2, 1) if bsz % k == 0)
    nb = bsz // n_seq
    by_seq = lambda x: x.reshape((nb, n_seq, nc) + x.shape[1:])
    mspec = pl.BlockSpec((1, n_seq, 1, GDN_PAIRS, pr, GDN_DK), lambda b, c: (b, 0, c, 0, 0, 0))
    ob, s_fin = pl.pallas_call(
        _gdn_inter_kernel,
        grid=(nb, nc),
        in_specs=[mspec] * 5 + [pl.BlockSpec((1, n_seq, 1, GDN_HEADS, LANE), lambda b, c: (b, 0, c, 0, 0)),
                                pl.BlockSpec((1, n_seq, GDN_CHUNK, wd), lambda b, c: (b, 0, c, C_ZB // wd)),
                                pl.BlockSpec((1, GDN_DV), lambda b, c: (0, 0))],
        out_specs=[pl.BlockSpec((1, n_seq, GDN_CHUNK, wd), lambda b, c: (b, 0, c, 0)),
                   pl.BlockSpec((1, n_seq, GDN_HEADS, GDN_DK, GDN_DV), lambda b, c: (b, 0, 0, 0, 0))],
        out_shape=[jax.ShapeDtypeStruct((nb, n_seq, t, wd), F32),
                   jax.ShapeDtypeStruct((nb, n_seq, GDN_HEADS, GDN_DK, GDN_DV), F32)],
        scratch_shapes=[pltpu.VMEM((n_seq, GDN_HEADS, GDN_DK, GDN_DV), F32)],
        compiler_params=_cparams(("parallel", "arbitrary")),
        name="gdn_inter",
    )(by_seq(u), by_seq(w), by_seq(att), by_seq(qd), by_seq(kd), by_seq(edl),
      h.reshape(nb, n_seq, t, h.shape[-1]), gnw.reshape(1, -1))
    return ob.reshape(bsz * t, wd), s_fin.reshape(bsz, GDN_HEADS, GDN_DK, GDN_DV)


def _out_kernel(oa_ref, ob_ref, ga_ref, gb_ref, x_ref, wpa_ref, wpb_ref, wo_ref, y_ref):
    ya = _dot(oa_ref[...].astype(BF16), wpa_ref[...])
    yb = _dot(ob_ref[...].astype(BF16), wpb_ref[...])
    m = jax.nn.sigmoid(ga_ref[...]) * ya + jax.nn.sigmoid(gb_ref[...]) * yb
    y_ref[...] = x_ref[...] + _dot(m.astype(BF16), wo_ref[...])


def _out_proj(oa, ob, h, x2d, w_pa, w_pb, w_o, tm):
    n, d = x2d.shape
    once = pl.Buffered(1)
    return pl.pallas_call(
        _out_kernel,
        grid=(n // tm,),
        in_specs=[pl.BlockSpec((tm, NSA_WIDTH), lambda i: (i, 0)),
                  pl.BlockSpec((tm, GDN_V_WIDTH), lambda i: (i, 0)),
                  pl.BlockSpec((tm, d), lambda i: (i, C_GA // d)),
                  pl.BlockSpec((tm, d), lambda i: (i, C_GB // d)),
                  pl.BlockSpec((tm, d), lambda i: (i, 0)),
                  pl.BlockSpec((NSA_WIDTH, d), lambda i: (0, 0), pipeline_mode=once),
                  pl.BlockSpec((GDN_V_WIDTH, d), lambda i: (0, 0), pipeline_mode=once),
                  pl.BlockSpec((d, d), lambda i: (0, 0), pipeline_mode=once)],
        out_specs=pl.BlockSpec((tm, d), lambda i: (i, 0)),
        out_shape=jax.ShapeDtypeStruct((n, d), F32),
        compiler_params=_cparams(("parallel",)),
        name="out_proj",
    )(oa, ob, h, h, x2d, w_pa, w_pb, w_o)


def _group_slopes(g):
    r_of = lax.broadcasted_iota(jnp.int32, (NSA_GROUP, 1), 0)
    slope = jnp.zeros((NSA_GROUP, 1), F32)
    for r in range(NSA_GROUP):
        s_r = jnp.where(g == 0, 2.0 ** -(r + 1.0), 2.0 ** -(r + 1.0 + NSA_GROUP)).astype(F32)
        slope = jnp.where(r_of == r, s_r, slope)
    return slope


def _attn_sample1_kernel(q_ref, kc_ref, vc_ref, selmap_ref, stw_ref, kvw_ref,
                         ocmp_ref, owin_ref, idx_ref, *, past_len, n_sel, n_sel_pad):
    qpos = past_len
    n_cmp = kc_ref.shape[2]
    wb = stw_ref.shape[1] // KV_ROWS
    rowi = lax.broadcasted_iota(jnp.int32, (LANE, n_sel_pad), 0)
    jrow = lax.broadcasted_iota(jnp.int32, (1, n_sel_pad), 1)
    scores = []
    for g in range(NSA_KV_HEADS):
        slope = _group_slopes(g)
        qg = q_ref[0, g].astype(BF16)
        kend = lax.broadcasted_iota(jnp.int32, (1, n_cmp), 1) * CMP_STRIDE + (CMP_LEN - 1)
        d = qpos - kend
        s = _dot_nt(qg, kc_ref[0, g].astype(BF16)) * SCALE - slope * d.astype(F32)
        p = _masked_softmax(s, d >= 0).astype(BF16)
        ocmp_ref[0, g] = _dot(p, vc_ref[0, g].astype(BF16))
        imp = jnp.sum(_dot(p, selmap_ref[...]), axis=0, keepdims=True)
        visible = jrow * SEL_BLOCK <= qpos
        back = qpos // SEL_BLOCK - jrow
        forced = (jrow == 0) | ((back >= 0) & (back < N_LOCAL_SEL))
        sc = jnp.where(visible, jnp.where(forced, FORCE_SCORE, imp), -FORCE_SCORE)
        scores.append(jnp.where(jrow < n_sel, sc, -3e38))
        kw = stw_ref[0, pl.ds(g, wb, stride=KV_ROWS), :].astype(BF16)
        vw = stw_ref[0, pl.ds(NSA_KV_HEADS + g, wb, stride=KV_ROWS), :].astype(BF16)
        knew = kvw_ref[0, g:g + 1, :].astype(BF16).astype(F32)
        vnew = kvw_ref[0, NSA_KV_HEADS + g:NSA_KV_HEADS + g + 1, :].astype(BF16).astype(F32)
        dwin = wb - lax.broadcasted_iota(jnp.int32, (1, wb), 1)
        sw = _dot_nt(qg, kw) * SCALE - slope * dwin.astype(F32)
        ok = (dwin >= 0) & (dwin <= WINDOW)
        sw = jnp.where(ok, sw, NEG)
        s_new = jnp.sum(qg.astype(F32) * knew, axis=-1, keepdims=True) * SCALE
        m = jnp.maximum(jnp.max(sw, axis=-1, keepdims=True), s_new)
        e = jnp.where(ok, jnp.exp(sw - m), 0.0)
        e_new = jnp.exp(s_new - m)
        den = jnp.sum(e, axis=-1, keepdims=True) + e_new
        owin_ref[0, g] = (_dot((e / den).astype(BF16), vw)
                          + (e_new / den).astype(BF16).astype(F32) * vnew)

    stacked = jnp.zeros((LANE, n_sel_pad), F32)
    for g in range(NSA_KV_HEADS):
        stacked = jnp.where(rowi == g, scores[g], stacked)
    cols = stacked.T
    ii = lax.broadcasted_iota(jnp.int32, (n_sel_pad, n_sel_pad), 0)
    jj = lax.broadcasted_iota(jnp.int32, (n_sel_pad, n_sel_pad), 1)
    earlier = (ii < jj).astype(F32)
    kcol = lax.broadcasted_iota(jnp.int32, (SEL_TOPK, 1), 0).astype(F32)
    for g in range(NSA_KV_HEADS):
        ci = cols[:, g:g + 1]
        rj = scores[g]
        beats = jnp.where(ci > rj, 1.0, jnp.where(ci == rj, earlier, 0.0))
        rank = jnp.sum(beats, axis=0, keepdims=True)
        onehot = rank == kcol
        idx = jnp.sum(jnp.where(onehot, jrow.astype(F32), 0.0), axis=-1, keepdims=True)
        idx_ref[0, g] = idx.astype(jnp.int32)


def _attn_sample1(q_s, kc_s, vc_s, st_win_rows, kvw_new, past_len):
    dbsz = q_s.shape[0]
    n_cmp = kc_s.shape[2]
    n_sel = -(-(past_len + 1) // SEL_BLOCK)
    n_sel_pad = -(-n_sel // LANE) * LANE
    assert n_sel >= SEL_TOPK
    selmap = _sel_map(n_cmp, n_sel, n_sel_pad)
    wrows = st_win_rows.shape[1]
    hshape = jax.ShapeDtypeStruct((dbsz, NSA_KV_HEADS, NSA_GROUP, HEAD_DIM), F32)
    hspec = pl.BlockSpec((1, NSA_KV_HEADS, NSA_GROUP, HEAD_DIM), lambda b: (b, 0, 0, 0))
    cspec = pl.BlockSpec((1, NSA_KV_HEADS, n_cmp, HEAD_DIM), lambda b: (b, 0, 0, 0))
    return pl.pallas_call(
        functools.partial(_attn_sample1_kernel, past_len=past_len, n_sel=n_sel, n_sel_pad=n_sel_pad),
        grid=(dbsz,),
        in_specs=[hspec, cspec, cspec,
                  pl.BlockSpec((n_cmp, n_sel_pad), lambda b: (0, 0)),
                  pl.BlockSpec((1, wrows, HEAD_DIM), lambda b: (b, 0, 0)),
                  pl.BlockSpec((1, 2 * NSA_KV_HEADS, HEAD_DIM), lambda b: (b, 0, 0))],
        out_specs=[hspec, hspec,
                   pl.BlockSpec((1, NSA_KV_HEADS, SEL_TOPK, 1), lambda b: (b, 0, 0, 0))],
        out_shape=[hshape, hshape,
                   jax.ShapeDtypeStruct((dbsz, NSA_KV_HEADS, SEL_TOPK, 1), jnp.int32)],
        compiler_params=_cparams(("parallel",)),
        name="attn_sample_cmp_win",
    )(q_s, kc_s, vc_s, selmap, st_win_rows, kvw_new)


BLOCK_ROWS = SEL_BLOCK * KV_ROWS


def _attn_sample2_kernel(idx_ref, pt_ref, q_ref, cache_ref, kvn_ref, ocmp_ref, owin_ref, gate_ref, za_ref,
                         o_ref, buf_ref, sem_ref, *, past_len, n_pages):
    i = pl.program_id(0)
    n_i = pl.num_programs(0)
    slot = i % 2
    g = i % NSA_KV_HEADS
    qpos = past_len
    n_past_blocks = past_len // SEL_BLOCK
    per_page = PAGE_SIZE // SEL_BLOCK
    n_keys = SEL_TOPK * SEL_BLOCK

    def block_copy(step, k, to_slot):
        blk = jnp.minimum(idx_ref[step * SEL_TOPK + k], n_past_blocks - 1)
        phys = pt_ref[(step // NSA_KV_HEADS) * n_pages + blk // per_page] * per_page + blk % per_page
        return pltpu.make_async_copy(cache_ref.at[pl.ds(phys * BLOCK_ROWS, BLOCK_ROWS)],
                                     buf_ref.at[to_slot, pl.ds(k * BLOCK_ROWS, BLOCK_ROWS)], sem_ref.at[to_slot])

    @pl.when(i == 0)
    def _():
        for k in range(SEL_TOPK):
            block_copy(0, k, 0).start()

    @pl.when(i + 1 < n_i)
    def _():
        for k in range(SEL_TOPK):
            block_copy(i + 1, k, 1 - slot).start()

    for k in range(SEL_TOPK):
        block_copy(i, k, slot).wait()

    slope = _group_slopes(g)
    q = q_ref[0, 0].astype(BF16)
    kk = buf_ref[slot, pl.ds(g, n_keys, stride=KV_ROWS), :].astype(BF16)
    vv = buf_ref[slot, pl.ds(NSA_KV_HEADS + g, n_keys, stride=KV_ROWS), :].astype(BF16)
    lane_blk = lax.broadcasted_iota(jnp.int32, (1, n_keys), 1) // SEL_BLOCK
    tok = lax.broadcasted_iota(jnp.int32, (1, n_keys), 1) % SEL_BLOCK
    for k in range(SEL_TOPK):
        blk = idx_ref[i * SEL_TOPK + k]
        tok = tok + jnp.where(lane_blk == k, jnp.where(blk < n_past_blocks, blk * SEL_BLOCK, qpos + 1), 0)
    dd = qpos - tok
    mask = dd >= 0
    sc = jnp.where(mask, _dot_nt(q, kk) * SCALE - slope * dd.astype(F32), NEG)
    knew = kvn_ref[0, 0, 0].astype(BF16).astype(F32)
    vnew = kvn_ref[0, 1, 0].astype(BF16).astype(F32)
    s_new = jnp.sum(q.astype(F32) * knew, axis=-1, keepdims=True) * SCALE
    m = jnp.maximum(jnp.max(sc, axis=-1, keepdims=True), s_new)
    p = jnp.where(mask, jnp.exp(sc - m), 0.0)
    p_new = jnp.exp(s_new - m)
    den = jnp.sum(p, axis=-1, keepdims=True) + p_new
    o_slc = (_dot(p.astype(BF16), vv) + p_new.astype(BF16).astype(F32) * vnew) / den
    tot = (jax.nn.sigmoid(gate_ref[0, 0, 0]) * ocmp_ref[0, 0]
           + jax.nn.sigmoid(gate_ref[0, 1, 0]) * o_slc
           + jax.nn.sigmoid(gate_ref[0, 2, 0]) * owin_ref[0, 0])
    o_ref[0, 0] = tot * _silu(za_ref[0, 0])


def _attn_sample2(idx, page_table, q_s, cache_rows, kvs_new, o_cmp, o_win, gates, za_s, past_len):
    dbsz, n_pages = page_table.shape
    ng = NSA_KV_HEADS
    head = pl.BlockSpec((1, 1, NSA_GROUP, HEAD_DIM), lambda i, x, p: (i // ng, i % ng, 0, 0))
    grid_spec = pltpu.PrefetchScalarGridSpec(
        num_scalar_prefetch=2,
        grid=(dbsz * ng,),
        in_specs=[head,
                  pl.BlockSpec(memory_space=pl.ANY),
                  pl.BlockSpec((1, 2, 1, 1, HEAD_DIM), lambda i, x, p: (i // ng, 0, i % ng, 0, 0)),
                  head, head,
                  pl.BlockSpec((1, 3, 1, NSA_GROUP, 1), lambda i, x, p: (i // ng, 0, i % ng, 0, 0)),
                  head],
        out_specs=head,
        scratch_shapes=[pltpu.VMEM((2, SEL_TOPK * BLOCK_ROWS, HEAD_DIM), F32),
                        pltpu.SemaphoreType.DMA((2,))],
    )
    return pl.pallas_call(
        functools.partial(_attn_sample2_kernel, past_len=past_len, n_pages=n_pages),
        grid_spec=grid_spec,
        out_shape=jax.ShapeDtypeStruct((dbsz, ng, NSA_GROUP, HEAD_DIM), F32),
        compiler_params=_cparams(("arbitrary",)),
        name="attn_sample_slc",
    )(idx.reshape(-1), page_table.reshape(-1), q_s, cache_rows, kvs_new, o_cmp, o_win, gates, za_s)


def _gdn_sample_kernel(x_ref, cst_ref, cw_ref, a_ref, b_ref, alog_ref, dtb_ref, s_ref, gnw_ref, zb_ref,
                       o_ref, s_out_ref):
    acc = None
    for i in range(CONV_W):
        row = cst_ref[0, i] if i < CONV_W - 1 else x_ref[0]
        term = row * cw_ref[i]
        acc = term if acc is None else acc + term
    c = _silu(acc)
    q = _l2(c[0:GDN_HEADS]) * (GDN_DK ** -0.5)
    k = _l2(c[GDN_HEADS:2 * GDN_HEADS])
    v = c[2 * GDN_HEADS:3 * GDN_HEADS]
    gdec = -jnp.exp(alog_ref[...]) * _softplus(a_ref[0] + dtb_ref[...])
    beta = jax.nn.sigmoid(b_ref[0])
    eg = jnp.exp(gdec)
    qk = jnp.sum(q * k, axis=-1, keepdims=True)
    pad = jnp.concatenate([q, k, jnp.zeros((LANE - 2 * GDN_HEADS, GDN_DK), F32)], axis=0)
    cols = pad.T
    outs = []
    for hh in range(GDN_HEADS):
        s_h = s_ref[0, hh]
        qcol = cols[:, hh:hh + 1]
        kcol = cols[:, GDN_HEADS + hh:GDN_HEADS + hh + 1]
        b_h = beta[hh:hh + 1]
        e_h = eg[hh:hh + 1]
        k_s = jnp.sum(kcol * s_h, axis=0, keepdims=True)
        q_s = jnp.sum(qcol * s_h, axis=0, keepdims=True)
        vn = b_h * v[hh:hh + 1] - (b_h * e_h) * k_s
        outs.append(e_h * q_s + qk[hh:hh + 1] * vn)
        s_out_ref[0, hh] = s_h * e_h + kcol * vn
    o = jnp.concatenate(outs, axis=0)
    o_ref[0] = _rms(o, gnw_ref[...]) * _silu(zb_ref[0])


def _gdn_sample(x_s, cst, cw, a_s, b_s, a_log, dt_bias, s0, gnw, zb_s):
    dbsz = x_s.shape[0]
    nh3 = 3 * GDN_HEADS
    col = pl.BlockSpec((1, GDN_HEADS, 1), lambda b: (b, 0, 0))
    par = pl.BlockSpec((GDN_HEADS, 1), lambda b: (0, 0))
    head = pl.BlockSpec((1, GDN_HEADS, GDN_DV), lambda b: (b, 0, 0))
    sspec = pl.BlockSpec((1, GDN_HEADS, GDN_DK, GDN_DV), lambda b: (b, 0, 0, 0))
    return pl.pallas_call(
        _gdn_sample_kernel,
        grid=(dbsz,),
        in_specs=[pl.BlockSpec((1, nh3, GDN_DK), lambda b: (b, 0, 0)),
                  pl.BlockSpec((1, CONV_W - 1, nh3, GDN_DK), lambda b: (b, 0, 0, 0)),
                  pl.BlockSpec((CONV_W, nh3, GDN_DK), lambda b: (0, 0, 0)),
                  col, col, par, par, sspec,
                  pl.BlockSpec((1, GDN_DV), lambda b: (0, 0)),
                  head],
        out_specs=[head, sspec],
        out_shape=[jax.ShapeDtypeStruct((dbsz, GDN_HEADS, GDN_DV), F32),
                   jax.ShapeDtypeStruct(s0.shape, F32)],
        compiler_params=_cparams(("parallel",)),
        name="gdn_sample",
    )(x_s, cst, cw, a_s, b_s, a_log.reshape(-1, 1), dt_bias.reshape(-1, 1), s0, gnw.reshape(1, -1), zb_s)


def _permute_w_in(w_in):
    splits = (NSA_WIDTH, 2 * NSA_KV_WIDTH, 2 * NSA_KV_WIDTH, 2 * NSA_KV_WIDTH, 3 * NSA_HEADS, NSA_WIDTH,
              GDN_CONV_CH, GDN_HEADS, GDN_HEADS, GDN_V_WIDTH, w_in.shape[0], w_in.shape[0])
    bounds = np.cumsum(np.array(splits))[:-1].tolist()
    (q_a, kvc, kvs, kvw, g_nsa, z_a, qkv_b, a_b, b_b, z_b, gate_a, gate_b) = jnp.split(w_in, bounds, axis=1)
    small_pad = jnp.zeros((w_in.shape[0], H_WIDTH - C_SM - SM_B - GDN_HEADS), w_in.dtype)
    return jnp.concatenate([q_a, z_a, z_b, qkv_b, gate_a, gate_b, kvc, kvs, kvw, g_nsa, a_b, b_b, small_pad],
                           axis=1).astype(BF16)


def _cmp_w1(w1):
    return jnp.transpose(w1, (1, 2, 0, 3)).reshape(CMP_STRIDE, HEAD_DIM, CMP_R * CMP_HIDDEN).astype(BF16)


def _layer(x_p, x_s, cache_cmp, cache_slc, layer, st_win, st_s, st_conv, page_table, w):
    (norm_w, w_in, q_norm_w, k_norm_cmp_w, k_norm_slc_w, k_norm_win_w,
     cmp_w1_k, cmp_b1_k, cmp_w2_k, cmp_w1_v, cmp_b1_v, cmp_w2_v,
     conv_w, a_log, dt_bias, gdn_norm_w, w_pa, w_pb, w_o) = w
    bsz, t, d = x_p.shape
    dbsz, dseq, _ = x_s.shape
    assert dseq == 1 and d == C_GB - C_GA
    n_pages = page_table.shape[1]
    past_len = n_pages * PAGE_SIZE

    w_perm = _permute_w_in(w_in)
    head_w = jnp.stack([q_norm_w, k_norm_slc_w, k_norm_win_w])
    w1k, w1v = _cmp_w1(cmp_w1_k), _cmp_w1(cmp_w1_v)
    w2k, w2v = cmp_w2_k.astype(BF16), cmp_w2_v.astype(BF16)
    wpa, wpb, wo = w_pa.astype(BF16), w_pb.astype(BF16), w_o.astype(BF16)
    kvw_cols = 2 * NSA_KV_WIDTH

    n = bsz * t
    x2 = x_p.reshape(n, d)
    h, kvc_rows, kvs_rows, kvw_rows = _proj(x2, norm_w, w_perm, head_w, tm=min(1024, n))
    pk, pv = _cmp_part_prompt(h, w1k, w1v, bsz, t)
    kc, vc = _cmp_finish(pk, pv, cmp_b1_k, w2k, cmp_b1_v, w2v, k_norm_cmp_w)
    oa = _attn_prompt(h, kc, vc, bsz, t)
    nc = t // GDN_CHUNK
    ab5 = h[:, C_SM + SM_A:C_SM + SM_B + GDN_HEADS].reshape(bsz * nc, GDN_CHUNK, 2, GDN_PAIRS, GDN_PAIR)
    col_form = lambda x: x.transpose(0, 3, 1, 2).reshape(bsz * nc, GDN_PROWS, GDN_PAIRS)
    a_row = ab5[:, :, 0].transpose(0, 2, 3, 1).reshape(bsz * nc, GDN_PAIRS, GDN_PROWS)
    u, wmat, att, qd, kd, edl = _gdn_intra(h, conv_w, col_form(ab5[:, :, 0]), col_form(ab5[:, :, 1]), a_row,
                                           a_log, dt_bias, bsz, t)
    ob, s_p = _gdn_inter(u, wmat, att, qd, kd, edl, h, gdn_norm_w, bsz, t)
    y_p = _out_proj(oa, ob, h, x2, wpa, wpb, wo, tm=min(256, n)).reshape(bsz, t, d)

    kv_shape = (bsz, t, 2, NSA_KV_HEADS, HEAD_DIM)
    cmp_p = kvc_rows.reshape(kv_shape)
    slc_p = kvs_rows.reshape(kv_shape)
    wlen = min(WINDOW, t)
    win_p = kvw_rows.reshape(kv_shape)[:, t - wlen:]
    conv_p = h.reshape(bsz, t, H_WIDTH)[:, t - (CONV_W - 1):, C_QKVB:C_QKVB + GDN_CONV_CH]

    xs2 = x_s.reshape(dbsz, d)
    hs, kvc_s, kvs_s, kvw_s = _proj(xs2, norm_w, w_perm, head_w, tm=dbsz)
    grp = (dbsz, NSA_KV_HEADS, NSA_GROUP, HEAD_DIM)
    q_s = hs[:, C_Q:C_Q + NSA_WIDTH].reshape(grp)
    za_s = hs[:, C_ZA:C_ZA + NSA_WIDTH].reshape(grp)
    pt_phys = page_table + layer * cache_cmp.shape[1]
    pk_s, pv_s = _cmp_part_sample(cache_cmp.reshape(-1, HEAD_DIM), pt_phys, w1k, w1v)
    kc_s, vc_s = _cmp_finish(pk_s, pv_s, cmp_b1_k, w2k, cmp_b1_v, w2v, k_norm_cmp_w)
    wb = st_win.shape[1]
    o_cmp_s, o_win_s, idx = _attn_sample1(q_s, kc_s, vc_s, st_win.reshape(dbsz, wb * KV_ROWS, HEAD_DIM),
                                          kvw_s.reshape(dbsz, 2 * NSA_KV_HEADS, HEAD_DIM), past_len)
    gates_s = hs[:, C_SM:C_SM + 3 * NSA_HEADS].reshape(dbsz, 3, NSA_KV_HEADS, NSA_GROUP, 1)
    oa_s = _attn_sample2(idx, pt_phys, q_s, cache_slc.reshape(-1, HEAD_DIM),
                         kvs_s.reshape(dbsz, 2, NSA_KV_HEADS, 1, HEAD_DIM), o_cmp_s, o_win_s, gates_s, za_s,
                         past_len)
    nh3 = 3 * GDN_HEADS
    ob_s, s_s = _gdn_sample(hs[:, C_QKVB:C_QKVB + GDN_CONV_CH].reshape(dbsz, nh3, GDN_DK),
                            st_conv.reshape(dbsz, CONV_W - 1, nh3, GDN_DK),
                            conv_w.reshape(CONV_W, nh3, GDN_DK),
                            hs[:, C_SM + SM_A:C_SM + SM_A + GDN_HEADS].reshape(dbsz, GDN_HEADS, 1),
                            hs[:, C_SM + SM_B:C_SM + SM_B + GDN_HEADS].reshape(dbsz, GDN_HEADS, 1),
                            a_log, dt_bias, st_s, gdn_norm_w,
                            hs[:, C_ZB:C_ZB + GDN_V_WIDTH].reshape(dbsz, GDN_HEADS, GDN_DV))
    y_s = _out_proj(oa_s.reshape(dbsz, NSA_WIDTH), ob_s.reshape(dbsz, GDN_V_WIDTH), hs, xs2,
                    wpa, wpb, wo, tm=dbsz).reshape(dbsz, 1, d)

    kv1 = (dbsz, 1, 2, NSA_KV_HEADS, HEAD_DIM)
    win_s = jnp.concatenate([st_win, kvw_s.reshape(kv1)], axis=1)[:, -wb:]
    conv_s = jnp.concatenate([st_conv, hs[:, None, C_QKVB:C_QKVB + GDN_CONV_CH]], axis=1)[:, -(CONV_W - 1):]
    return y_p, y_s, (cmp_p, slc_p, win_p, s_p, conv_p,
                      kvc_s.reshape(kv1), kvs_s.reshape(kv1), win_s, s_s, conv_s)


def _take_layer(x, layer):
    return x.reshape(x.shape[1:]) if x.shape[0] == 1 else x[layer]


def kernel(x_prompt, x_sample, cache_cmp_kv, cache_slc_kv, state_win_kv, state_gdn_S, state_gdn_conv, page_table, norm_w, w_in, q_norm_w, k_norm_cmp_w, k_norm_slc_w, k_norm_win_w, cmp_w1_k, cmp_b1_k, cmp_w2_k, cmp_w1_v, cmp_b1_v, cmp_w2_v, conv_w, a_log, dt_bias, gdn_norm_w, w_pa, w_pb, w_o):
    weights = (norm_w, w_in, q_norm_w, k_norm_cmp_w, k_norm_slc_w, k_norm_win_w,
               cmp_w1_k, cmp_b1_k, cmp_w2_k, cmp_w1_v, cmp_b1_v, cmp_w2_v,
               conv_w, a_log, dt_bias, gdn_norm_w, w_pa, w_pb, w_o)
    depth = norm_w.shape[0]
    y_p, y_s = x_prompt, x_sample
    per_layer = []
    for layer in range(depth):
        y_p, y_s, st = _layer(y_p, y_s, cache_cmp_kv, cache_slc_kv, layer, _take_layer(state_win_kv, layer),
                              _take_layer(state_gdn_S, layer), _take_layer(state_gdn_conv, layer), page_table,
                              tuple(_take_layer(wt, layer) for wt in weights))
        per_layer.append(st)
    stacked = [jnp.stack(a) for a in zip(*per_layer)]
    return (y_p, y_s, *stacked)
```

```python
import functools

import numpy as np
import jax
import jax.numpy as jnp
from jax import lax
from jax.experimental import pallas as pl
from jax.experimental.pallas import tpu as pltpu

F32 = jnp.float32
BF16 = jnp.bfloat16
HI = lax.Precision.HIGHEST

PAGE_SIZE = 128
HEAD_DIM = 128
NSA_HEADS = 8
NSA_KV_HEADS = 2
NSA_GROUP = NSA_HEADS // NSA_KV_HEADS
NSA_WIDTH = NSA_HEADS * HEAD_DIM
NSA_KV_WIDTH = NSA_KV_HEADS * HEAD_DIM
CMP_LEN = 32
CMP_STRIDE = 16
CMP_R = CMP_LEN // CMP_STRIDE
CMP_HIDDEN = 128
SEL_BLOCK = 64
SEL_TOPK = 16
N_LOCAL_SEL = 2
WINDOW = 512
Q_BLOCK = 128
FORCE_SCORE = 1e4
ALIBI_MAX = 8.0
GDN_HEADS = 8
GDN_DK = 128
GDN_DV = 128
GDN_QK_WIDTH = GDN_HEADS * GDN_DK
GDN_V_WIDTH = GDN_HEADS * GDN_DV
GDN_CONV_CH = 2 * GDN_QK_WIDTH + GDN_V_WIDTH
CONV_W = 4
GDN_CHUNK = 64
EPS = 1e-6
SCALE = HEAD_DIM ** -0.5
NEG = -1e30

C_Q = 0
C_ZA = 1024
C_ZB = 2048
C_QKVB = 3072
C_GA = 6144
C_GB = 8192
C_KVC = 10240
C_KVS = 10752
C_KVW = 11264
C_SM = 11776
H_WIDTH = 12288
TN = 512
PROJ_TN = 1024
SM_A = 3 * NSA_HEADS
SM_B = SM_A + GDN_HEADS

LANE = 128
KV_ROWS = 2 * NSA_KV_HEADS
PAGE_ROWS = PAGE_SIZE * KV_ROWS
VMEM_LIMIT = 56 * 1024 * 1024


def _cparams(sem, vmem=VMEM_LIMIT):
    return pltpu.CompilerParams(dimension_semantics=sem, vmem_limit_bytes=vmem)


def _silu(x):
    return x * jax.nn.sigmoid(x)


def _rms(a, w):
    return a * lax.rsqrt(jnp.mean(a * a, axis=-1, keepdims=True) + EPS) * w


def _l2(a):
    return a * lax.rsqrt(jnp.sum(a * a, axis=-1, keepdims=True) + EPS)


def _dot(a, b):
    return jnp.dot(a, b, preferred_element_type=F32)


def _dot_nt(a, b):
    return lax.dot_general(a, b, (((1,), (1,)), ((), ())), preferred_element_type=F32)


def _dot_hi(a, b):
    return jnp.dot(a, b, preferred_element_type=F32, precision=HI)


def _softplus(x):
    return jnp.maximum(x, 0.0) + jnp.log1p(jnp.exp(-jnp.abs(x)))


def _masked_softmax(s, mask):
    sm = jnp.where(mask, s, NEG)
    m = jnp.max(sm, axis=-1, keepdims=True)
    e = jnp.where(mask, jnp.exp(sm - m), 0.0)
    den = jnp.sum(e, axis=-1, keepdims=True)
    return e / jnp.where(den > 0, den, 1.0)


def _proj_kernel(x_ref, nw_ref, w_ref, hw_ref, o_ref, kvc_ref, kvs_ref, kvw_ref, xn_ref):
    j = pl.program_id(1)
    tm = x_ref.shape[0]
    kv_refs = (kvc_ref, kvs_ref, kvw_ref)
    kv_dest = _proj_kv_dest()

    @pl.when(j == 0)
    def _():
        x = x_ref[...]
        y = x * lax.rsqrt(jnp.mean(x * x, axis=-1, keepdims=True) + EPS)
        xn_ref[...] = (y * nw_ref[...]).astype(BF16)

    acc = _dot(xn_ref[...], w_ref[...])

    plan = _proj_norm_plan()
    for tile, groups in plan.items():
        @pl.when(j == tile)
        def _(groups=groups):
            parts = []
            for grp in range(PROJ_TN // LANE):
                a = acc[:, grp * LANE:(grp + 1) * LANE]
                parts.append(_rms(a, hw_ref[groups[grp]:groups[grp] + 1, :]) if grp in groups else a)
            o_ref[...] = jnp.concatenate(parts, axis=1)
            for grp, (seg, c) in kv_dest.get(tile, {}).items():
                kv_refs[seg][pl.ds(c, tm, stride=KV_ROWS), :] = parts[grp]

    assert set(kv_dest) <= set(plan)
    plain = functools.reduce(jnp.logical_and, [j != tile for tile in plan])

    @pl.when(plain)
    def _():
        o_ref[...] = acc


def _proj_kv_dest():
    dest = {}
    for seg, c0 in enumerate((C_KVC, C_KVS, C_KVW)):
        for c in range(KV_ROWS):
            col = c0 + c * LANE
            dest.setdefault(col // PROJ_TN, {})[(col % PROJ_TN) // LANE] = (seg, c)
    return dest


def _proj_norm_plan():
    plan = {}
    for c0, width, row in ((C_Q, NSA_WIDTH, 0), (C_KVS, NSA_KV_WIDTH, 1), (C_KVW, NSA_KV_WIDTH, 2)):
        for grp in range(width // LANE):
            col = c0 + grp * LANE
            plan.setdefault(col // PROJ_TN, {})[(col % PROJ_TN) // LANE] = row
    return plan


def _proj(x2d, norm_w, w_perm, head_w, tm):
    n, d = x2d.shape
    return pl.pallas_call(
        _proj_kernel,
        grid=(n // tm, H_WIDTH // PROJ_TN),
        in_specs=[pl.BlockSpec((tm, d), lambda i, j: (i, 0)),
                  pl.BlockSpec((1, d), lambda i, j: (0, 0)),
                  pl.BlockSpec((d, PROJ_TN), lambda i, j: (0, j)),
                  pl.BlockSpec((3, LANE), lambda i, j: (0, 0))],
        out_specs=[pl.BlockSpec((tm, PROJ_TN), lambda i, j: (i, j))]
        + [pl.BlockSpec((tm * KV_ROWS, LANE), lambda i, j: (i, 0), pipeline_mode=pl.Buffered(1))] * 3,
        out_shape=[jax.ShapeDtypeStruct((n, H_WIDTH), F32)]
        + [jax.ShapeDtypeStruct((n * KV_ROWS, LANE), F32)] * 3,
        scratch_shapes=[pltpu.VMEM((tm, d), BF16)],
        compiler_params=_cparams(("parallel", "arbitrary")),
        name="in_proj",
    )(x2d, norm_w.reshape(1, d), w_perm, head_w)


def _cmp_partials_body(strided_rows, w1k_ref, w1v_ref, pk_ref, pv_ref, n_chunks):
    for g in range(NSA_KV_HEADS):
        for kv, (w1_ref, p_ref) in enumerate(((w1k_ref, pk_ref), (w1v_ref, pv_ref))):
            acc = jnp.zeros((n_chunks, CMP_R * CMP_HIDDEN), F32)
            for s in range(CMP_STRIDE):
                xs = strided_rows(kv * NSA_KV_HEADS + g, s)
                acc = acc + _dot(xs.astype(BF16), w1_ref[s])
            p_ref[0, g] = acc


def _cmp_part_prompt_kernel(r0_ref, r1_ref, r2_ref, r3_ref, w1k_ref, w1v_ref, pk_ref, pv_ref, *, n_chunks):
    refs = (r0_ref, r1_ref, r2_ref, r3_ref)

    def strided_rows(c, s):
        return refs[c][pl.ds(s, n_chunks, stride=CMP_STRIDE), :]

    _cmp_partials_body(strided_rows, w1k_ref, w1v_ref, pk_ref, pv_ref, n_chunks)


def _cmp_part_prompt(h, w1k, w1v, bsz, t):
    n_chunks = t // CMP_STRIDE
    pshape = jax.ShapeDtypeStruct((bsz, NSA_KV_HEADS, n_chunks, CMP_R * CMP_HIDDEN), F32)
    pspec = pl.BlockSpec((1, NSA_KV_HEADS, n_chunks, CMP_R * CMP_HIDDEN), lambda b: (b, 0, 0, 0))
    wspec = pl.BlockSpec((CMP_STRIDE, HEAD_DIM, CMP_R * CMP_HIDDEN), lambda b: (0, 0, 0))
    return pl.pallas_call(
        functools.partial(_cmp_part_prompt_kernel, n_chunks=n_chunks),
        grid=(bsz,),
        in_specs=[pl.BlockSpec((t, HEAD_DIM), functools.partial(lambda c, b: (b, C_KVC // HEAD_DIM + c), c))
                  for c in range(2 * NSA_KV_HEADS)] + [wspec, wspec],
        out_specs=[pspec, pspec],
        out_shape=[pshape, pshape],
        compiler_params=_cparams(("parallel",)),
        name="cmp_part_prompt",
    )(h, h, h, h, w1k, w1v)


def _cmp_part_sample_kernel(pt_ref, cache_ref, w1k_ref, w1v_ref, pk_ref, pv_ref, buf_ref, tok_ref, sem_ref,
                            *, pages_per_step):
    i = pl.program_id(0)
    n_i = pl.num_programs(0)
    slot = i % 2
    n_chunks = pages_per_step * PAGE_SIZE // CMP_STRIDE

    def page_copy(step, p, to_slot):
        page = pt_ref[step * pages_per_step + p]
        return pltpu.make_async_copy(cache_ref.at[pl.ds(page * PAGE_ROWS, PAGE_ROWS)],
                                     buf_ref.at[to_slot, pl.ds(p * PAGE_ROWS, PAGE_ROWS)], sem_ref.at[to_slot])

    @pl.when(i == 0)
    def _():
        for p in range(pages_per_step):
            page_copy(0, p, 0).start()

    @pl.when(i + 1 < n_i)
    def _():
        for p in range(pages_per_step):
            page_copy(i + 1, p, 1 - slot).start()

    for p in range(pages_per_step):
        page_copy(i, p, slot).wait()

    n_tok = pages_per_step * PAGE_SIZE
    for c in range(KV_ROWS):
        tok_ref[c] = buf_ref[slot, pl.ds(c, n_tok, stride=KV_ROWS), :]
    for kv, (w1_ref, p_ref) in enumerate(((w1k_ref, pk_ref), (w1v_ref, pv_ref))):
        xs = jnp.concatenate(
            [jnp.concatenate([tok_ref[kv * NSA_KV_HEADS + g, pl.ds(s, n_chunks, stride=CMP_STRIDE), :]
                              for g in range(NSA_KV_HEADS)], axis=0).astype(BF16)
             for s in range(CMP_STRIDE)], axis=1)
        acc = _dot(xs, w1_ref[...])
        for g in range(NSA_KV_HEADS):
            p_ref[0, g] = acc[g * n_chunks:(g + 1) * n_chunks]


def _cmp_part_sample(cache_rows, page_table, w1k, w1v):
    dbsz, n_pages = page_table.shape
    pages_per_step = min(32, n_pages)
    n_steps = n_pages // pages_per_step
    cps = pages_per_step * PAGE_SIZE // CMP_STRIDE
    n_chunks = n_pages * PAGE_SIZE // CMP_STRIDE
    pshape = jax.ShapeDtypeStruct((dbsz, NSA_KV_HEADS, n_chunks, CMP_R * CMP_HIDDEN), F32)
    pspec = pl.BlockSpec((1, NSA_KV_HEADS, cps, CMP_R * CMP_HIDDEN),
                         lambda i, pt: (i // n_steps, 0, i % n_steps, 0))
    wspec = pl.BlockSpec((CMP_STRIDE * HEAD_DIM, CMP_R * CMP_HIDDEN), lambda i, pt: (0, 0))
    w1k, w1v = (w.reshape(CMP_STRIDE * HEAD_DIM, CMP_R * CMP_HIDDEN) for w in (w1k, w1v))
    grid_spec = pltpu.PrefetchScalarGridSpec(
        num_scalar_prefetch=1,
        grid=(dbsz * n_steps,),
        in_specs=[pl.BlockSpec(memory_space=pl.ANY), wspec, wspec],
        out_specs=[pspec, pspec],
        scratch_shapes=[pltpu.VMEM((2, pages_per_step * PAGE_ROWS, HEAD_DIM), F32),
                        pltpu.VMEM((KV_ROWS, pages_per_step * PAGE_SIZE, HEAD_DIM), F32),
                        pltpu.SemaphoreType.DMA((2,))],
    )
    return pl.pallas_call(
        functools.partial(_cmp_part_sample_kernel, pages_per_step=pages_per_step),
        grid_spec=grid_spec,
        out_shape=[pshape, pshape],
        compiler_params=_cparams(("arbitrary",)),
        name="cmp_part_sample",
    )(page_table.reshape(-1), cache_rows, w1k, w1v)


def _cmp_finish_kernel(pk_ref, pv_ref, b1k_ref, w2k_ref, b1v_ref, w2v_ref, nk_ref, kc_ref, vc_ref,
                       *, n_chunks):
    for g in range(NSA_KV_HEADS):
        for kv, (p_ref, b1_ref, w2_ref, o_ref) in enumerate(((pk_ref, b1k_ref, w2k_ref, kc_ref),
                                                            (pv_ref, b1v_ref, w2v_ref, vc_ref))):
            part = p_ref[0, g]
            nxt = pltpu.roll(part[:, CMP_HIDDEN:], n_chunks - 1, 0)
            hid = _silu(part[:, :CMP_HIDDEN] + nxt + b1_ref[...])
            out = _dot(hid.astype(BF16), w2_ref[...])
            if kv == 0:
                out = _rms(out, nk_ref[...])
            o_ref[0, g] = out


def _cmp_finish(pk, pv, b1k, w2k, b1v, w2v, nk):
    bsz, _, n_chunks, _ = pk.shape
    pspec = pl.BlockSpec((1, NSA_KV_HEADS, n_chunks, CMP_R * CMP_HIDDEN), lambda b: (b, 0, 0, 0))
    vspec = pl.BlockSpec((1, CMP_HIDDEN), lambda b: (0, 0))
    wspec = pl.BlockSpec((CMP_HIDDEN, HEAD_DIM), lambda b: (0, 0))
    oshape = jax.ShapeDtypeStruct((bsz, NSA_KV_HEADS, n_chunks, HEAD_DIM), F32)
    ospec = pl.BlockSpec((1, NSA_KV_HEADS, n_chunks, HEAD_DIM), lambda b: (b, 0, 0, 0))
    return pl.pallas_call(
        functools.partial(_cmp_finish_kernel, n_chunks=n_chunks),
        grid=(bsz,),
        in_specs=[pspec, pspec, vspec, wspec, vspec, wspec, vspec],
        out_specs=[ospec, ospec],
        out_shape=[oshape, oshape],
        compiler_params=_cparams(("parallel",)),
        name="cmp_finish",
    )(pk, pv, b1k.reshape(1, -1), w2k, b1v.reshape(1, -1), w2v, nk.reshape(1, -1))


def _sel_map(n_cmp, n_sel, n_sel_pad):
    cs = np.arange(n_cmp) * CMP_STRIDE
    ss = np.arange(n_sel) * SEL_BLOCK
    ov = np.minimum(cs[:, None] + CMP_LEN, ss[None] + SEL_BLOCK) - np.maximum(cs[:, None], ss[None])
    m = np.zeros((n_cmp, n_sel_pad), np.float32)
    m[:, :n_sel] = np.clip(ov, 0, None).astype(np.float32) / CMP_LEN
    return jnp.asarray(m, dtype=BF16)


def _topk_rank_rows(score, n_valid):
    j = lax.broadcasted_iota(jnp.int32, score.shape, 1)
    cnt = jnp.zeros(score.shape, F32)
    for jp in range(n_valid):
        col = score[:, jp:jp + 1]
        earlier = (j > jp).astype(F32)
        cnt = cnt + jnp.where(col > score, 1.0, jnp.where(col == score, earlier, 0.0))
    return cnt


QT = 256
MASK_BIG = 16384.0


def _attn_prompt_kernel(q_ref, ks_ref, vs_ref, kw_ref, vw_ref, kc_ref, vc_ref, sm_ref, za_ref,
                        selmap_ref, kaug_ref, caug_ref, padaug_ref, band_ref, o_ref,
                        ksa_ref, vsb_ref, kwa_ref, vwb_ref, kca_ref, vcb_ref, *, t_len):
    g = pl.program_id(1)
    qb = pl.program_id(2)
    qstart = pl.multiple_of(qb * QT, QT)
    n_sel = t_len // SEL_BLOCK
    rows = NSA_GROUP * QT

    @pl.when(qb == 0)
    def _():
        ksa_ref[:, 0:HEAD_DIM] = ks_ref[...].astype(BF16)
        ksa_ref[:, HEAD_DIM:] = kaug_ref[...]
        vsb_ref[...] = vs_ref[...].astype(BF16)
        kwa_ref[0:WINDOW, 0:HEAD_DIM] = jnp.zeros((WINDOW, HEAD_DIM), BF16)
        kwa_ref[0:WINDOW, HEAD_DIM:] = padaug_ref[...]
        kwa_ref[WINDOW:, 0:HEAD_DIM] = kw_ref[...].astype(BF16)
        kwa_ref[WINDOW:, HEAD_DIM:] = kaug_ref[...]
        vwb_ref[0:WINDOW, :] = jnp.zeros((WINDOW, HEAD_DIM), BF16)
        vwb_ref[WINDOW:, :] = vw_ref[...].astype(BF16)
        kca_ref[:, 0:HEAD_DIM] = kc_ref[0, 0].astype(BF16)
        kca_ref[:, HEAD_DIM:] = caug_ref[...]
        vcb_ref[...] = vc_ref[0, 0].astype(BF16)

    q = q_ref[...]
    qsb = (jnp.concatenate([q[:, r * HEAD_DIM:(r + 1) * HEAD_DIM] for r in range(NSA_GROUP)], axis=0)
           * SCALE).astype(BF16)
    row = lax.broadcasted_iota(jnp.int32, (rows, 1), 0)
    r_of = row // QT
    tloc = row % QT
    qpos = qstart + tloc
    slope = jnp.zeros((rows, 1), F32)
    for r in range(NSA_GROUP):
        s_r = jnp.where(g == 0, 2.0 ** -(r + 1.0), 2.0 ** -(r + 1.0 + NSA_GROUP)).astype(F32)
        slope = jnp.where(r_of == r, s_r, slope)
    lane = lax.broadcasted_iota(jnp.int32, (rows, LANE), 1)
    pos_cols = jnp.where(lane == n_sel, slope * 256.0, jnp.where(lane == n_sel + 1, slope, 0.0))
    q_w = jnp.concatenate([qsb, jnp.where(lane == n_sel + 2, -MASK_BIG, pos_cols).astype(BF16)], axis=1)

    n_cmp = kca_ref.shape[0]
    kend = lax.broadcasted_iota(jnp.int32, (1, n_cmp), 1) * CMP_STRIDE + (CMP_LEN - 1)
    p_cmp = _masked_softmax(_dot_nt(q_w, kca_ref[...]), (qpos - kend) >= 0).astype(BF16)
    o_cmp = _dot(p_cmp, vcb_ref[...])
    imp4 = _dot(p_cmp, selmap_ref[...])
    imp = imp4[0:QT]
    for r in range(1, NSA_GROUP):
        imp = imp + imp4[r * QT:(r + 1) * QT]

    n_sel_rows = -(-n_sel // SUBLANES) * SUBLANES
    imp_t = imp.T[0:n_sel_rows]
    tq = qstart + lax.broadcasted_iota(jnp.int32, (n_sel_rows, QT), 1)
    j = lax.broadcasted_iota(jnp.int32, (n_sel_rows, QT), 0)
    visible = (j * SEL_BLOCK <= tq) & (j < n_sel)
    back = tq // SEL_BLOCK - j
    forced = (j == 0) | ((back >= 0) & (back < N_LOCAL_SEL))
    score = jnp.where(visible, jnp.where(forced, FORCE_SCORE, imp_t), -FORCE_SCORE)
    score = jnp.where(j < n_sel, score, -3e38)
    rank = jnp.zeros((n_sel_rows, QT), F32)
    for jp in range(n_sel):
        other = score[jp:jp + 1, :]
        rank = rank + jnp.where(other > score, 1.0, jnp.where(other == score, (j > jp).astype(F32), 0.0))
    sel_t = jnp.where((rank < min(SEL_TOPK, n_sel)) & visible, 1.0, 0.0)
    sel = jnp.concatenate([sel_t, jnp.zeros((LANE - n_sel_rows, QT), F32)], axis=0).T
    sel4 = jnp.concatenate([sel] * NSA_GROUP, axis=0)
    q_s = jnp.concatenate([qsb, jnp.where(lane < n_sel, (sel4 - 1.0) * MASK_BIG, pos_cols).astype(BF16)],
                          axis=1)

    def online(sc, k0, carry):
        m, l, acc = carry
        m_new = jnp.maximum(m, jnp.max(sc, axis=-1, keepdims=True))
        alpha = jnp.exp(m - m_new)
        p = jnp.exp(sc - m_new)
        l = alpha * l + jnp.sum(p, axis=-1, keepdims=True)
        acc = alpha * acc + _dot(p.astype(BF16), vsb_ref[pl.ds(k0, QT), :])
        return m_new, l, acc

    def past_chunk(c, carry):
        k0 = pl.multiple_of(c * QT, QT)
        return online(_dot_nt(q_s, ksa_ref[pl.ds(k0, QT), :]), k0, carry)

    init = (jnp.full((rows, 1), NEG, F32), jnp.zeros((rows, 1), F32), jnp.zeros((rows, HEAD_DIM), F32))
    carry = lax.fori_loop(0, qb, past_chunk, init)
    ki = lax.broadcasted_iota(jnp.int32, (1, QT), 1)
    diag = jnp.where(tloc >= ki, _dot_nt(q_s, ksa_ref[pl.ds(qstart, QT), :]), -MASK_BIG)
    _, l_s, acc_s = online(diag, qstart, carry)
    o_slc = acc_s / l_s

    span = QT + WINDOW
    band = band_ref[...]
    sw = _dot_nt(q_w, kwa_ref[pl.ds(qstart, span), :]) + jnp.concatenate([band] * NSA_GROUP, axis=0)
    e_win = jnp.exp(sw - jnp.max(sw, axis=-1, keepdims=True))
    o_win = (_dot(e_win.astype(BF16), vwb_ref[pl.ds(qstart, span), :])
             / jnp.sum(e_win, axis=-1, keepdims=True))

    sm = sm_ref[...]
    outs = []
    for r in range(NSA_GROUP):
        tot = jnp.zeros((QT, HEAD_DIM), F32)
        for br, o_br in enumerate((o_cmp, o_slc, o_win)):
            c0 = br * NSA_HEADS + r
            gate = jnp.where(g == 0, sm[:, c0:c0 + 1], sm[:, c0 + NSA_GROUP:c0 + NSA_GROUP + 1])
            tot = tot + jax.nn.sigmoid(gate) * o_br[r * QT:(r + 1) * QT]
        outs.append(tot * _silu(za_ref[:, r * HEAD_DIM:(r + 1) * HEAD_DIM]))
    o_ref[...] = jnp.concatenate(outs, axis=1)


def _attn_prompt(h, kc, vc, bsz, t):
    nqb = t // QT
    n_cmp = t // CMP_STRIDE
    n_sel = t // SEL_BLOCK
    span = QT + WINDOW
    assert n_sel + 3 <= LANE and t % QT == 0 and t + CMP_LEN < 256 * 256
    selmap = _sel_map(n_cmp, n_sel, LANE)

    def aug_cols(pos, onehot_block):
        a = np.zeros((pos.shape[0], LANE), np.float32)
        if onehot_block:
            a[np.arange(pos.shape[0]), pos // SEL_BLOCK] = 1.0
        a[:, n_sel] = pos // 256
        a[:, n_sel + 1] = pos % 256
        return jnp.asarray(a, dtype=BF16)

    kaug = aug_cols(np.arange(t), True)
    caug = aug_cols(np.arange(n_cmp) * CMP_STRIDE + CMP_LEN - 1, False)
    pad = np.zeros((WINDOW, LANE), np.float32)
    pad[:, n_sel + 2] = 1.0
    padaug = jnp.asarray(pad, dtype=BF16)
    ti = np.arange(QT)[:, None]
    ii = np.arange(span)[None]
    band = jnp.asarray(np.where((ii >= ti) & (ii <= ti + WINDOW), 0.0, -MASK_BIG).astype(np.float32))
    gw = NSA_GROUP * HEAD_DIM
    row_blk = lambda b, g, i: b * nqb + i
    kv_spec = lambda c0: pl.BlockSpec((t, HEAD_DIM), lambda b, g, i: (b, c0 // HEAD_DIM + g))
    const = lambda shape: pl.BlockSpec(shape, lambda b, g, i: (0,) * len(shape))
    return pl.pallas_call(
        functools.partial(_attn_prompt_kernel, t_len=t),
        grid=(bsz, NSA_KV_HEADS, nqb),
        in_specs=[pl.BlockSpec((QT, gw), lambda b, g, i: (row_blk(b, g, i), C_Q // gw + g)),
                  kv_spec(C_KVS), kv_spec(C_KVS + NSA_KV_WIDTH),
                  kv_spec(C_KVW), kv_spec(C_KVW + NSA_KV_WIDTH),
                  pl.BlockSpec((1, 1, n_cmp, HEAD_DIM), lambda b, g, i: (b, g, 0, 0)),
                  pl.BlockSpec((1, 1, n_cmp, HEAD_DIM), lambda b, g, i: (b, g, 0, 0)),
                  pl.BlockSpec((QT, TN), lambda b, g, i: (row_blk(b, g, i), C_SM // TN)),
                  pl.BlockSpec((QT, gw), lambda b, g, i: (row_blk(b, g, i), C_ZA // gw + g)),
                  const((n_cmp, LANE)), const((t, LANE)), const((n_cmp, LANE)), const((WINDOW, LANE)),
                  const((QT, span))],
        out_specs=pl.BlockSpec((QT, gw), lambda b, g, i: (row_blk(b, g, i), g)),
        out_shape=jax.ShapeDtypeStruct((bsz * t, NSA_WIDTH), F32),
        scratch_shapes=[pltpu.VMEM((t, 2 * HEAD_DIM), BF16), pltpu.VMEM((t, HEAD_DIM), BF16),
                        pltpu.VMEM((WINDOW + t, 2 * HEAD_DIM), BF16), pltpu.VMEM((WINDOW + t, HEAD_DIM), BF16),
                        pltpu.VMEM((n_cmp, 2 * HEAD_DIM), BF16), pltpu.VMEM((n_cmp, HEAD_DIM), BF16)],
        compiler_params=_cparams(("parallel", "parallel", "arbitrary")),
        name="attn_prompt",
    )(h, h, h, h, h, kc, vc, h, h, selmap, kaug, caug, padaug, band)


SUBLANES = 8
GDN_PAIR = 2
GDN_PAIRS = GDN_HEADS // GDN_PAIR
GDN_PROWS = GDN_PAIR * GDN_CHUNK


def _split2(x):
    hi = x.astype(BF16)
    return hi, (x - hi.astype(F32)).astype(BF16)


def _split3(x):
    hi = x.astype(BF16)
    r = x - hi.astype(F32)
    mid = r.astype(BF16)
    return hi, mid, (r - mid.astype(F32)).astype(BF16)


def _dot_x3(a_parts, b_parts):
    a_hi, a_lo = a_parts
    b_hi, b_lo = b_parts
    return _dot(jnp.concatenate([a_hi, a_lo, a_hi], axis=1), jnp.concatenate([b_hi, b_hi, b_lo], axis=0))


def _gdn_intra_kernel(x_ref, prev_ref, cw_ref, acol_ref, bcol_ref, arow_ref, alogc_ref, dtbc_ref, alogr_ref,
                      dtbr_ref, u_ref, w_ref, att_ref, qd_ref, kd_ref, edl_ref, *, chunks_per_seq):
    pr = GDN_PROWS
    first = pl.program_id(0) % chunks_per_seq == 0
    full = jnp.concatenate([jnp.where(first, 0.0, prev_ref[...]), x_ref[...]], axis=0)
    conv = None
    for i in range(CONV_W):
        dly = CONV_W - 1 - i
        xs = full if dly == 0 else pltpu.roll(full, dly, 0)
        term = xs[SUBLANES:] * cw_ref[i:i + 1, :]
        conv = term if conv is None else conv + term
    conv = _silu(conv)

    def head_cols(kind, hh):
        c0 = kind * GDN_QK_WIDTH + hh * GDN_DK
        return conv[:, c0:c0 + GDN_DK]

    def stack(kind, p):
        parts = [head_cols(kind, GDN_PAIR * p + i) for i in range(GDN_PAIR)]
        if kind == 0:
            parts = [_l2(x) * (GDN_DK ** -0.5) for x in parts]
        elif kind == 1:
            parts = [_l2(x) for x in parts]
        return jnp.concatenate(parts, axis=0)

    ri = lax.broadcasted_iota(jnp.int32, (pr, pr), 0)
    ci = lax.broadcasted_iota(jnp.int32, (pr, pr), 1)
    same = (ri // GDN_CHUNK) == (ci // GDN_CHUNK)
    tril = same & (ri >= ci)
    stril = same & (ri > ci)
    eye = (ri == ci).astype(F32)
    lo_b = tril.astype(BF16)
    ut_b = (same & (ri <= ci)).astype(BF16)
    rcol = lax.broadcasted_iota(jnp.int32, (pr, 1), 0)

    gcol = -jnp.exp(alogc_ref[...]) * _softplus(acol_ref[0] + dtbc_ref[...])
    beta = jax.nn.sigmoid(bcol_ref[0])
    grow = -jnp.exp(alogr_ref[...]) * _softplus(arow_ref[0] + dtbr_ref[...])
    dec_col = sum(_dot(lo_b, part) for part in _split3(gcol))
    dec_row = sum(_dot(part, ut_b) for part in _split3(grow))

    ks, kbs, kbfs, lmats, vbs, qs, dcs = [], [], [], [], [], [], []
    for p in range(GDN_PAIRS):
        k2 = stack(1, p)
        dc = dec_col[:, p:p + 1]
        b2 = beta[:, p:p + 1]
        dmat = jnp.where(tril, dc - dec_row[p:p + 1, :], 0.0)
        ks.append(k2)
        kbs.append(k2 * b2)
        kbfs.append(k2.astype(BF16))
        lmats.append(jnp.where(tril, jnp.exp(dmat), 0.0))
        vbs.append(stack(2, p) * b2)
        qs.append(stack(0, p))
        dcs.append(dc)

    npows = [-jnp.where(stril, _dot_nt(kbs[p].astype(BF16), kbfs[p]) * lmats[p], 0.0) for p in range(GDN_PAIRS)]
    tms = [eye + n for n in npows]
    parts = [_split2(n) for n in npows]
    npows = [_dot_x3(parts[p], parts[p]) for p in range(GDN_PAIRS)]
    for _ in range(4):
        for p in range(GDN_PAIRS):
            n_hi, n_lo = _split2(npows[p])
            s_hi, s_lo = _split2(tms[p])
            both = _dot_x3((n_hi, n_lo), (jnp.concatenate([n_hi, s_hi], axis=1),
                                          jnp.concatenate([n_lo, s_lo], axis=1)))
            npows[p] = both[:, :pr]
            tms[p] = tms[p] + both[:, pr:]
    tms = [tms[p] + _dot_x3(_split2(npows[p]), _split2(tms[p])) for p in range(GDN_PAIRS)]

    edl_rows = []
    for p in range(GDN_PAIRS):
        tm_b = tms[p].astype(BF16)
        dc = dcs[p]
        edec = jnp.exp(dc)
        edl_rows += [jnp.broadcast_to(jnp.exp(dc[(i + 1) * GDN_CHUNK - 1:(i + 1) * GDN_CHUNK, :]), (1, LANE))
                     for i in range(GDN_PAIR)]
        dl = jnp.where(rcol < GDN_CHUNK, dc[GDN_CHUNK - 1:GDN_CHUNK, :], dc[pr - 1:pr, :])
        uw = _dot(tm_b, jnp.concatenate([vbs[p].astype(BF16), (kbs[p] * edec).astype(BF16)], axis=1))
        u_ref[0, p] = uw[:, :GDN_DV]
        w_ref[0, p] = uw[:, GDN_DV:].astype(BF16)
        att_ref[0, p] = jnp.where(tril, _dot_nt(qs[p].astype(BF16), kbfs[p]) * lmats[p], 0.0).astype(BF16)
        qd_ref[0, p] = (qs[p] * edec).astype(BF16)
        kd_ref[0, p] = (ks[p] * jnp.exp(dl - dc)).astype(BF16)
    edl_ref[0] = jnp.concatenate(edl_rows, axis=0)


def _gdn_intra(h, conv_w, a_col, b_col, a_row, a_log, dt_bias, bsz, t):
    nc = t // GDN_CHUNK
    n_chunks = bsz * nc
    pr = GDN_PROWS
    per_chunk = GDN_CHUNK // SUBLANES
    alog_c = jnp.tile(a_log.reshape(GDN_PAIRS, GDN_PAIR).T[:, None, :], (1, GDN_CHUNK, 1)).reshape(pr, GDN_PAIRS)
    dtb_c = jnp.tile(dt_bias.reshape(GDN_PAIRS, GDN_PAIR).T[:, None, :], (1, GDN_CHUNK, 1)).reshape(pr, GDN_PAIRS)
    alog_r = jnp.repeat(a_log.reshape(GDN_PAIRS, GDN_PAIR), GDN_CHUNK, axis=1)
    dtb_r = jnp.repeat(dt_bias.reshape(GDN_PAIRS, GDN_PAIR), GDN_CHUNK, axis=1)
    cspec = pl.BlockSpec((1, pr, GDN_PAIRS), lambda i: (i, 0, 0))
    cpar = pl.BlockSpec((pr, GDN_PAIRS), lambda i: (0, 0))
    rpar = pl.BlockSpec((GDN_PAIRS, pr), lambda i: (0, 0))
    mspec = pl.BlockSpec((1, GDN_PAIRS, pr, GDN_DK), lambda i: (i, 0, 0, 0))
    mshape = lambda dt: jax.ShapeDtypeStruct((n_chunks, GDN_PAIRS, pr, GDN_DK), dt)
    qkv_blk = C_QKVB // GDN_CONV_CH
    return pl.pallas_call(
        functools.partial(_gdn_intra_kernel, chunks_per_seq=nc),
        grid=(n_chunks,),
        in_specs=[pl.BlockSpec((GDN_CHUNK, GDN_CONV_CH), lambda i: (i, qkv_blk)),
                  pl.BlockSpec((SUBLANES, GDN_CONV_CH), lambda i: (jnp.maximum(i * per_chunk - 1, 0), qkv_blk)),
                  pl.BlockSpec((CONV_W, GDN_CONV_CH), lambda i: (0, 0)),
                  cspec, cspec,
                  pl.BlockSpec((1, GDN_PAIRS, pr), lambda i: (i, 0, 0)),
                  cpar, cpar, rpar, rpar],
        out_specs=[mspec] * 5 + [pl.BlockSpec((1, GDN_HEADS, LANE), lambda i: (i, 0, 0))],
        out_shape=[mshape(F32), mshape(BF16), mshape(BF16), mshape(BF16), mshape(BF16),
                   jax.ShapeDtypeStruct((n_chunks, GDN_HEADS, LANE), F32)],
        compiler_params=_cparams(("parallel",)),
        name="gdn_intra",
    )(h, h, conv_w, a_col, b_col, a_row, alog_c, dtb_c, alog_r, dtb_r)


def _gdn_inter_kernel(u_ref, w_ref, att_ref, qd_ref, kd_ref, edl_ref, zb_ref, gnw_ref,
                      o_ref, s_out_ref, s_ref):
    c = pl.program_id(1)
    n_c = pl.num_programs(1)
    cs = GDN_CHUNK

    @pl.when(c == 0)
    def _():
        s_ref[...] = jnp.zeros_like(s_ref)

    n_seq = u_ref.shape[1]
    heads = [(sq, p, i, GDN_PAIR * p + i, slice(i * cs, (i + 1) * cs))
             for sq in range(n_seq) for p in range(GDN_PAIRS) for i in range(GDN_PAIR)]
    s_old = {(sq, hh): s_ref[sq, hh] for (sq, _, _, hh, _) in heads}
    s_bf = {key: s.astype(BF16) for key, s in s_old.items()}
    ws = {(sq, hh): _dot(w_ref[0, sq, 0, p, rows, :], s_bf[sq, hh]) for (sq, p, _, hh, rows) in heads}
    qss = {(sq, hh): _dot(qd_ref[0, sq, 0, p, rows, :], s_bf[sq, hh]) for (sq, p, _, hh, rows) in heads}
    vn_b, o2 = {}, {}
    for sq in range(n_seq):
        for p in range(GDN_PAIRS):
            hs = [GDN_PAIR * p + i for i in range(GDN_PAIR)]
            vn = u_ref[0, sq, 0, p] - jnp.concatenate([ws[sq, hh] for hh in hs], axis=0)
            vn_b[sq, p] = vn.astype(BF16)
            o2[sq, p] = (jnp.concatenate([qss[sq, hh] for hh in hs], axis=0)
                         + _dot(att_ref[0, sq, 0, p], vn_b[sq, p]))
    outs = {sq: [] for sq in range(n_seq)}
    for (sq, p, i, hh, rows) in heads:
        edl = edl_ref[0, sq, 0, hh:hh + 1, :]
        upd = lax.dot_general(kd_ref[0, sq, 0, p, rows, :], vn_b[sq, p][rows], (((0,), (0,)), ((), ())),
                              preferred_element_type=F32)
        s_ref[sq, hh] = s_old[sq, hh] * edl + upd
        outs[sq].append(_rms(o2[sq, p][rows], gnw_ref[...])
                        * _silu(zb_ref[0, sq, :, hh * GDN_DV:(hh + 1) * GDN_DV]))
    for sq in range(n_seq):
        o_ref[0, sq] = jnp.concatenate(outs[sq], axis=1)

    @pl.when(c == n_c - 1)
    def _():
        s_out_ref[0] = s_ref[...]


def _gdn_inter(u, w, att, qd, kd, edl, h, gnw, bsz, t):
    nc = t // GDN_CHUNK
    wd = GDN_V_WIDTH
    pr = GDN_PROWS
    n_seq = next(k for k in (4, 2, 1) if bsz % k == 0)
    nb = bsz // n_seq
    by_seq = lambda x: x.reshape((nb, n_seq, nc) + x.shape[1:])
    mspec = pl.BlockSpec((1, n_seq, 1, GDN_PAIRS, pr, GDN_DK), lambda b, c: (b, 0, c, 0, 0, 0))
    ob, s_fin = pl.pallas_call(
        _gdn_inter_kernel,
        grid=(nb, nc),
        in_specs=[mspec] * 5 + [pl.BlockSpec((1, n_seq, 1, GDN_HEADS, LANE), lambda b, c: (b, 0, c, 0, 0)),
                                pl.BlockSpec((1, n_seq, GDN_CHUNK, wd), lambda b, c: (b, 0, c, C_ZB // wd)),
                                pl.BlockSpec((1, GDN_DV), lambda b, c: (0, 0))],
        out_specs=[pl.BlockSpec((1, n_seq, GDN_CHUNK, wd), lambda b, c: (b, 0, c, 0)),
                   pl.BlockSpec((1, n_seq, GDN_HEADS, GDN_DK, GDN_DV), lambda b, c: (b, 0, 0, 0, 0))],
        out_shape=[jax.ShapeDtypeStruct((nb, n_seq, t, wd), F32),
                   jax.ShapeDtypeStruct((nb, n_seq, GDN_HEADS, GDN_DK, GDN_DV), F32)],
        scratch_shapes=[pltpu.VMEM((n_seq, GDN_HEADS, GDN_DK, GDN_DV), F32)],
        compiler_params=_cparams(("parallel", "arbitrary")),
        name="gdn_inter",
    )(by_seq(u), by_seq(w), by_seq(att), by_seq(qd), by_seq(kd), by_seq(edl),
      h.reshape(nb, n_seq, t, h.shape[-1]), gnw.reshape(1, -1))
    return ob.reshape(bsz * t, wd), s_fin.reshape(bsz, GDN_HEADS, GDN_DK, GDN_DV)


def _out_kernel(oa_ref, ob_ref, ga_ref, gb_ref, x_ref, wpa_ref, wpb_ref, wo_ref, y_ref):
    ya = _dot(oa_ref[...].astype(BF16), wpa_ref[...])
    yb = _dot(ob_ref[...].astype(BF16), wpb_ref[...])
    m = jax.nn.sigmoid(ga_ref[...]) * ya + jax.nn.sigmoid(gb_ref[...]) * yb
    y_ref[...] = x_ref[...] + _dot(m.astype(BF16), wo_ref[...])


def _out_proj(oa, ob, h, x2d, w_pa, w_pb, w_o, tm):
    n, d = x2d.shape
    once = pl.Buffered(1)
    return pl.pallas_call(
        _out_kernel,
        grid=(n // tm,),
        in_specs=[pl.BlockSpec((tm, NSA_WIDTH), lambda i: (i, 0)),
                  pl.BlockSpec((tm, GDN_V_WIDTH), lambda i: (i, 0)),
                  pl.BlockSpec((tm, d), lambda i: (i, C_GA // d)),
                  pl.BlockSpec((tm, d), lambda i: (i, C_GB // d)),
                  pl.BlockSpec((tm, d), lambda i: (i, 0)),
                  pl.BlockSpec((NSA_WIDTH, d), lambda i: (0, 0), pipeline_mode=once),
                  pl.BlockSpec((GDN_V_WIDTH, d), lambda i: (0, 0), pipeline_mode=once),
                  pl.BlockSpec((d, d), lambda i: (0, 0), pipeline_mode=once)],
        out_specs=pl.BlockSpec((tm, d), lambda i: (i, 0)),
        out_shape=jax.ShapeDtypeStruct((n, d), F32),
        compiler_params=_cparams(("parallel",)),
        name="out_proj",
    )(oa, ob, h, h, x2d, w_pa, w_pb, w_o)


def _group_slopes(g):
    r_of = lax.broadcasted_iota(jnp.int32, (NSA_GROUP, 1), 0)
    slope = jnp.zeros((NSA_GROUP, 1), F32)
    for r in range(NSA_GROUP):
        s_r = jnp.where(g == 0, 2.0 ** -(r + 1.0), 2.0 ** -(r + 1.0 + NSA_GROUP)).astype(F32)
        slope = jnp.where(r_of == r, s_r, slope)
    return slope


def _attn_sample1_kernel(q_ref, kc_ref, vc_ref, selmap_ref, stw_ref, kvw_ref,
                         ocmp_ref, owin_ref, idx_ref, *, past_len, n_sel, n_sel_pad):
    qpos = past_len
    n_cmp = kc_ref.shape[2]
    wb = stw_ref.shape[1] // KV_ROWS
    rowi = lax.broadcasted_iota(jnp.int32, (LANE, n_sel_pad), 0)
    jrow = lax.broadcasted_iota(jnp.int32, (1, n_sel_pad), 1)
    scores = []
    for g in range(NSA_KV_HEADS):
        slope = _group_slopes(g)
        qg = q_ref[0, g].astype(BF16)
        kend = lax.broadcasted_iota(jnp.int32, (1, n_cmp), 1) * CMP_STRIDE + (CMP_LEN - 1)
        d = qpos - kend
        s = _dot_nt(qg, kc_ref[0, g].astype(BF16)) * SCALE - slope * d.astype(F32)
        p = _masked_softmax(s, d >= 0).astype(BF16)
        ocmp_ref[0, g] = _dot(p, vc_ref[0, g].astype(BF16))
        imp = jnp.sum(_dot(p, selmap_ref[...]), axis=0, keepdims=True)
        visible = jrow * SEL_BLOCK <= qpos
        back = qpos // SEL_BLOCK - jrow
        forced = (jrow == 0) | ((back >= 0) & (back < N_LOCAL_SEL))
        sc = jnp.where(visible, jnp.where(forced, FORCE_SCORE, imp), -FORCE_SCORE)
        scores.append(jnp.where(jrow < n_sel, sc, -3e38))
        kw = stw_ref[0, pl.ds(g, wb, stride=KV_ROWS), :].astype(BF16)
        vw = stw_ref[0, pl.ds(NSA_KV_HEADS + g, wb, stride=KV_ROWS), :].astype(BF16)
        knew = kvw_ref[0, g:g + 1, :].astype(BF16).astype(F32)
        vnew = kvw_ref[0, NSA_KV_HEADS + g:NSA_KV_HEADS + g + 1, :].astype(BF16).astype(F32)
        dwin = wb - lax.broadcasted_iota(jnp.int32, (1, wb), 1)
        sw = _dot_nt(qg, kw) * SCALE - slope * dwin.astype(F32)
        ok = (dwin >= 0) & (dwin <= WINDOW)
        sw = jnp.where(ok, sw, NEG)
        s_new = jnp.sum(qg.astype(F32) * knew, axis=-1, keepdims=True) * SCALE
        m = jnp.maximum(jnp.max(sw, axis=-1, keepdims=True), s_new)
        e = jnp.where(ok, jnp.exp(sw - m), 0.0)
        e_new = jnp.exp(s_new - m)
        den = jnp.sum(e, axis=-1, keepdims=True) + e_new
        owin_ref[0, g] = (_dot((e / den).astype(BF16), vw)
                          + (e_new / den).astype(BF16).astype(F32) * vnew)

    stacked = jnp.zeros((LANE, n_sel_pad), F32)
    for g in range(NSA_KV_HEADS):
        stacked = jnp.where(rowi == g, scores[g], stacked)
    cols = stacked.T
    ii = lax.broadcasted_iota(jnp.int32, (n_sel_pad, n_sel_pad), 0)
    jj = lax.broadcasted_iota(jnp.int32, (n_sel_pad, n_sel_pad), 1)
    earlier = (ii < jj).astype(F32)
    kcol = lax.broadcasted_iota(jnp.int32, (SEL_TOPK, 1), 0).astype(F32)
    for g in range(NSA_KV_HEADS):
        ci = cols[:, g:g + 1]
        rj = scores[g]
        beats = jnp.where(ci > rj, 1.0, jnp.where(ci == rj, earlier, 0.0))
        rank = jnp.sum(beats, axis=0, keepdims=True)
        onehot = rank == kcol
        idx = jnp.sum(jnp.where(onehot, jrow.astype(F32), 0.0), axis=-1, keepdims=True)
        idx_ref[0, g] = idx.astype(jnp.int32)


def _attn_sample1(q_s, kc_s, vc_s, st_win_rows, kvw_new, past_len):
    dbsz = q_s.shape[0]
    n_cmp = kc_s.shape[2]
    n_sel = -(-(past_len + 1) // SEL_BLOCK)
    n_sel_pad = -(-n_sel // LANE) * LANE
    assert n_sel >= SEL_TOPK
    selmap = _sel_map(n_cmp, n_sel, n_sel_pad)
    wrows = st_win_rows.shape[1]
    hshape = jax.ShapeDtypeStruct((dbsz, NSA_KV_HEADS, NSA_GROUP, HEAD_DIM), F32)
    hspec = pl.BlockSpec((1, NSA_KV_HEADS, NSA_GROUP, HEAD_DIM), lambda b: (b, 0, 0, 0))
    cspec = pl.BlockSpec((1, NSA_KV_HEADS, n_cmp, HEAD_DIM), lambda b: (b, 0, 0, 0))
    return pl.pallas_call(
        functools.partial(_attn_sample1_kernel, past_len=past_len, n_sel=n_sel, n_sel_pad=n_sel_pad),
        grid=(dbsz,),
        in_specs=[hspec, cspec, cspec,
                  pl.BlockSpec((n_cmp, n_sel_pad), lambda b: (0, 0)),
                  pl.BlockSpec((1, wrows, HEAD_DIM), lambda b: (b, 0, 0)),
                  pl.BlockSpec((1, 2 * NSA_KV_HEADS, HEAD_DIM), lambda b: (b, 0, 0))],
        out_specs=[hspec, hspec,
                   pl.BlockSpec((1, NSA_KV_HEADS, SEL_TOPK, 1), lambda b: (b, 0, 0, 0))],
        out_shape=[hshape, hshape,
                   jax.ShapeDtypeStruct((dbsz, NSA_KV_HEADS, SEL_TOPK, 1), jnp.int32)],
        compiler_params=_cparams(("parallel",)),
        name="attn_sample_cmp_win",
    )(q_s, kc_s, vc_s, selmap, st_win_rows, kvw_new)


BLOCK_ROWS = SEL_BLOCK * KV_ROWS


def _attn_sample2_kernel(idx_ref, pt_ref, q_ref, cache_ref, kvn_ref, ocmp_ref, owin_ref, gate_ref, za_ref,
                         o_ref, buf_ref, sem_ref, *, past_len, n_pages):
    i = pl.program_id(0)
    n_i = pl.num_programs(0)
    slot = i % 2
    g = i % NSA_KV_HEADS
    qpos = past_len
    n_past_blocks = past_len // SEL_BLOCK
    per_page = PAGE_SIZE // SEL_BLOCK
    n_keys = SEL_TOPK * SEL_BLOCK

    def block_copy(step, k, to_slot):
        blk = jnp.minimum(idx_ref[step * SEL_TOPK + k], n_past_blocks - 1)
        phys = pt_ref[(step // NSA_KV_HEADS) * n_pages + blk // per_page] * per_page + blk % per_page
        return pltpu.make_async_copy(cache_ref.at[pl.ds(phys * BLOCK_ROWS, BLOCK_ROWS)],
                                     buf_ref.at[to_slot, pl.ds(k * BLOCK_ROWS, BLOCK_ROWS)], sem_ref.at[to_slot])

    @pl.when(i == 0)
    def _():
        for k in range(SEL_TOPK):
            block_copy(0, k, 0).start()

    @pl.when(i + 1 < n_i)
    def _():
        for k in range(SEL_TOPK):
            block_copy(i + 1, k, 1 - slot).start()

    for k in range(SEL_TOPK):
        block_copy(i, k, slot).wait()

    slope = _group_slopes(g)
    q = q_ref[0, 0].astype(BF16)
    kk = buf_ref[slot, pl.ds(g, n_keys, stride=KV_ROWS), :].astype(BF16)
    vv = buf_ref[slot, pl.ds(NSA_KV_HEADS + g, n_keys, stride=KV_ROWS), :].astype(BF16)
    lane_blk = lax.broadcasted_iota(jnp.int32, (1, n_keys), 1) // SEL_BLOCK
    tok = lax.broadcasted_iota(jnp.int32, (1, n_keys), 1) % SEL_BLOCK
    for k in range(SEL_TOPK):
        blk = idx_ref[i * SEL_TOPK + k]
        tok = tok + jnp.where(lane_blk == k, jnp.where(blk < n_past_blocks, blk * SEL_BLOCK, qpos + 1), 0)
    dd = qpos - tok
    mask = dd >= 0
    sc = jnp.where(mask, _dot_nt(q, kk) * SCALE - slope * dd.astype(F32), NEG)
    knew = kvn_ref[0, 0, 0].astype(BF16).astype(F32)
    vnew = kvn_ref[0, 1, 0].astype(BF16).astype(F32)
    s_new = jnp.sum(q.astype(F32) * knew, axis=-1, keepdims=True) * SCALE
    m = jnp.maximum(jnp.max(sc, axis=-1, keepdims=True), s_new)
    p = jnp.where(mask, jnp.exp(sc - m), 0.0)
    p_new = jnp.exp(s_new - m)
    den = jnp.sum(p, axis=-1, keepdims=True) + p_new
    o_slc = (_dot(p.astype(BF16), vv) + p_new.astype(BF16).astype(F32) * vnew) / den
    tot = (jax.nn.sigmoid(gate_ref[0, 0, 0]) * ocmp_ref[0, 0]
           + jax.nn.sigmoid(gate_ref[0, 1, 0]) * o_slc
           + jax.nn.sigmoid(gate_ref[0, 2, 0]) * owin_ref[0, 0])
    o_ref[0, 0] = tot * _silu(za_ref[0, 0])


def _attn_sample2(idx, page_table, q_s, cache_rows, kvs_new, o_cmp, o_win, gates, za_s, past_len):
    dbsz, n_pages = page_table.shape
    ng = NSA_KV_HEADS
    head = pl.BlockSpec((1, 1, NSA_GROUP, HEAD_DIM), lambda i, x, p: (i // ng, i % ng, 0, 0))
    grid_spec = pltpu.PrefetchScalarGridSpec(
        num_scalar_prefetch=2,
        grid=(dbsz * ng,),
        in_specs=[head,
                  pl.BlockSpec(memory_space=pl.ANY),
                  pl.BlockSpec((1, 2, 1, 1, HEAD_DIM), lambda i, x, p: (i // ng, 0, i % ng, 0, 0)),
                  head, head,
                  pl.BlockSpec((1, 3, 1, NSA_GROUP, 1), lambda i, x, p: (i // ng, 0, i % ng, 0, 0)),
                  head],
        out_specs=head,
        scratch_shapes=[pltpu.VMEM((2, SEL_TOPK * BLOCK_ROWS, HEAD_DIM), F32),
                        pltpu.SemaphoreType.DMA((2,))],
    )
    return pl.pallas_call(
        functools.partial(_attn_sample2_kernel, past_len=past_len, n_pages=n_pages),
        grid_spec=grid_spec,
        out_shape=jax.ShapeDtypeStruct((dbsz, ng, NSA_GROUP, HEAD_DIM), F32),
        compiler_params=_cparams(("arbitrary",)),
        name="attn_sample_slc",
    )(idx.reshape(-1), page_table.reshape(-1), q_s, cache_rows, kvs_new, o_cmp, o_win, gates, za_s)


def _gdn_sample_kernel(x_ref, cst_ref, cw_ref, a_ref, b_ref, alog_ref, dtb_ref, s_ref, gnw_ref, zb_ref,
                       o_ref, s_out_ref):
    acc = None
    for i in range(CONV_W):
        row = cst_ref[0, i] if i < CONV_W - 1 else x_ref[0]
        term = row * cw_ref[i]
        acc = term if acc is None else acc + term
    c = _silu(acc)
    q = _l2(c[0:GDN_HEADS]) * (GDN_DK ** -0.5)
    k = _l2(c[GDN_HEADS:2 * GDN_HEADS])
    v = c[2 * GDN_HEADS:3 * GDN_HEADS]
    gdec = -jnp.exp(alog_ref[...]) * _softplus(a_ref[0] + dtb_ref[...])
    beta = jax.nn.sigmoid(b_ref[0])
    eg = jnp.exp(gdec)
    qk = jnp.sum(q * k, axis=-1, keepdims=True)
    pad = jnp.concatenate([q, k, jnp.zeros((LANE - 2 * GDN_HEADS, GDN_DK), F32)], axis=0)
    cols = pad.T
    outs = []
    for hh in range(GDN_HEADS):
        s_h = s_ref[0, hh]
        qcol = cols[:, hh:hh + 1]
        kcol = cols[:, GDN_HEADS + hh:GDN_HEADS + hh + 1]
        b_h = beta[hh:hh + 1]
        e_h = eg[hh:hh + 1]
        k_s = jnp.sum(kcol * s_h, axis=0, keepdims=True)
        q_s = jnp.sum(qcol * s_h, axis=0, keepdims=True)
        vn = b_h * v[hh:hh + 1] - (b_h * e_h) * k_s
        outs.append(e_h * q_s + qk[hh:hh + 1] * vn)
        s_out_ref[0, hh] = s_h * e_h + kcol * vn
    o = jnp.concatenate(outs, axis=0)
    o_ref[0] = _rms(o, gnw_ref[...]) * _silu(zb_ref[0])


def _gdn_sample(x_s, cst, cw, a_s, b_s, a_log, dt_bias, s0, gnw, zb_s):
    dbsz = x_s.shape[0]
    nh3 = 3 * GDN_HEADS
    col = pl.BlockSpec((1, GDN_HEADS, 1), lambda b: (b, 0, 0))
    par = pl.BlockSpec((GDN_HEADS, 1), lambda b: (0, 0))
    head = pl.BlockSpec((1, GDN_HEADS, GDN_DV), lambda b: (b, 0, 0))
    sspec = pl.BlockSpec((1, GDN_HEADS, GDN_DK, GDN_DV), lambda b: (b, 0, 0, 0))
    return pl.pallas_call(
        _gdn_sample_kernel,
        grid=(dbsz,),
        in_specs=[pl.BlockSpec((1, nh3, GDN_DK), lambda b: (b, 0, 0)),
                  pl.BlockSpec((1, CONV_W - 1, nh3, GDN_DK), lambda b: (b, 0, 0, 0)),
                  pl.BlockSpec((CONV_W, nh3, GDN_DK), lambda b: (0, 0, 0)),
                  col, col, par, par, sspec,
                  pl.BlockSpec((1, GDN_DV), lambda b: (0, 0)),
                  head],
        out_specs=[head, sspec],
        out_shape=[jax.ShapeDtypeStruct((dbsz, GDN_HEADS, GDN_DV), F32),
                   jax.ShapeDtypeStruct(s0.shape, F32)],
        compiler_params=_cparams(("parallel",)),
        name="gdn_sample",
    )(x_s, cst, cw, a_s, b_s, a_log.reshape(-1, 1), dt_bias.reshape(-1, 1), s0, gnw.reshape(1, -1), zb_s)


def _permute_w_in(w_in):
    splits = (NSA_WIDTH, 2 * NSA_KV_WIDTH, 2 * NSA_KV_WIDTH, 2 * NSA_KV_WIDTH, 3 * NSA_HEADS, NSA_WIDTH,
              GDN_CONV_CH, GDN_HEADS, GDN_HEADS, GDN_V_WIDTH, w_in.shape[0], w_in.shape[0])
    bounds = np.cumsum(np.array(splits))[:-1].tolist()
    (q_a, kvc, kvs, kvw, g_nsa, z_a, qkv_b, a_b, b_b, z_b, gate_a, gate_b) = jnp.split(w_in, bounds, axis=1)
    small_pad = jnp.zeros((w_in.shape[0], H_WIDTH - C_SM - SM_B - GDN_HEADS), w_in.dtype)
    return jnp.concatenate([q_a, z_a, z_b, qkv_b, gate_a, gate_b, kvc, kvs, kvw, g_nsa, a_b, b_b, small_pad],
                           axis=1).astype(BF16)


def _cmp_w1(w1):
    return jnp.transpose(w1, (1, 2, 0, 3)).reshape(CMP_STRIDE, HEAD_DIM, CMP_R * CMP_HIDDEN).astype(BF16)


def _layer(x_p, x_s, cache_cmp, cache_slc, layer, st_win, st_s, st_conv, page_table, w):
    (norm_w, w_in, q_norm_w, k_norm_cmp_w, k_norm_slc_w, k_norm_win_w,
     cmp_w1_k, cmp_b1_k, cmp_w2_k, cmp_w1_v, cmp_b1_v, cmp_w2_v,
     conv_w, a_log, dt_bias, gdn_norm_w, w_pa, w_pb, w_o) = w
    bsz, t, d = x_p.shape
    dbsz, dseq, _ = x_s.shape
    assert dseq == 1 and d == C_GB - C_GA
    n_pages = page_table.shape[1]
    past_len = n_pages * PAGE_SIZE

    w_perm = _permute_w_in(w_in)
    head_w = jnp.stack([q_norm_w, k_norm_slc_w, k_norm_win_w])
    w1k, w1v = _cmp_w1(cmp_w1_k), _cmp_w1(cmp_w1_v)
    w2k, w2v = cmp_w2_k.astype(BF16), cmp_w2_v.astype(BF16)
    wpa, wpb, wo = w_pa.astype(BF16), w_pb.astype(BF16), w_o.astype(BF16)
    kvw_cols = 2 * NSA_KV_WIDTH

    n = bsz * t
    x2 = x_p.reshape(n, d)
    h, kvc_rows, kvs_rows, kvw_rows = _proj(x2, norm_w, w_perm, head_w, tm=min(1024, n))
    pk, pv = _cmp_part_prompt(h, w1k, w1v, bsz, t)
    kc, vc = _cmp_finish(pk, pv, cmp_b1_k, w2k, cmp_b1_v, w2v, k_norm_cmp_w)
    oa = _attn_prompt(h, kc, vc, bsz, t)
    nc = t // GDN_CHUNK
    ab5 = h[:, C_SM + SM_A:C_SM + SM_B + GDN_HEADS].reshape(bsz * nc, GDN_CHUNK, 2, GDN_PAIRS, GDN_PAIR)
    col_form = lambda x: x.transpose(0, 3, 1, 2).reshape(bsz * nc, GDN_PROWS, GDN_PAIRS)
    a_row = ab5[:, :, 0].transpose(0, 2, 3, 1).reshape(bsz * nc, GDN_PAIRS, GDN_PROWS)
    u, wmat, att, qd, kd, edl = _gdn_intra(h, conv_w, col_form(ab5[:, :, 0]), col_form(ab5[:, :, 1]), a_row,
                                           a_log, dt_bias, bsz, t)
    ob, s_p = _gdn_inter(u, wmat, att, qd, kd, edl, h, gdn_norm_w, bsz, t)
    y_p = _out_proj(oa, ob, h, x2, wpa, wpb, wo, tm=min(256, n)).reshape(bsz, t, d)

    kv_shape = (bsz, t, 2, NSA_KV_HEADS, HEAD_DIM)
    cmp_p = kvc_rows.reshape(kv_shape)
    slc_p = kvs_rows.reshape(kv_shape)
    wlen = min(WINDOW, t)
    win_p = kvw_rows.reshape(kv_shape)[:, t - wlen:]
    conv_p = h.reshape(bsz, t, H_WIDTH)[:, t - (CONV_W - 1):, C_QKVB:C_QKVB + GDN_CONV_CH]

    xs2 = x_s.reshape(dbsz, d)
    hs, kvc_s, kvs_s, kvw_s = _proj(xs2, norm_w, w_perm, head_w, tm=dbsz)
    grp = (dbsz, NSA_KV_HEADS, NSA_GROUP, HEAD_DIM)
    q_s = hs[:, C_Q:C_Q + NSA_WIDTH].reshape(grp)
    za_s = hs[:, C_ZA:C_ZA + NSA_WIDTH].reshape(grp)
    pt_phys = page_table + layer * cache_cmp.shape[1]
    pk_s, pv_s = _cmp_part_sample(cache_cmp.reshape(-1, HEAD_DIM), pt_phys, w1k, w1v)
    kc_s, vc_s = _cmp_finish(pk_s, pv_s, cmp_b1_k, w2k, cmp_b1_v, w2v, k_norm_cmp_w)
    wb = st_win.shape[1]
    o_cmp_s, o_win_s, idx = _attn_sample1(q_s, kc_s, vc_s, st_win.reshape(dbsz, wb * KV_ROWS, HEAD_DIM),
                                          kvw_s.reshape(dbsz, 2 * NSA_KV_HEADS, HEAD_DIM), past_len)
    gates_s = hs[:, C_SM:C_SM + 3 * NSA_HEADS].reshape(dbsz, 3, NSA_KV_HEADS, NSA_GROUP, 1)
    oa_s = _attn_sample2(idx, pt_phys, q_s, cache_slc.reshape(-1, HEAD_DIM),
                         kvs_s.reshape(dbsz, 2, NSA_KV_HEADS, 1, HEAD_DIM), o_cmp_s, o_win_s, gates_s, za_s,
                         past_len)
    nh3 = 3 * GDN_HEADS
    ob_s, s_s = _gdn_sample(hs[:, C_QKVB:C_QKVB + GDN_CONV_CH].reshape(dbsz, nh3, GDN_DK),
                            st_conv.reshape(dbsz, CONV_W - 1, nh3, GDN_DK),
                            conv_w.reshape(CONV_W, nh3, GDN_DK),
                            hs[:, C_SM + SM_A:C_SM + SM_A + GDN_HEADS].reshape(dbsz, GDN_HEADS, 1),
                            hs[:, C_SM + SM_B:C_SM + SM_B + GDN_HEADS].reshape(dbsz, GDN_HEADS, 1),
                            a_log, dt_bias, st_s, gdn_norm_w,
                            hs[:, C_ZB:C_ZB + GDN_V_WIDTH].reshape(dbsz, GDN_HEADS, GDN_DV))
    y_s = _out_proj(oa_s.reshape(dbsz, NSA_WIDTH), ob_s.reshape(dbsz, GDN_V_WIDTH), hs, xs2,
                    wpa, wpb, wo, tm=dbsz).reshape(dbsz, 1, d)

    kv1 = (dbsz, 1, 2, NSA_KV_HEADS, HEAD_DIM)
    win_s = jnp.concatenate([st_win, kvw_s.reshape(kv1)], axis=1)[:, -wb:]
    conv_s = jnp.concatenate([st_conv, hs[:, None, C_QKVB:C_QKVB + GDN_CONV_CH]], axis=1)[:, -(CONV_W - 1):]
    return y_p, y_s, (cmp_p, slc_p, win_p, s_p, conv_p,
                      kvc_s.reshape(kv1), kvs_s.reshape(kv1), win_s, s_s, conv_s)


def _take_layer(x, layer):
    return x.reshape(x.shape[1:]) if x.shape[0] == 1 else x[layer]


def kernel(x_prompt, x_sample, cache_cmp_kv, cache_slc_kv, state_win_kv, state_gdn_S, state_gdn_conv, page_table, norm_w, w_in, q_norm_w, k_norm_cmp_w, k_norm_slc_w, k_norm_win_w, cmp_w1_k, cmp_b1_k, cmp_w2_k, cmp_w1_v, cmp_b1_v, cmp_w2_v, conv_w, a_log, dt_bias, gdn_norm_w, w_pa, w_pb, w_o):
    weights = (norm_w, w_in, q_norm_w, k_norm_cmp_w, k_norm_slc_w, k_norm_win_w,
               cmp_w1_k, cmp_b1_k, cmp_w2_k, cmp_w1_v, cmp_b1_v, cmp_w2_v,
               conv_w, a_log, dt_bias, gdn_norm_w, w_pa, w_pb, w_o)
    depth = norm_w.shape[0]
    y_p, y_s = x_prompt, x_sample
    per_layer = []
    for layer in range(depth):
        y_p, y_s, st = _layer(y_p, y_s, cache_cmp_kv, cache_slc_kv, layer, _take_layer(state_win_kv, layer),
                              _take_layer(state_gdn_S, layer), _take_layer(state_gdn_conv, layer), page_table,
                              tuple(_take_layer(wt, layer) for wt in weights))
        per_layer.append(st)
    stacked = [jnp.stack(a) for a in zip(*per_layer)]
    return (y_p, y_s, *stacked)
```

```python
import functools

import numpy as np
import jax
import jax.numpy as jnp
from jax import lax
from jax.experimental import pallas as pl
from jax.experimental.pallas import tpu as pltpu

F32 = jnp.float32
BF16 = jnp.bfloat16
HI = lax.Precision.HIGHEST

PAGE_SIZE = 128
HEAD_DIM = 128
NSA_HEADS = 8
NSA_KV_HEADS = 2
NSA_GROUP = NSA_HEADS // NSA_KV_HEADS
NSA_WIDTH = NSA_HEADS * HEAD_DIM
NSA_KV_WIDTH = NSA_KV_HEADS * HEAD_DIM
CMP_LEN = 32
CMP_STRIDE = 16
CMP_R = CMP_LEN // CMP_STRIDE
CMP_HIDDEN = 128
SEL_BLOCK = 64
SEL_TOPK = 16
N_LOCAL_SEL = 2
WINDOW = 512
Q_BLOCK = 128
FORCE_SCORE = 1e4
ALIBI_MAX = 8.0
GDN_HEADS = 8
GDN_DK = 128
GDN_DV = 128
GDN_QK_WIDTH = GDN_HEADS * GDN_DK
GDN_V_WIDTH = GDN_HEADS * GDN_DV
GDN_CONV_CH = 2 * GDN_QK_WIDTH + GDN_V_WIDTH
CONV_W = 4
GDN_CHUNK = 64
EPS = 1e-6
SCALE = HEAD_DIM ** -0.5
NEG = -1e30

C_Q = 0
C_ZA = 1024
C_ZB = 2048
C_QKVB = 3072
C_GA = 6144
C_GB = 8192
C_KVC = 10240
C_KVS = 10752
C_KVW = 11264
C_SM = 11776
H_WIDTH = 12288
TN = 512
PROJ_TN = 1024
SM_A = 3 * NSA_HEADS
SM_B = SM_A + GDN_HEADS

LANE = 128
KV_ROWS = 2 * NSA_KV_HEADS
PAGE_ROWS = PAGE_SIZE * KV_ROWS
VMEM_LIMIT = 56 * 1024 * 1024


def _cparams(sem, vmem=VMEM_LIMIT):
    return pltpu.CompilerParams(dimension_semantics=sem, vmem_limit_bytes=vmem)


def _silu(x):
    return x * jax.nn.sigmoid(x)


def _rms(a, w):
    return a * lax.rsqrt(jnp.mean(a * a, axis=-1, keepdims=True) + EPS) * w


def _l2(a):
    return a * lax.rsqrt(jnp.sum(a * a, axis=-1, keepdims=True) + EPS)


def _dot(a, b):
    return jnp.dot(a, b, preferred_element_type=F32)


def _dot_nt(a, b):
    return lax.dot_general(a, b, (((1,), (1,)), ((), ())), preferred_element_type=F32)


def _dot_hi(a, b):
    return jnp.dot(a, b, preferred_element_type=F32, precision=HI)


def _softplus(x):
    return jnp.maximum(x, 0.0) + jnp.log1p(jnp.exp(-jnp.abs(x)))


def _masked_softmax(s, mask):
    sm = jnp.where(mask, s, NEG)
    m = jnp.max(sm, axis=-1, keepdims=True)
    e = jnp.where(mask, jnp.exp(sm - m), 0.0)
    den = jnp.sum(e, axis=-1, keepdims=True)
    return e / jnp.where(den > 0, den, 1.0)


def _proj_kernel(x_ref, nw_ref, w_ref, hw_ref, o_ref, kvc_ref, kvs_ref, kvw_ref, xn_ref):
    j = pl.program_id(1)
    tm = x_ref.shape[0]
    kv_refs = (kvc_ref, kvs_ref, kvw_ref)
    kv_dest = _proj_kv_dest()

    @pl.when(j == 0)
    def _():
        x = x_ref[...]
        y = x * lax.rsqrt(jnp.mean(x * x, axis=-1, keepdims=True) + EPS)
        xn_ref[...] = (y * nw_ref[...]).astype(BF16)

    acc = _dot(xn_ref[...], w_ref[...])

    plan = _proj_norm_plan()
    for tile, groups in plan.items():
        @pl.when(j == tile)
        def _(groups=groups):
            parts = []
            for grp in range(PROJ_TN // LANE):
                a = acc[:, grp * LANE:(grp + 1) * LANE]
                parts.append(_rms(a, hw_ref[groups[grp]:groups[grp] + 1, :]) if grp in groups else a)
            o_ref[...] = jnp.concatenate(parts, axis=1)
            for grp, (seg, c) in kv_dest.get(tile, {}).items():
                kv_refs[seg][pl.ds(c, tm, stride=KV_ROWS), :] = parts[grp]

    assert set(kv_dest) <= set(plan)
    plain = functools.reduce(jnp.logical_and, [j != tile for tile in plan])

    @pl.when(plain)
    def _():
        o_ref[...] = acc


def _proj_kv_dest():
    dest = {}
    for seg, c0 in enumerate((C_KVC, C_KVS, C_KVW)):
        for c in range(KV_ROWS):
            col = c0 + c * LANE
            dest.setdefault(col // PROJ_TN, {})[(col % PROJ_TN) // LANE] = (seg, c)
    return dest


def _proj_norm_plan():
    plan = {}
    for c0, width, row in ((C_Q, NSA_WIDTH, 0), (C_KVS, NSA_KV_WIDTH, 1), (C_KVW, NSA_KV_WIDTH, 2)):
        for grp in range(width // LANE):
            col = c0 + grp * LANE
            plan.setdefault(col // PROJ_TN, {})[(col % PROJ_TN) // LANE] = row
    return plan


def _proj(x2d, norm_w, w_perm, head_w, tm):
    n, d = x2d.shape
    return pl.pallas_call(
        _proj_kernel,
        grid=(n // tm, H_WIDTH // PROJ_TN),
        in_specs=[pl.BlockSpec((tm, d), lambda i, j: (i, 0)),
                  pl.BlockSpec((1, d), lambda i, j: (0, 0)),
                  pl.BlockSpec((d, PROJ_TN), lambda i, j: (0, j)),
                  pl.BlockSpec((3, LANE), lambda i, j: (0, 0))],
        out_specs=[pl.BlockSpec((tm, PROJ_TN), lambda i, j: (i, j))]
        + [pl.BlockSpec((tm * KV_ROWS, LANE), lambda i, j: (i, 0))] * 3,
        out_shape=[jax.ShapeDtypeStruct((n, H_WIDTH), F32)]
        + [jax.ShapeDtypeStruct((n * KV_ROWS, LANE), F32)] * 3,
        scratch_shapes=[pltpu.VMEM((tm, d), BF16)],
        compiler_params=_cparams(("parallel", "arbitrary")),
        name="in_proj",
    )(x2d, norm_w.reshape(1, d), w_perm, head_w)


def _cmp_partials_body(strided_rows, w1k_ref, w1v_ref, pk_ref, pv_ref, n_chunks):
    for g in range(NSA_KV_HEADS):
        for kv, (w1_ref, p_ref) in enumerate(((w1k_ref, pk_ref), (w1v_ref, pv_ref))):
            acc = jnp.zeros((n_chunks, CMP_R * CMP_HIDDEN), F32)
            for s in range(CMP_STRIDE):
                xs = strided_rows(kv * NSA_KV_HEADS + g, s)
                acc = acc + _dot(xs.astype(BF16), w1_ref[s])
            p_ref[0, g] = acc


def _cmp_part_prompt_kernel(r0_ref, r1_ref, r2_ref, r3_ref, w1k_ref, w1v_ref, pk_ref, pv_ref, *, n_chunks):
    refs = (r0_ref, r1_ref, r2_ref, r3_ref)

    def strided_rows(c, s):
        return refs[c][pl.ds(s, n_chunks, stride=CMP_STRIDE), :]

    _cmp_partials_body(strided_rows, w1k_ref, w1v_ref, pk_ref, pv_ref, n_chunks)


def _cmp_part_prompt(h, w1k, w1v, bsz, t):
    n_chunks = t // CMP_STRIDE
    pshape = jax.ShapeDtypeStruct((bsz, NSA_KV_HEADS, n_chunks, CMP_R * CMP_HIDDEN), F32)
    pspec = pl.BlockSpec((1, NSA_KV_HEADS, n_chunks, CMP_R * CMP_HIDDEN), lambda b: (b, 0, 0, 0))
    wspec = pl.BlockSpec((CMP_STRIDE, HEAD_DIM, CMP_R * CMP_HIDDEN), lambda b: (0, 0, 0))
    return pl.pallas_call(
        functools.partial(_cmp_part_prompt_kernel, n_chunks=n_chunks),
        grid=(bsz,),
        in_specs=[pl.BlockSpec((t, HEAD_DIM), functools.partial(lambda c, b: (b, C_KVC // HEAD_DIM + c), c))
                  for c in range(2 * NSA_KV_HEADS)] + [wspec, wspec],
        out_specs=[pspec, pspec],
        out_shape=[pshape, pshape],
        compiler_params=_cparams(("parallel",)),
        name="cmp_part_prompt",
    )(h, h, h, h, w1k, w1v)


def _cmp_part_sample_kernel(pt_ref, cache_ref, w1k_ref, w1v_ref, pk_ref, pv_ref, buf_ref, tok_ref, sem_ref,
                            *, pages_per_step):
    i = pl.program_id(0)
    n_i = pl.num_programs(0)
    slot = i % 2
    n_chunks = pages_per_step * PAGE_SIZE // CMP_STRIDE

    def page_copy(step, p, to_slot):
        page = pt_ref[step * pages_per_step + p]
        return pltpu.make_async_copy(cache_ref.at[pl.ds(page * PAGE_ROWS, PAGE_ROWS)],
                                     buf_ref.at[to_slot, pl.ds(p * PAGE_ROWS, PAGE_ROWS)], sem_ref.at[to_slot])

    @pl.when(i == 0)
    def _():
        for p in range(pages_per_step):
            page_copy(0, p, 0).start()

    @pl.when(i + 1 < n_i)
    def _():
        for p in range(pages_per_step):
            page_copy(i + 1, p, 1 - slot).start()

    for p in range(pages_per_step):
        page_copy(i, p, slot).wait()

    n_tok = pages_per_step * PAGE_SIZE
    for c in range(KV_ROWS):
        tok_ref[c] = buf_ref[slot, pl.ds(c, n_tok, stride=KV_ROWS), :]
    for kv, (w1_ref, p_ref) in enumerate(((w1k_ref, pk_ref), (w1v_ref, pv_ref))):
        xs = jnp.concatenate(
            [jnp.concatenate([tok_ref[kv * NSA_KV_HEADS + g, pl.ds(s, n_chunks, stride=CMP_STRIDE), :]
                              for g in range(NSA_KV_HEADS)], axis=0).astype(BF16)
             for s in range(CMP_STRIDE)], axis=1)
        acc = _dot(xs, w1_ref[...])
        for g in range(NSA_KV_HEADS):
            p_ref[0, g] = acc[g * n_chunks:(g + 1) * n_chunks]


def _cmp_part_sample(cache_rows, page_table, w1k, w1v):
    dbsz, n_pages = page_table.shape
    pages_per_step = min(32, n_pages)
    n_steps = n_pages // pages_per_step
    cps = pages_per_step * PAGE_SIZE // CMP_STRIDE
    n_chunks = n_pages * PAGE_SIZE // CMP_STRIDE
    pshape = jax.ShapeDtypeStruct((dbsz, NSA_KV_HEADS, n_chunks, CMP_R * CMP_HIDDEN), F32)
    pspec = pl.BlockSpec((1, NSA_KV_HEADS, cps, CMP_R * CMP_HIDDEN),
                         lambda i, pt: (i // n_steps, 0, i % n_steps, 0))
    wspec = pl.BlockSpec((CMP_STRIDE * HEAD_DIM, CMP_R * CMP_HIDDEN), lambda i, pt: (0, 0))
    w1k, w1v = (w.reshape(CMP_STRIDE * HEAD_DIM, CMP_R * CMP_HIDDEN) for w in (w1k, w1v))
    grid_spec = pltpu.PrefetchScalarGridSpec(
        num_scalar_prefetch=1,
        grid=(dbsz * n_steps,),
        in_specs=[pl.BlockSpec(memory_space=pl.ANY), wspec, wspec],
        out_specs=[pspec, pspec],
        scratch_shapes=[pltpu.VMEM((2, pages_per_step * PAGE_ROWS, HEAD_DIM), F32),
                        pltpu.VMEM((KV_ROWS, pages_per_step * PAGE_SIZE, HEAD_DIM), F32),
                        pltpu.SemaphoreType.DMA((2,))],
    )
    return pl.pallas_call(
        functools.partial(_cmp_part_sample_kernel, pages_per_step=pages_per_step),
        grid_spec=grid_spec,
        out_shape=[pshape, pshape],
        compiler_params=_cparams(("arbitrary",)),
        name="cmp_part_sample",
    )(page_table.reshape(-1), cache_rows, w1k, w1v)


def _cmp_finish_kernel(pk_ref, pv_ref, b1k_ref, w2k_ref, b1v_ref, w2v_ref, nk_ref, kc_ref, vc_ref,
                       *, n_chunks):
    for g in range(NSA_KV_HEADS):
        for kv, (p_ref, b1_ref, w2_ref, o_ref) in enumerate(((pk_ref, b1k_ref, w2k_ref, kc_ref),
                                                            (pv_ref, b1v_ref, w2v_ref, vc_ref))):
            part = p_ref[0, g]
            nxt = pltpu.roll(part[:, CMP_HIDDEN:], n_chunks - 1, 0)
            hid = _silu(part[:, :CMP_HIDDEN] + nxt + b1_ref[...])
            out = _dot(hid.astype(BF16), w2_ref[...])
            if kv == 0:
                out = _rms(out, nk_ref[...])
            o_ref[0, g] = out


def _cmp_finish(pk, pv, b1k, w2k, b1v, w2v, nk):
    bsz, _, n_chunks, _ = pk.shape
    pspec = pl.BlockSpec((1, NSA_KV_HEADS, n_chunks, CMP_R * CMP_HIDDEN), lambda b: (b, 0, 0, 0))
    vspec = pl.BlockSpec((1, CMP_HIDDEN), lambda b: (0, 0))
    wspec = pl.BlockSpec((CMP_HIDDEN, HEAD_DIM), lambda b: (0, 0))
    oshape = jax.ShapeDtypeStruct((bsz, NSA_KV_HEADS, n_chunks, HEAD_DIM), F32)
    ospec = pl.BlockSpec((1, NSA_KV_HEADS, n_chunks, HEAD_DIM), lambda b: (b, 0, 0, 0))
    return pl.pallas_call(
        functools.partial(_cmp_finish_kernel, n_chunks=n_chunks),
        grid=(bsz,),
        in_specs=[pspec, pspec, vspec, wspec, vspec, wspec, vspec],
        out_specs=[ospec, ospec],
        out_shape=[oshape, oshape],
        compiler_params=_cparams(("parallel",)),
        name="cmp_finish",
    )(pk, pv, b1k.reshape(1, -1), w2k, b1v.reshape(1, -1), w2v, nk.reshape(1, -1))


def _sel_map(n_cmp, n_sel, n_sel_pad):
    cs = np.arange(n_cmp) * CMP_STRIDE
    ss = np.arange(n_sel) * SEL_BLOCK
    ov = np.minimum(cs[:, None] + CMP_LEN, ss[None] + SEL_BLOCK) - np.maximum(cs[:, None], ss[None])
    m = np.zeros((n_cmp, n_sel_pad), np.float32)
    m[:, :n_sel] = np.clip(ov, 0, None).astype(np.float32) / CMP_LEN
    return jnp.asarray(m, dtype=BF16)


def _topk_rank_rows(score, n_valid):
    j = lax.broadcasted_iota(jnp.int32, score.shape, 1)
    cnt = jnp.zeros(score.shape, F32)
    for jp in range(n_valid):
        col = score[:, jp:jp + 1]
        earlier = (j > jp).astype(F32)
        cnt = cnt + jnp.where(col > score, 1.0, jnp.where(col == score, earlier, 0.0))
    return cnt


QT = 256
MASK_BIG = 16384.0


def _attn_prompt_kernel(q_ref, ks_ref, vs_ref, kw_ref, vw_ref, kc_ref, vc_ref, sm_ref, za_ref,
                        selmap_ref, kaug_ref, caug_ref, padaug_ref, band_ref, o_ref,
                        ksa_ref, vsb_ref, kwa_ref, vwb_ref, kca_ref, vcb_ref, *, t_len):
    g = pl.program_id(1)
    qb = pl.program_id(2)
    qstart = pl.multiple_of(qb * QT, QT)
    n_sel = t_len // SEL_BLOCK
    rows = NSA_GROUP * QT

    @pl.when(qb == 0)
    def _():
        ksa_ref[:, 0:HEAD_DIM] = ks_ref[...].astype(BF16)
        ksa_ref[:, HEAD_DIM:] = kaug_ref[...]
        vsb_ref[...] = vs_ref[...].astype(BF16)
        kwa_ref[0:WINDOW, 0:HEAD_DIM] = jnp.zeros((WINDOW, HEAD_DIM), BF16)
        kwa_ref[0:WINDOW, HEAD_DIM:] = padaug_ref[...]
        kwa_ref[WINDOW:, 0:HEAD_DIM] = kw_ref[...].astype(BF16)
        kwa_ref[WINDOW:, HEAD_DIM:] = kaug_ref[...]
        vwb_ref[0:WINDOW, :] = jnp.zeros((WINDOW, HEAD_DIM), BF16)
        vwb_ref[WINDOW:, :] = vw_ref[...].astype(BF16)
        kca_ref[:, 0:HEAD_DIM] = kc_ref[0, 0].astype(BF16)
        kca_ref[:, HEAD_DIM:] = caug_ref[...]
        vcb_ref[...] = vc_ref[0, 0].astype(BF16)

    q = q_ref[...]
    qsb = (jnp.concatenate([q[:, r * HEAD_DIM:(r + 1) * HEAD_DIM] for r in range(NSA_GROUP)], axis=0)
           * SCALE).astype(BF16)
    row = lax.broadcasted_iota(jnp.int32, (rows, 1), 0)
    r_of = row // QT
    tloc = row % QT
    qpos = qstart + tloc
    slope = jnp.zeros((rows, 1), F32)
    for r in range(NSA_GROUP):
        s_r = jnp.where(g == 0, 2.0 ** -(r + 1.0), 2.0 ** -(r + 1.0 + NSA_GROUP)).astype(F32)
        slope = jnp.where(r_of == r, s_r, slope)
    lane = lax.broadcasted_iota(jnp.int32, (rows, LANE), 1)
    pos_cols = jnp.where(lane == n_sel, slope * 256.0, jnp.where(lane == n_sel + 1, slope, 0.0))
    q_w = jnp.concatenate([qsb, jnp.where(lane == n_sel + 2, -MASK_BIG, pos_cols).astype(BF16)], axis=1)

    n_cmp = kca_ref.shape[0]
    kend = lax.broadcasted_iota(jnp.int32, (1, n_cmp), 1) * CMP_STRIDE + (CMP_LEN - 1)
    p_cmp = _masked_softmax(_dot_nt(q_w, kca_ref[...]), (qpos - kend) >= 0).astype(BF16)
    o_cmp = _dot(p_cmp, vcb_ref[...])
    imp4 = _dot(p_cmp, selmap_ref[...])
    imp = imp4[0:QT]
    for r in range(1, NSA_GROUP):
        imp = imp + imp4[r * QT:(r + 1) * QT]

    n_sel_rows = -(-n_sel // SUBLANES) * SUBLANES
    imp_t = imp.T[0:n_sel_rows]
    tq = qstart + lax.broadcasted_iota(jnp.int32, (n_sel_rows, QT), 1)
    j = lax.broadcasted_iota(jnp.int32, (n_sel_rows, QT), 0)
    visible = (j * SEL_BLOCK <= tq) & (j < n_sel)
    back = tq // SEL_BLOCK - j
    forced = (j == 0) | ((back >= 0) & (back < N_LOCAL_SEL))
    score = jnp.where(visible, jnp.where(forced, FORCE_SCORE, imp_t), -FORCE_SCORE)
    score = jnp.where(j < n_sel, score, -3e38)
    rank = jnp.zeros((n_sel_rows, QT), F32)
    for jp in range(n_sel):
        other = score[jp:jp + 1, :]
        rank = rank + jnp.where(other > score, 1.0, jnp.where(other == score, (j > jp).astype(F32), 0.0))
    sel_t = jnp.where((rank < min(SEL_TOPK, n_sel)) & visible, 1.0, 0.0)
    sel = jnp.concatenate([sel_t, jnp.zeros((LANE - n_sel_rows, QT), F32)], axis=0).T
    sel4 = jnp.concatenate([sel] * NSA_GROUP, axis=0)
    q_s = jnp.concatenate([qsb, jnp.where(lane < n_sel, (sel4 - 1.0) * MASK_BIG, pos_cols).astype(BF16)],
                          axis=1)

    def online(sc, k0, carry):
        m, l, acc = carry
        m_new = jnp.maximum(m, jnp.max(sc, axis=-1, keepdims=True))
        alpha = jnp.exp(m - m_new)
        p = jnp.exp(sc - m_new)
        l = alpha * l + jnp.sum(p, axis=-1, keepdims=True)
        acc = alpha * acc + _dot(p.astype(BF16), vsb_ref[pl.ds(k0, QT), :])
        return m_new, l, acc

    def past_chunk(c, carry):
        k0 = pl.multiple_of(c * QT, QT)
        return online(_dot_nt(q_s, ksa_ref[pl.ds(k0, QT), :]), k0, carry)

    init = (jnp.full((rows, 1), NEG, F32), jnp.zeros((rows, 1), F32), jnp.zeros((rows, HEAD_DIM), F32))
    carry = lax.fori_loop(0, qb, past_chunk, init)
    ki = lax.broadcasted_iota(jnp.int32, (1, QT), 1)
    diag = jnp.where(tloc >= ki, _dot_nt(q_s, ksa_ref[pl.ds(qstart, QT), :]), -MASK_BIG)
    _, l_s, acc_s = online(diag, qstart, carry)
    o_slc = acc_s / l_s

    span = QT + WINDOW
    band = band_ref[...]
    sw = _dot_nt(q_w, kwa_ref[pl.ds(qstart, span), :]) + jnp.concatenate([band] * NSA_GROUP, axis=0)
    e_win = jnp.exp(sw - jnp.max(sw, axis=-1, keepdims=True))
    o_win = (_dot(e_win.astype(BF16), vwb_ref[pl.ds(qstart, span), :])
             / jnp.sum(e_win, axis=-1, keepdims=True))

    sm = sm_ref[...]
    outs = []
    for r in range(NSA_GROUP):
        tot = jnp.zeros((QT, HEAD_DIM), F32)
        for br, o_br in enumerate((o_cmp, o_slc, o_win)):
            c0 = br * NSA_HEADS + r
            gate = jnp.where(g == 0, sm[:, c0:c0 + 1], sm[:, c0 + NSA_GROUP:c0 + NSA_GROUP + 1])
            tot = tot + jax.nn.sigmoid(gate) * o_br[r * QT:(r + 1) * QT]
        outs.append(tot * _silu(za_ref[:, r * HEAD_DIM:(r + 1) * HEAD_DIM]))
    o_ref[...] = jnp.concatenate(outs, axis=1)


def _attn_prompt(h, kc, vc, bsz, t):
    nqb = t // QT
    n_cmp = t // CMP_STRIDE
    n_sel = t // SEL_BLOCK
    span = QT + WINDOW
    assert n_sel + 3 <= LANE and t % QT == 0 and t + CMP_LEN < 256 * 256
    selmap = _sel_map(n_cmp, n_sel, LANE)

    def aug_cols(pos, onehot_block):
        a = np.zeros((pos.shape[0], LANE), np.float32)
        if onehot_block:
            a[np.arange(pos.shape[0]), pos // SEL_BLOCK] = 1.0
        a[:, n_sel] = pos // 256
        a[:, n_sel + 1] = pos % 256
        return jnp.asarray(a, dtype=BF16)

    kaug = aug_cols(np.arange(t), True)
    caug = aug_cols(np.arange(n_cmp) * CMP_STRIDE + CMP_LEN - 1, False)
    pad = np.zeros((WINDOW, LANE), np.float32)
    pad[:, n_sel + 2] = 1.0
    padaug = jnp.asarray(pad, dtype=BF16)
    ti = np.arange(QT)[:, None]
    ii = np.arange(span)[None]
    band = jnp.asarray(np.where((ii >= ti) & (ii <= ti + WINDOW), 0.0, -MASK_BIG).astype(np.float32))
    gw = NSA_GROUP * HEAD_DIM
    row_blk = lambda b, g, i: b * nqb + i
    kv_spec = lambda c0: pl.BlockSpec((t, HEAD_DIM), lambda b, g, i: (b, c0 // HEAD_DIM + g))
    const = lambda shape: pl.BlockSpec(shape, lambda b, g, i: (0,) * len(shape))
    return pl.pallas_call(
        functools.partial(_attn_prompt_kernel, t_len=t),
        grid=(bsz, NSA_KV_HEADS, nqb),
        in_specs=[pl.BlockSpec((QT, gw), lambda b, g, i: (row_blk(b, g, i), C_Q // gw + g)),
                  kv_spec(C_KVS), kv_spec(C_KVS + NSA_KV_WIDTH),
                  kv_spec(C_KVW), kv_spec(C_KVW + NSA_KV_WIDTH),
                  pl.BlockSpec((1, 1, n_cmp, HEAD_DIM), lambda b, g, i: (b, g, 0, 0)),
                  pl.BlockSpec((1, 1, n_cmp, HEAD_DIM), lambda b, g, i: (b, g, 0, 0)),
                  pl.BlockSpec((QT, TN), lambda b, g, i: (row_blk(b, g, i), C_SM // TN)),
                  pl.BlockSpec((QT, gw), lambda b, g, i: (row_blk(b, g, i), C_ZA // gw + g)),
                  const((n_cmp, LANE)), const((t, LANE)), const((n_cmp, LANE)), const((WINDOW, LANE)),
                  const((QT, span))],
        out_specs=pl.BlockSpec((QT, gw), lambda b, g, i: (row_blk(b, g, i), g)),
        out_shape=jax.ShapeDtypeStruct((bsz * t, NSA_WIDTH), F32),
        scratch_shapes=[pltpu.VMEM((t, 2 * HEAD_DIM), BF16), pltpu.VMEM((t, HEAD_DIM), BF16),
                        pltpu.VMEM((WINDOW + t, 2 * HEAD_DIM), BF16), pltpu.VMEM((WINDOW + t, HEAD_DIM), BF16),
                        pltpu.VMEM((n_cmp, 2 * HEAD_DIM), BF16), pltpu.VMEM((n_cmp, HEAD_DIM), BF16)],
        compiler_params=_cparams(("parallel", "parallel", "arbitrary")),
        name="attn_prompt",
    )(h, h, h, h, h, kc, vc, h, h, selmap, kaug, caug, padaug, band)


SUBLANES = 8
GDN_PAIR = 2
GDN_PAIRS = GDN_HEADS // GDN_PAIR
GDN_PROWS = GDN_PAIR * GDN_CHUNK


def _split2(x):
    hi = x.astype(BF16)
    return hi, (x - hi.astype(F32)).astype(BF16)


def _split3(x):
    hi = x.astype(BF16)
    r = x - hi.astype(F32)
    mid = r.astype(BF16)
    return hi, mid, (r - mid.astype(F32)).astype(BF16)


def _dot_x3(a_parts, b_parts):
    a_hi, a_lo = a_parts
    b_hi, b_lo = b_parts
    return _dot(jnp.concatenate([a_hi, a_lo, a_hi], axis=1), jnp.concatenate([b_hi, b_hi, b_lo], axis=0))


def _gdn_intra_kernel(x_ref, prev_ref, cw_ref, acol_ref, bcol_ref, arow_ref, alogc_ref, dtbc_ref, alogr_ref,
                      dtbr_ref, u_ref, w_ref, att_ref, qd_ref, kd_ref, edl_ref, *, chunks_per_seq):
    pr = GDN_PROWS
    first = pl.program_id(0) % chunks_per_seq == 0
    full = jnp.concatenate([jnp.where(first, 0.0, prev_ref[...]), x_ref[...]], axis=0)
    conv = None
    for i in range(CONV_W):
        dly = CONV_W - 1 - i
        xs = full if dly == 0 else pltpu.roll(full, dly, 0)
        term = xs[SUBLANES:] * cw_ref[i:i + 1, :]
        conv = term if conv is None else conv + term
    conv = _silu(conv)

    def head_cols(kind, hh):
        c0 = kind * GDN_QK_WIDTH + hh * GDN_DK
        return conv[:, c0:c0 + GDN_DK]

    def stack(kind, p):
        parts = [head_cols(kind, GDN_PAIR * p + i) for i in range(GDN_PAIR)]
        if kind == 0:
            parts = [_l2(x) * (GDN_DK ** -0.5) for x in parts]
        elif kind == 1:
            parts = [_l2(x) for x in parts]
        return jnp.concatenate(parts, axis=0)

    ri = lax.broadcasted_iota(jnp.int32, (pr, pr), 0)
    ci = lax.broadcasted_iota(jnp.int32, (pr, pr), 1)
    same = (ri // GDN_CHUNK) == (ci // GDN_CHUNK)
    tril = same & (ri >= ci)
    stril = same & (ri > ci)
    eye = (ri == ci).astype(F32)
    lo_b = tril.astype(BF16)
    ut_b = (same & (ri <= ci)).astype(BF16)
    rcol = lax.broadcasted_iota(jnp.int32, (pr, 1), 0)

    gcol = -jnp.exp(alogc_ref[...]) * _softplus(acol_ref[0] + dtbc_ref[...])
    beta = jax.nn.sigmoid(bcol_ref[0])
    grow = -jnp.exp(alogr_ref[...]) * _softplus(arow_ref[0] + dtbr_ref[...])
    dec_col = sum(_dot(lo_b, part) for part in _split3(gcol))
    dec_row = sum(_dot(part, ut_b) for part in _split3(grow))

    ks, kbs, kbfs, lmats, vbs, qs, dcs = [], [], [], [], [], [], []
    for p in range(GDN_PAIRS):
        k2 = stack(1, p)
        dc = dec_col[:, p:p + 1]
        b2 = beta[:, p:p + 1]
        dmat = jnp.where(tril, dc - dec_row[p:p + 1, :], 0.0)
        ks.append(k2)
        kbs.append(k2 * b2)
        kbfs.append(k2.astype(BF16))
        lmats.append(jnp.where(tril, jnp.exp(dmat), 0.0))
        vbs.append(stack(2, p) * b2)
        qs.append(stack(0, p))
        dcs.append(dc)

    npows = [-jnp.where(stril, _dot_nt(kbs[p].astype(BF16), kbfs[p]) * lmats[p], 0.0) for p in range(GDN_PAIRS)]
    tms = [eye + n for n in npows]
    parts = [_split2(n) for n in npows]
    npows = [_dot_x3(parts[p], parts[p]) for p in range(GDN_PAIRS)]
    for _ in range(4):
        for p in range(GDN_PAIRS):
            n_hi, n_lo = _split2(npows[p])
            s_hi, s_lo = _split2(tms[p])
            both = _dot_x3((n_hi, n_lo), (jnp.concatenate([n_hi, s_hi], axis=1),
                                          jnp.concatenate([n_lo, s_lo], axis=1)))
            npows[p] = both[:, :pr]
            tms[p] = tms[p] + both[:, pr:]
    tms = [tms[p] + _dot_x3(_split2(npows[p]), _split2(tms[p])) for p in range(GDN_PAIRS)]

    edl_rows = []
    for p in range(GDN_PAIRS):
        tm_b = tms[p].astype(BF16)
        dc = dcs[p]
        edec = jnp.exp(dc)
        edl_rows += [jnp.broadcast_to(jnp.exp(dc[(i + 1) * GDN_CHUNK - 1:(i + 1) * GDN_CHUNK, :]), (1, LANE))
                     for i in range(GDN_PAIR)]
        dl = jnp.where(rcol < GDN_CHUNK, dc[GDN_CHUNK - 1:GDN_CHUNK, :], dc[pr - 1:pr, :])
        uw = _dot(tm_b, jnp.concatenate([vbs[p].astype(BF16), (kbs[p] * edec).astype(BF16)], axis=1))
        u_ref[0, p] = uw[:, :GDN_DV]
        w_ref[0, p] = uw[:, GDN_DV:].astype(BF16)
        att_ref[0, p] = jnp.where(tril, _dot_nt(qs[p].astype(BF16), kbfs[p]) * lmats[p], 0.0).astype(BF16)
        qd_ref[0, p] = (qs[p] * edec).astype(BF16)
        kd_ref[0, p] = (ks[p] * jnp.exp(dl - dc)).astype(BF16)
    edl_ref[0] = jnp.concatenate(edl_rows, axis=0)


def _gdn_intra(h, conv_w, a_col, b_col, a_row, a_log, dt_bias, bsz, t):
    nc = t // GDN_CHUNK
    n_chunks = bsz * nc
    pr = GDN_PROWS
    per_chunk = GDN_CHUNK // SUBLANES
    alog_c = jnp.tile(a_log.reshape(GDN_PAIRS, GDN_PAIR).T[:, None, :], (1, GDN_CHUNK, 1)).reshape(pr, GDN_PAIRS)
    dtb_c = jnp.tile(dt_bias.reshape(GDN_PAIRS, GDN_PAIR).T[:, None, :], (1, GDN_CHUNK, 1)).reshape(pr, GDN_PAIRS)
    alog_r = jnp.repeat(a_log.reshape(GDN_PAIRS, GDN_PAIR), GDN_CHUNK, axis=1)
    dtb_r = jnp.repeat(dt_bias.reshape(GDN_PAIRS, GDN_PAIR), GDN_CHUNK, axis=1)
    cspec = pl.BlockSpec((1, pr, GDN_PAIRS), lambda i: (i, 0, 0))
    cpar = pl.BlockSpec((pr, GDN_PAIRS), lambda i: (0, 0))
    rpar = pl.BlockSpec((GDN_PAIRS, pr), lambda i: (0, 0))
    mspec = pl.BlockSpec((1, GDN_PAIRS, pr, GDN_DK), lambda i: (i, 0, 0, 0))
    mshape = lambda dt: jax.ShapeDtypeStruct((n_chunks, GDN_PAIRS, pr, GDN_DK), dt)
    qkv_blk = C_QKVB // GDN_CONV_CH
    return pl.pallas_call(
        functools.partial(_gdn_intra_kernel, chunks_per_seq=nc),
        grid=(n_chunks,),
        in_specs=[pl.BlockSpec((GDN_CHUNK, GDN_CONV_CH), lambda i: (i, qkv_blk)),
                  pl.BlockSpec((SUBLANES, GDN_CONV_CH), lambda i: (jnp.maximum(i * per_chunk - 1, 0), qkv_blk)),
                  pl.BlockSpec((CONV_W, GDN_CONV_CH), lambda i: (0, 0)),
                  cspec, cspec,
                  pl.BlockSpec((1, GDN_PAIRS, pr), lambda i: (i, 0, 0)),
                  cpar, cpar, rpar, rpar],
        out_specs=[mspec] * 5 + [pl.BlockSpec((1, GDN_HEADS, LANE), lambda i: (i, 0, 0))],
        out_shape=[mshape(F32), mshape(BF16), mshape(BF16), mshape(BF16), mshape(BF16),
                   jax.ShapeDtypeStruct((n_chunks, GDN_HEADS, LANE), F32)],
        compiler_params=_cparams(("parallel",)),
        name="gdn_intra",
    )(h, h, conv_w, a_col, b_col, a_row, alog_c, dtb_c, alog_r, dtb_r)


def _gdn_inter_kernel(u_ref, w_ref, att_ref, qd_ref, kd_ref, edl_ref, zb_ref, gnw_ref,
                      o_ref, s_out_ref, s_ref):
    c = pl.program_id(1)
    n_c = pl.num_programs(1)
    cs = GDN_CHUNK

    @pl.when(c == 0)
    def _():
        s_ref[...] = jnp.zeros_like(s_ref)

    n_seq = u_ref.shape[1]
    heads = [(sq, p, i, GDN_PAIR * p + i, slice(i * cs, (i + 1) * cs))
             for sq in range(n_seq) for p in range(GDN_PAIRS) for i in range(GDN_PAIR)]
    s_old = {(sq, hh): s_ref[sq, hh] for (sq, _, _, hh, _) in heads}
    s_bf = {key: s.astype(BF16) for key, s in s_old.items()}
    ws = {(sq, hh): _dot(w_ref[0, sq, 0, p, rows, :], s_bf[sq, hh]) for (sq, p, _, hh, rows) in heads}
    qss = {(sq, hh): _dot(qd_ref[0, sq, 0, p, rows, :], s_bf[sq, hh]) for (sq, p, _, hh, rows) in heads}
    vn_b, o2 = {}, {}
    for sq in range(n_seq):
        for p in range(GDN_PAIRS):
            hs = [GDN_PAIR * p + i for i in range(GDN_PAIR)]
            vn = u_ref[0, sq, 0, p] - jnp.concatenate([ws[sq, hh] for hh in hs], axis=0)
            vn_b[sq, p] = vn.astype(BF16)
            o2[sq, p] = (jnp.concatenate([qss[sq, hh] for hh in hs], axis=0)
                         + _dot(att_ref[0, sq, 0, p], vn_b[sq, p]))
    outs = {sq: [] for sq in range(n_seq)}
    for (sq, p, i, hh, rows) in heads:
        edl = edl_ref[0, sq, 0, hh:hh + 1, :]
        upd = lax.dot_general(kd_ref[0, sq, 0, p, rows, :], vn_b[sq, p][rows], (((0,), (0,)), ((), ())),
                              preferred_element_type=F32)
        s_ref[sq, hh] = s_old[sq, hh] * edl + upd
        outs[sq].append(_rms(o2[sq, p][rows], gnw_ref[...])
                        * _silu(zb_ref[0, sq, :, hh * GDN_DV:(hh + 1) * GDN_DV]))
    for sq in range(n_seq):
        o_ref[0, sq] = jnp.concatenate(outs[sq], axis=1)

    @pl.when(c == n_c - 1)
    def _():
        s_out_ref[0] = s_ref[...]


def _gdn_inter(u, w, att, qd, kd, edl, h, gnw, bsz, t):
    nc = t // GDN_CHUNK
    wd = GDN_V_WIDTH
    pr = GDN_PROWS
    n_seq = next(k for k in (4, 2, 1) if bsz % k == 0)
    nb = bsz // n_seq
    by_seq = lambda x: x.reshape((nb, n_seq, nc) + x.shape[1:])
    mspec = pl.BlockSpec((1, n_seq, 1, GDN_PAIRS, pr, GDN_DK), lambda b, c: (b, 0, c, 0, 0, 0))
    ob, s_fin = pl.pallas_call(
        _gdn_inter_kernel,
        grid=(nb, nc),
        in_specs=[mspec] * 5 + [pl.BlockSpec((1, n_seq, 1, GDN_HEADS, LANE), lambda b, c: (b, 0, c, 0, 0)),
                                pl.BlockSpec((1, n_seq, GDN_CHUNK, wd), lambda b, c: (b, 0, c, C_ZB // wd)),
                                pl.BlockSpec((1, GDN_DV), lambda b, c: (0, 0))],
        out_specs=[pl.BlockSpec((1, n_seq, GDN_CHUNK, wd), lambda b, c: (b, 0, c, 0)),
                   pl.BlockSpec((1, n_seq, GDN_HEADS, GDN_DK, GDN_DV), lambda b, c: (b, 0, 0, 0, 0))],
        out_shape=[jax.ShapeDtypeStruct((nb, n_seq, t, wd), F32),
                   jax.ShapeDtypeStruct((nb, n_seq, GDN_HEADS, GDN_DK, GDN_DV), F32)],
        scratch_shapes=[pltpu.VMEM((n_seq, GDN_HEADS, GDN_DK, GDN_DV), F32)],
        compiler_params=_cparams(("parallel", "arbitrary")),
        name="gdn_inter",
    )(by_seq(u), by_seq(w), by_seq(att), by_seq(qd), by_seq(kd), by_seq(edl),
      h.reshape(nb, n_seq, t, h.shape[-1]), gnw.reshape(1, -1))
    return ob.reshape(bsz * t, wd), s_fin.reshape(bsz, GDN_HEADS, GDN_DK, GDN_DV)


def _out_kernel(oa_ref, ob_ref, ga_ref, gb_ref, x_ref, wpa_ref, wpb_ref, wo_ref, y_ref):
    ya = _dot(oa_ref[...].astype(BF16), wpa_ref[...])
    yb = _dot(ob_ref[...].astype(BF16), wpb_ref[...])
    m = jax.nn.sigmoid(ga_ref[...]) * ya + jax.nn.sigmoid(gb_ref[...]) * yb
    y_ref[...] = x_ref[...] + _dot(m.astype(BF16), wo_ref[...])


def _out_proj(oa, ob, h, x2d, w_pa, w_pb, w_o, tm):
    n, d = x2d.shape
    once = pl.Buffered(1)
    return pl.pallas_call(
        _out_kernel,
        grid=(n // tm,),
        in_specs=[pl.BlockSpec((tm, NSA_WIDTH), lambda i: (i, 0)),
                  pl.BlockSpec((tm, GDN_V_WIDTH), lambda i: (i, 0)),
                  pl.BlockSpec((tm, d), lambda i: (i, C_GA // d)),
                  pl.BlockSpec((tm, d), lambda i: (i, C_GB // d)),
                  pl.BlockSpec((tm, d), lambda i: (i, 0)),
                  pl.BlockSpec((NSA_WIDTH, d), lambda i: (0, 0), pipeline_mode=once),
                  pl.BlockSpec((GDN_V_WIDTH, d), lambda i: (0, 0), pipeline_mode=once),
                  pl.BlockSpec((d, d), lambda i: (0, 0), pipeline_mode=once)],
        out_specs=pl.BlockSpec((tm, d), lambda i: (i, 0)),
        out_shape=jax.ShapeDtypeStruct((n, d), F32),
        compiler_params=_cparams(("parallel",)),
        name="out_proj",
    )(oa, ob, h, h, x2d, w_pa, w_pb, w_o)


def _group_slopes(g):
    r_of = lax.broadcasted_iota(jnp.int32, (NSA_GROUP, 1), 0)
    slope = jnp.zeros((NSA_GROUP, 1), F32)
    for r in range(NSA_GROUP):
        s_r = jnp.where(g == 0, 2.0 ** -(r + 1.0), 2.0 ** -(r + 1.0 + NSA_GROUP)).astype(F32)
        slope = jnp.where(r_of == r, s_r, slope)
    return slope


def _attn_sample1_kernel(q_ref, kc_ref, vc_ref, selmap_ref, stw_ref, kvw_ref,
                         ocmp_ref, owin_ref, idx_ref, *, past_len, n_sel, n_sel_pad):
    qpos = past_len
    n_cmp = kc_ref.shape[2]
    wb = stw_ref.shape[1] // KV_ROWS
    rowi = lax.broadcasted_iota(jnp.int32, (LANE, n_sel_pad), 0)
    jrow = lax.broadcasted_iota(jnp.int32, (1, n_sel_pad), 1)
    scores = []
    for g in range(NSA_KV_HEADS):
        slope = _group_slopes(g)
        qg = q_ref[0, g].astype(BF16)
        kend = lax.broadcasted_iota(jnp.int32, (1, n_cmp), 1) * CMP_STRIDE + (CMP_LEN - 1)
        d = qpos - kend
        s = _dot_nt(qg, kc_ref[0, g].astype(BF16)) * SCALE - slope * d.astype(F32)
        p = _masked_softmax(s, d >= 0).astype(BF16)
        ocmp_ref[0, g] = _dot(p, vc_ref[0, g].astype(BF16))
        imp = jnp.sum(_dot(p, selmap_ref[...]), axis=0, keepdims=True)
        visible = jrow * SEL_BLOCK <= qpos
        back = qpos // SEL_BLOCK - jrow
        forced = (jrow == 0) | ((back >= 0) & (back < N_LOCAL_SEL))
        sc = jnp.where(visible, jnp.where(forced, FORCE_SCORE, imp), -FORCE_SCORE)
        scores.append(jnp.where(jrow < n_sel, sc, -3e38))
        kw = stw_ref[0, pl.ds(g, wb, stride=KV_ROWS), :].astype(BF16)
        vw = stw_ref[0, pl.ds(NSA_KV_HEADS + g, wb, stride=KV_ROWS), :].astype(BF16)
        knew = kvw_ref[0, g:g + 1, :].astype(BF16).astype(F32)
        vnew = kvw_ref[0, NSA_KV_HEADS + g:NSA_KV_HEADS + g + 1, :].astype(BF16).astype(F32)
        dwin = wb - lax.broadcasted_iota(jnp.int32, (1, wb), 1)
        sw = _dot_nt(qg, kw) * SCALE - slope * dwin.astype(F32)
        ok = (dwin >= 0) & (dwin <= WINDOW)
        sw = jnp.where(ok, sw, NEG)
        s_new = jnp.sum(qg.astype(F32) * knew, axis=-1, keepdims=True) * SCALE
        m = jnp.maximum(jnp.max(sw, axis=-1, keepdims=True), s_new)
        e = jnp.where(ok, jnp.exp(sw - m), 0.0)
        e_new = jnp.exp(s_new - m)
        den = jnp.sum(e, axis=-1, keepdims=True) + e_new
        owin_ref[0, g] = (_dot((e / den).astype(BF16), vw)
                          + (e_new / den).astype(BF16).astype(F32) * vnew)

    stacked = jnp.zeros((LANE, n_sel_pad), F32)
    for g in range(NSA_KV_HEADS):
        stacked = jnp.where(rowi == g, scores[g], stacked)
    cols = stacked.T
    ii = lax.broadcasted_iota(jnp.int32, (n_sel_pad, n_sel_pad), 0)
    jj = lax.broadcasted_iota(jnp.int32, (n_sel_pad, n_sel_pad), 1)
    earlier = (ii < jj).astype(F32)
    kcol = lax.broadcasted_iota(jnp.int32, (SEL_TOPK, 1), 0).astype(F32)
    for g in range(NSA_KV_HEADS):
        ci = cols[:, g:g + 1]
        rj = scores[g]
        beats = jnp.where(ci > rj, 1.0, jnp.where(ci == rj, earlier, 0.0))
        rank = jnp.sum(beats, axis=0, keepdims=True)
        onehot = rank == kcol
        idx = jnp.sum(jnp.where(onehot, jrow.astype(F32), 0.0), axis=-1, keepdims=True)
        idx_ref[0, g] = idx.astype(jnp.int32)


def _attn_sample1(q_s, kc_s, vc_s, st_win_rows, kvw_new, past_len):
    dbsz = q_s.shape[0]
    n_cmp = kc_s.shape[2]
    n_sel = -(-(past_len + 1) // SEL_BLOCK)
    n_sel_pad = -(-n_sel // LANE) * LANE
    assert n_sel >= SEL_TOPK
    selmap = _sel_map(n_cmp, n_sel, n_sel_pad)
    wrows = st_win_rows.shape[1]
    hshape = jax.ShapeDtypeStruct((dbsz, NSA_KV_HEADS, NSA_GROUP, HEAD_DIM), F32)
    hspec = pl.BlockSpec((1, NSA_KV_HEADS, NSA_GROUP, HEAD_DIM), lambda b: (b, 0, 0, 0))
    cspec = pl.BlockSpec((1, NSA_KV_HEADS, n_cmp, HEAD_DIM), lambda b: (b, 0, 0, 0))
    return pl.pallas_call(
        functools.partial(_attn_sample1_kernel, past_len=past_len, n_sel=n_sel, n_sel_pad=n_sel_pad),
        grid=(dbsz,),
        in_specs=[hspec, cspec, cspec,
                  pl.BlockSpec((n_cmp, n_sel_pad), lambda b: (0, 0)),
                  pl.BlockSpec((1, wrows, HEAD_DIM), lambda b: (b, 0, 0)),
                  pl.BlockSpec((1, 2 * NSA_KV_HEADS, HEAD_DIM), lambda b: (b, 0, 0))],
        out_specs=[hspec, hspec,
                   pl.BlockSpec((1, NSA_KV_HEADS, SEL_TOPK, 1), lambda b: (b, 0, 0, 0))],
        out_shape=[hshape, hshape,
                   jax.ShapeDtypeStruct((dbsz, NSA_KV_HEADS, SEL_TOPK, 1), jnp.int32)],
        compiler_params=_cparams(("parallel",)),
        name="attn_sample_cmp_win",
    )(q_s, kc_s, vc_s, selmap, st_win_rows, kvw_new)


BLOCK_ROWS = SEL_BLOCK * KV_ROWS


def _attn_sample2_kernel(idx_ref, pt_ref, q_ref, cache_ref, kvn_ref, ocmp_ref, owin_ref, gate_ref, za_ref,
                         o_ref, buf_ref, sem_ref, *, past_len, n_pages):
    i = pl.program_id(0)
    n_i = pl.num_programs(0)
    slot = i % 2
    g = i % NSA_KV_HEADS
    qpos = past_len
    n_past_blocks = past_len // SEL_BLOCK
    per_page = PAGE_SIZE // SEL_BLOCK
    n_keys = SEL_TOPK * SEL_BLOCK

    def block_copy(step, k, to_slot):
        blk = jnp.minimum(idx_ref[step * SEL_TOPK + k], n_past_blocks - 1)
        phys = pt_ref[(step // NSA_KV_HEADS) * n_pages + blk // per_page] * per_page + blk % per_page
        return pltpu.make_async_copy(cache_ref.at[pl.ds(phys * BLOCK_ROWS, BLOCK_ROWS)],
                                     buf_ref.at[to_slot, pl.ds(k * BLOCK_ROWS, BLOCK_ROWS)], sem_ref.at[to_slot])

    @pl.when(i == 0)
    def _():
        for k in range(SEL_TOPK):
            block_copy(0, k, 0).start()

    @pl.when(i + 1 < n_i)
    def _():
        for k in range(SEL_TOPK):
            block_copy(i + 1, k, 1 - slot).start()

    for k in range(SEL_TOPK):
        block_copy(i, k, slot).wait()

    slope = _group_slopes(g)
    q = q_ref[0, 0].astype(BF16)
    kk = buf_ref[slot, pl.ds(g, n_keys, stride=KV_ROWS), :].astype(BF16)
    vv = buf_ref[slot, pl.ds(NSA_KV_HEADS + g, n_keys, stride=KV_ROWS), :].astype(BF16)
    lane_blk = lax.broadcasted_iota(jnp.int32, (1, n_keys), 1) // SEL_BLOCK
    tok = lax.broadcasted_iota(jnp.int32, (1, n_keys), 1) % SEL_BLOCK
    for k in range(SEL_TOPK):
        blk = idx_ref[i * SEL_TOPK + k]
        tok = tok + jnp.where(lane_blk == k, jnp.where(blk < n_past_blocks, blk * SEL_BLOCK, qpos + 1), 0)
    dd = qpos - tok
    mask = dd >= 0
    sc = jnp.where(mask, _dot_nt(q, kk) * SCALE - slope * dd.astype(F32), NEG)
    knew = kvn_ref[0, 0, 0].astype(BF16).astype(F32)
    vnew = kvn_ref[0, 1, 0].astype(BF16).astype(F32)
    s_new = jnp.sum(q.astype(F32) * knew, axis=-1, keepdims=True) * SCALE
    m = jnp.maximum(jnp.max(sc, axis=-1, keepdims=True), s_new)
    p = jnp.where(mask, jnp.exp(sc - m), 0.0)
    p_new = jnp.exp(s_new - m)
    den = jnp.sum(p, axis=-1, keepdims=True) + p_new
    o_slc = (_dot(p.astype(BF16), vv) + p_new.astype(BF16).astype(F32) * vnew) / den
    tot = (jax.nn.sigmoid(gate_ref[0, 0, 0]) * ocmp_ref[0, 0]
           + jax.nn.sigmoid(gate_ref[0, 1, 0]) * o_slc
           + jax.nn.sigmoid(gate_ref[0, 2, 0]) * owin_ref[0, 0])
    o_ref[0, 0] = tot * _silu(za_ref[0, 0])


def _attn_sample2(idx, page_table, q_s, cache_rows, kvs_new, o_cmp, o_win, gates, za_s, past_len):
    dbsz, n_pages = page_table.shape
    ng = NSA_KV_HEADS
    head = pl.BlockSpec((1, 1, NSA_GROUP, HEAD_DIM), lambda i, x, p: (i // ng, i % ng, 0, 0))
    grid_spec = pltpu.PrefetchScalarGridSpec(
        num_scalar_prefetch=2,
        grid=(dbsz * ng,),
        in_specs=[head,
                  pl.BlockSpec(memory_space=pl.ANY),
                  pl.BlockSpec((1, 2, 1, 1, HEAD_DIM), lambda i, x, p: (i // ng, 0, i % ng, 0, 0)),
                  head, head,
                  pl.BlockSpec((1, 3, 1, NSA_GROUP, 1), lambda i, x, p: (i // ng, 0, i % ng, 0, 0)),
                  head],
        out_specs=head,
        scratch_shapes=[pltpu.VMEM((2, SEL_TOPK * BLOCK_ROWS, HEAD_DIM), F32),
                        pltpu.SemaphoreType.DMA((2,))],
    )
    return pl.pallas_call(
        functools.partial(_attn_sample2_kernel, past_len=past_len, n_pages=n_pages),
        grid_spec=grid_spec,
        out_shape=jax.ShapeDtypeStruct((dbsz, ng, NSA_GROUP, HEAD_DIM), F32),
        compiler_params=_cparams(("arbitrary",)),
        name="attn_sample_slc",
    )(idx.reshape(-1), page_table.reshape(-1), q_s, cache_rows, kvs_new, o_cmp, o_win, gates, za_s)


def _gdn_sample_kernel(x_ref, cst_ref, cw_ref, a_ref, b_ref, alog_ref, dtb_ref, s_ref, gnw_ref, zb_ref,
                       o_ref, s_out_ref):
    acc = None
    for i in range(CONV_W):
        row = cst_ref[0, i] if i < CONV_W - 1 else x_ref[0]
        term = row * cw_ref[i]
        acc = term if acc is None else acc + term
    c = _silu(acc)
    q = _l2(c[0:GDN_HEADS]) * (GDN_DK ** -0.5)
    k = _l2(c[GDN_HEADS:2 * GDN_HEADS])
    v = c[2 * GDN_HEADS:3 * GDN_HEADS]
    gdec = -jnp.exp(alog_ref[...]) * _softplus(a_ref[0] + dtb_ref[...])
    beta = jax.nn.sigmoid(b_ref[0])
    eg = jnp.exp(gdec)
    qk = jnp.sum(q * k, axis=-1, keepdims=True)
    pad = jnp.concatenate([q, k, jnp.zeros((LANE - 2 * GDN_HEADS, GDN_DK), F32)], axis=0)
    cols = pad.T
    outs = []
    for hh in range(GDN_HEADS):
        s_h = s_ref[0, hh]
        qcol = cols[:, hh:hh + 1]
        kcol = cols[:, GDN_HEADS + hh:GDN_HEADS + hh + 1]
        b_h = beta[hh:hh + 1]
        e_h = eg[hh:hh + 1]
        k_s = jnp.sum(kcol * s_h, axis=0, keepdims=True)
        q_s = jnp.sum(qcol * s_h, axis=0, keepdims=True)
        vn = b_h * v[hh:hh + 1] - (b_h * e_h) * k_s
        outs.append(e_h * q_s + qk[hh:hh + 1] * vn)
        s_out_ref[0, hh] = s_h * e_h + kcol * vn
    o = jnp.concatenate(outs, axis=0)
    o_ref[0] = _rms(o, gnw_ref[...]) * _silu(zb_ref[0])


def _gdn_sample(x_s, cst, cw, a_s, b_s, a_log, dt_bias, s0, gnw, zb_s):
    dbsz = x_s.shape[0]
    nh3 = 3 * GDN_HEADS
    col = pl.BlockSpec((1, GDN_HEADS, 1), lambda b: (b, 0, 0))
    par = pl.BlockSpec((GDN_HEADS, 1), lambda b: (0, 0))
    head = pl.BlockSpec((1, GDN_HEADS, GDN_DV), lambda b: (b, 0, 0))
    sspec = pl.BlockSpec((1, GDN_HEADS, GDN_DK, GDN_DV), lambda b: (b, 0, 0, 0))
    return pl.pallas_call(
        _gdn_sample_kernel,
        grid=(dbsz,),
        in_specs=[pl.BlockSpec((1, nh3, GDN_DK), lambda b: (b, 0, 0)),
                  pl.BlockSpec((1, CONV_W - 1, nh3, GDN_DK), lambda b: (b, 0, 0, 0)),
                  pl.BlockSpec((CONV_W, nh3, GDN_DK), lambda b: (0, 0, 0)),
                  col, col, par, par, sspec,
                  pl.BlockSpec((1, GDN_DV), lambda b: (0, 0)),
                  head],
        out_specs=[head, sspec],
        out_shape=[jax.ShapeDtypeStruct((dbsz, GDN_HEADS, GDN_DV), F32),
                   jax.ShapeDtypeStruct(s0.shape, F32)],
        compiler_params=_cparams(("parallel",)),
        name="gdn_sample",
    )(x_s, cst, cw, a_s, b_s, a_log.reshape(-1, 1), dt_bias.reshape(-1, 1), s0, gnw.reshape(1, -1), zb_s)


def _permute_w_in(w_in):
    splits = (NSA_WIDTH, 2 * NSA_KV_WIDTH, 2 * NSA_KV_WIDTH, 2 * NSA_KV_WIDTH, 3 * NSA_HEADS, NSA_WIDTH,
              GDN_CONV_CH, GDN_HEADS, GDN_HEADS, GDN_V_WIDTH, w_in.shape[0], w_in.shape[0])
    bounds = np.cumsum(np.array(splits))[:-1].tolist()
    (q_a, kvc, kvs, kvw, g_nsa, z_a, qkv_b, a_b, b_b, z_b, gate_a, gate_b) = jnp.split(w_in, bounds, axis=1)
    small_pad = jnp.zeros((w_in.shape[0], H_WIDTH - C_SM - SM_B - GDN_HEADS), w_in.dtype)
    return jnp.concatenate([q_a, z_a, z_b, qkv_b, gate_a, gate_b, kvc, kvs, kvw, g_nsa, a_b, b_b, small_pad],
                           axis=1).astype(BF16)


def _cmp_w1(w1):
    return jnp.transpose(w1, (1, 2, 0, 3)).reshape(CMP_STRIDE, HEAD_DIM, CMP_R * CMP_HIDDEN).astype(BF16)


def _layer(x_p, x_s, cache_cmp, cache_slc, layer, st_win, st_s, st_conv, page_table, w):
    (norm_w, w_in, q_norm_w, k_norm_cmp_w, k_norm_slc_w, k_norm_win_w,
     cmp_w1_k, cmp_b1_k, cmp_w2_k, cmp_w1_v, cmp_b1_v, cmp_w2_v,
     conv_w, a_log, dt_bias, gdn_norm_w, w_pa, w_pb, w_o) = w
    bsz, t, d = x_p.shape
    dbsz, dseq, _ = x_s.shape
    assert dseq == 1 and d == C_GB - C_GA
    n_pages = page_table.shape[1]
    past_len = n_pages * PAGE_SIZE

    w_perm = _permute_w_in(w_in)
    head_w = jnp.stack([q_norm_w, k_norm_slc_w, k_norm_win_w])
    w1k, w1v = _cmp_w1(cmp_w1_k), _cmp_w1(cmp_w1_v)
    w2k, w2v = cmp_w2_k.astype(BF16), cmp_w2_v.astype(BF16)
    wpa, wpb, wo = w_pa.astype(BF16), w_pb.astype(BF16), w_o.astype(BF16)
    kvw_cols = 2 * NSA_KV_WIDTH

    n = bsz * t
    x2 = x_p.reshape(n, d)
    h, kvc_rows, kvs_rows, kvw_rows = _proj(x2, norm_w, w_perm, head_w, tm=min(1024, n))
    pk, pv = _cmp_part_prompt(h, w1k, w1v, bsz, t)
    kc, vc = _cmp_finish(pk, pv, cmp_b1_k, w2k, cmp_b1_v, w2v, k_norm_cmp_w)
    oa = _attn_prompt(h, kc, vc, bsz, t)
    nc = t // GDN_CHUNK
    ab5 = h[:, C_SM + SM_A:C_SM + SM_B + GDN_HEADS].reshape(bsz * nc, GDN_CHUNK, 2, GDN_PAIRS, GDN_PAIR)
    col_form = lambda x: x.transpose(0, 3, 1, 2).reshape(bsz * nc, GDN_PROWS, GDN_PAIRS)
    a_row = ab5[:, :, 0].transpose(0, 2, 3, 1).reshape(bsz * nc, GDN_PAIRS, GDN_PROWS)
    u, wmat, att, qd, kd, edl = _gdn_intra(h, conv_w, col_form(ab5[:, :, 0]), col_form(ab5[:, :, 1]), a_row,
                                           a_log, dt_bias, bsz, t)
    ob, s_p = _gdn_inter(u, wmat, att, qd, kd, edl, h, gdn_norm_w, bsz, t)
    y_p = _out_proj(oa, ob, h, x2, wpa, wpb, wo, tm=min(256, n)).reshape(bsz, t, d)

    kv_shape = (bsz, t, 2, NSA_KV_HEADS, HEAD_DIM)
    cmp_p = kvc_rows.reshape(kv_shape)
    slc_p = kvs_rows.reshape(kv_shape)
    wlen = min(WINDOW, t)
    win_p = kvw_rows.reshape(kv_shape)[:, t - wlen:]
    conv_p = h.reshape(bsz, t, H_WIDTH)[:, t - (CONV_W - 1):, C_QKVB:C_QKVB + GDN_CONV_CH]

    xs2 = x_s.reshape(dbsz, d)
    hs, kvc_s, kvs_s, kvw_s = _proj(xs2, norm_w, w_perm, head_w, tm=dbsz)
    grp = (dbsz, NSA_KV_HEADS, NSA_GROUP, HEAD_DIM)
    q_s = hs[:, C_Q:C_Q + NSA_WIDTH].reshape(grp)
    za_s = hs[:, C_ZA:C_ZA + NSA_WIDTH].reshape(grp)
    pt_phys = page_table + layer * cache_cmp.shape[1]
    pk_s, pv_s = _cmp_part_sample(cache_cmp.reshape(-1, HEAD_DIM), pt_phys, w1k, w1v)
    kc_s, vc_s = _cmp_finish(pk_s, pv_s, cmp_b1_k, w2k, cmp_b1_v, w2v, k_norm_cmp_w)
    wb = st_win.shape[1]
    o_cmp_s, o_win_s, idx = _attn_sample1(q_s, kc_s, vc_s, st_win.reshape(dbsz, wb * KV_ROWS, HEAD_DIM),
                                          kvw_s.reshape(dbsz, 2 * NSA_KV_HEADS, HEAD_DIM), past_len)
    gates_s = hs[:, C_SM:C_SM + 3 * NSA_HEADS].reshape(dbsz, 3, NSA_KV_HEADS, NSA_GROUP, 1)
    oa_s = _attn_sample2(idx, pt_phys, q_s, cache_slc.reshape(-1, HEAD_DIM),
                         kvs_s.reshape(dbsz, 2, NSA_KV_HEADS, 1, HEAD_DIM), o_cmp_s, o_win_s, gates_s, za_s,
                         past_len)
    nh3 = 3 * GDN_HEADS
    ob_s, s_s = _gdn_sample(hs[:, C_QKVB:C_QKVB + GDN_CONV_CH].reshape(dbsz, nh3, GDN_DK),
                            st_conv.reshape(dbsz, CONV_W - 1, nh3, GDN_DK),
                            conv_w.reshape(CONV_W, nh3, GDN_DK),
                            hs[:, C_SM + SM_A:C_SM + SM_A + GDN_HEADS].reshape(dbsz, GDN_HEADS, 1),
                            hs[:, C_SM + SM_B:C_SM + SM_B + GDN_HEADS].reshape(dbsz, GDN_HEADS, 1),
                            a_log, dt_bias, st_s, gdn_norm_w,
                            hs[:, C_ZB:C_ZB + GDN_V_WIDTH].reshape(dbsz, GDN_HEADS, GDN_DV))
    y_s = _out_proj(oa_s.reshape(dbsz, NSA_WIDTH), ob_s.reshape(dbsz, GDN_V_WIDTH), hs, xs2,
                    wpa, wpb, wo, tm=dbsz).reshape(dbsz, 1, d)

    kv1 = (dbsz, 1, 2, NSA_KV_HEADS, HEAD_DIM)
    win_s = jnp.concatenate([st_win, kvw_s.reshape(kv1)], axis=1)[:, -wb:]
    conv_s = jnp.concatenate([st_conv, hs[:, None, C_QKVB:C_QKVB + GDN_CONV_CH]], axis=1)[:, -(CONV_W - 1):]
    return y_p, y_s, (cmp_p, slc_p, win_p, s_p, conv_p,
                      kvc_s.reshape(kv1), kvs_s.reshape(kv1), win_s, s_s, conv_s)


def _take_layer(x, layer):
    return x.reshape(x.shape[1:]) if x.shape[0] == 1 else x[layer]


def kernel(x_prompt, x_sample, cache_cmp_kv, cache_slc_kv, state_win_kv, state_gdn_S, state_gdn_conv, page_table, norm_w, w_in, q_norm_w, k_norm_cmp_w, k_norm_slc_w, k_norm_win_w, cmp_w1_k, cmp_b1_k, cmp_w2_k, cmp_w1_v, cmp_b1_v, cmp_w2_v, conv_w, a_log, dt_bias, gdn_norm_w, w_pa, w_pb, w_o):
    weights = (norm_w, w_in, q_norm_w, k_norm_cmp_w, k_norm_slc_w, k_norm_win_w,
               cmp_w1_k, cmp_b1_k, cmp_w2_k, cmp_w1_v, cmp_b1_v, cmp_w2_v,
               conv_w, a_log, dt_bias, gdn_norm_w, w_pa, w_pb, w_o)
    depth = norm_w.shape[0]
    y_p, y_s = x_prompt, x_sample
    per_layer = []
    for layer in range(depth):
        y_p, y_s, st = _layer(y_p, y_s, cache_cmp_kv, cache_slc_kv, layer, _take_layer(state_win_kv, layer),
                              _take_layer(state_gdn_S, layer), _take_layer(state_gdn_conv, layer), page_table,
                              tuple(_take_layer(wt, layer) for wt in weights))
        per_layer.append(st)
    stacked = [jnp.stack(a) for a in zip(*per_layer)]
    return (y_p, y_s, *stacked)
```
